```python
import math
import jax, jax.numpy as jnp
from jax import lax
import numpy as np

D_MODEL = 1024
BATCH = 8
SEQ = 4096
DEPTH = 2
DEC_BATCH = 32
DEC_SEQ = 1
PAST_LEN = 16384
PAGE_SIZE = 128

N_MIXERS = 2
N_GMLP = (DEPTH + 1) // 2
N_ATT = DEPTH // 2

CHUNK = 128
GMLP_WIDTH = 2 * D_MODEL
GMLP_GROUPS = 8
GMLP_GROUP_DIM = GMLP_WIDTH // GMLP_GROUPS

N_HEADS = 16
HEAD_DIM = D_MODEL // N_HEADS
N_KV_HEADS = 4
GROUP = N_HEADS // N_KV_HEADS
IDX_HEADS = 8
IDX_DIM = 64
TOPK_MAX = 256
Q_BLOCK = 128
ROPE_THETA = 10000.0
IDX_SCALE = (IDX_DIM ** -0.5) * (IDX_HEADS ** -0.5)
ATT_SIZES = (N_HEADS * HEAD_DIM, N_KV_HEADS * HEAD_DIM, N_KV_HEADS * HEAD_DIM,
             IDX_HEADS * IDX_DIM, IDX_DIM, IDX_HEADS)
ATT_PROJ = sum(ATT_SIZES)

PEER_HEADS = 8
N_KEYS = 128
N_EXPERTS = N_KEYS * N_KEYS
PEER_TOPK = 16
PEER_KEY_DIM = 256
PEER_HALF = PEER_KEY_DIM // 2
PEER_BLOCK = 128

EPS = 1e-6

kernel_name = "gmlp_dsa_peer_hybrid_step"


def _rms(x, g):
    xf = x.astype(jnp.float32)
    y = xf * lax.rsqrt(jnp.mean(xf * xf, axis=-1, keepdims=True) + EPS)
    return (y * g.astype(jnp.float32)).astype(x.dtype)


def _rope(x, pos):
    half = x.shape[-1] // 2
    inv = jnp.exp(-math.log(ROPE_THETA) * jnp.arange(half, dtype=jnp.float32) / half)
    ang = pos.astype(jnp.float32)[:, None] * inv[None, :]
    cos = jnp.cos(ang)[:, None, :]
    sin = jnp.sin(ang)[:, None, :]
    xf = x.astype(jnp.float32)
    x1, x2 = xf[..., :half], xf[..., half:]
    return jnp.concatenate([x1 * cos - x2 * sin, x2 * cos + x1 * sin], axis=-1).astype(x.dtype)


def _gmlp_mix(h, w_in, v_g, w_s, b_s, w_out):
    B, T, _ = h.shape
    uv = h @ w_in
    u, v = uv[..., :GMLP_WIDTH], uv[..., GMLP_WIDTH:]
    v = _rms(v, v_g)
    n_chunks = -(-T // CHUNK)
    pad = n_chunks * CHUNK - T
    vp = jnp.pad(v, ((0, 0), (0, pad), (0, 0))).reshape(B, n_chunks, CHUNK, GMLP_GROUPS, GMLP_GROUP_DIM)
    causal = jnp.tril(jnp.ones((CHUNK, CHUNK), dtype=bool))
    ws = jnp.where(causal, w_s, 0)
    mixed = jnp.einsum('gts,bcsgd->bctgd', ws, vp) + b_s.T[None, None, :, :, None]
    mixed = mixed.reshape(B, n_chunks * CHUNK, GMLP_WIDTH)[:, :T]
    return (u * mixed) @ w_out, v


def _att_project(h, w_in, qn_g, kn_g, ikn_g, pos):
    B, T, _ = h.shape
    p = h @ w_in
    splits = [int(s) for s in np.cumsum(ATT_SIZES)[:-1]]
    q, k, v, qi, ki, wi = jnp.split(p, splits, axis=-1)
    q = _rope(_rms(q.reshape(B, T, N_HEADS, HEAD_DIM), qn_g), pos)
    k = _rope(_rms(k.reshape(B, T, N_KV_HEADS, HEAD_DIM), kn_g), pos)
    v = v.reshape(B, T, N_KV_HEADS, HEAD_DIM)
    qi = _rope(qi.reshape(B, T, IDX_HEADS, IDX_DIM), pos)
    ki = _rope(_rms(ki, ikn_g)[:, :, None, :], pos)[:, :, 0]
    return q, k, v, qi, ki, wi


def _sparse_attend(q, qi, wi, ki_all, q_pos, n_keep, gather_kv):
    B, Tq = q.shape[0], q.shape[1]
    s = jax.nn.relu(jnp.einsum('bthd,bsd->bths', qi, ki_all))
    score = jnp.einsum('bths,bth->bts', s, wi).astype(jnp.float32) * IDX_SCALE
    key_pos = jnp.arange(ki_all.shape[1])
    admissible = key_pos[None, :] <= q_pos[:, None]
    score = jnp.where(admissible[None], score, -jnp.inf)
    _, idx = lax.top_k(score, n_keep)
    sel_ok = idx <= q_pos[None, :, None]
    k_sel, v_sel = gather_kv(idx)
    qg = q.reshape(B, Tq, N_KV_HEADS, GROUP, HEAD_DIM)
    logits = jnp.einsum('bthgd,btnhd->bthgn', qg, k_sel).astype(jnp.float32) * (HEAD_DIM ** -0.5)
    logits = jnp.where(sel_ok[:, :, None, None, :], logits, -jnp.inf)
    prob = jax.nn.softmax(logits, axis=-1).astype(v_sel.dtype)
    o = jnp.einsum('bthgn,btnhd->bthgd', prob, v_sel)
    return o.reshape(B, Tq, N_HEADS * HEAD_DIM)


def _take_rows(a, idx):
    return jax.vmap(lambda ab, ib: ab[ib])(a, idx)


def _att_prompt(h, w_in, qn_g, kn_g, ikn_g, w_out):
    B, S, _ = h.shape
    pos = jnp.arange(S)
    q, k, v, qi, ki, wi = _att_project(h, w_in, qn_g, kn_g, ikn_g, pos)
    n_keep = min(TOPK_MAX, S // 4)
    nb = S // Q_BLOCK

    def gather(idx):
        return _take_rows(k, idx), _take_rows(v, idx)

    def blocks(a):
        return a.reshape((B, nb, Q_BLOCK) + a.shape[2:]).swapaxes(0, 1)

    def one_block(args):
        qb, qib, wib, posb = args
        return _sparse_attend(qb, qib, wib, ki, posb, n_keep, gather)

    o = lax.map(one_block, (blocks(q), blocks(qi), blocks(wi), pos.reshape(nb, Q_BLOCK)))
    o = o.swapaxes(0, 1).reshape(B, S, N_HEADS * HEAD_DIM)
    return o @ w_out, k, v, ki


def _att_sample(h, ck, cv, cik, page_table, w_in, qn_g, kn_g, ikn_g, w_out):
    B, T, _ = h.shape
    pos = PAST_LEN + jnp.arange(T)
    q, k, v, qi, ki, wi = _att_project(h, w_in, qn_g, kn_g, ikn_g, pos)
    n_pages = PAST_LEN // PAGE_SIZE
    ki_past = cik[page_table].reshape(B, n_pages * PAGE_SIZE, IDX_DIM)
    ki_all = jnp.concatenate([ki_past, ki], axis=1)
    n_keep = min(TOPK_MAX, (PAST_LEN + T) // 4)

    def gather(idx):
        is_new = (idx >= PAST_LEN)[..., None, None]
        past = jnp.minimum(idx, PAST_LEN - 1)
        phys = jnp.take_along_axis(page_table, (past // PAGE_SIZE).reshape(B, -1), axis=1).reshape(idx.shape)
        off = past % PAGE_SIZE
        new = jnp.clip(idx - PAST_LEN, 0, T - 1)
        k_sel = jnp.where(is_new, _take_rows(k, new), ck[phys, off])
        v_sel = jnp.where(is_new, _take_rows(v, new), cv[phys, off])
        return k_sel, v_sel

    o = _sparse_attend(q, qi, wi, ki_all, pos, n_keep, gather)
    return o @ w_out, k, v, ki


def _peer(h, w_q, sub_k1, sub_k2, expert_u, expert_v):
    shape = h.shape
    x = h.reshape(-1, D_MODEL)
    n = x.shape[0]
    nb = -(-n // PEER_BLOCK)
    xp = jnp.pad(x, ((0, nb * PEER_BLOCK - n), (0, 0))).reshape(nb, PEER_BLOCK, D_MODEL)

    def one_block(xb):
        q = (xb @ w_q).reshape(PEER_BLOCK, PEER_HEADS, 2, PEER_HALF)
        s1 = jnp.einsum('thd,kd->thk', q[:, :, 0], sub_k1).astype(jnp.float32)
        s2 = jnp.einsum('thd,kd->thk', q[:, :, 1], sub_k2).astype(jnp.float32)
        v1, i1 = lax.top_k(s1, PEER_TOPK)
        v2, i2 = lax.top_k(s2, PEER_TOPK)
        cand = (v1[..., :, None] + v2[..., None, :]).reshape(PEER_BLOCK, PEER_HEADS, PEER_TOPK * PEER_TOPK)
        cand_idx = (i1[..., :, None] * N_KEYS + i2[..., None, :]).reshape(PEER_BLOCK, PEER_HEADS, PEER_TOPK * PEER_TOPK)
        top_s, top_pos = lax.top_k(cand, PEER_TOPK)
        eidx = jnp.take_along_axis(cand_idx, top_pos, axis=-1)
        gate = jax.nn.softmax(top_s, axis=-1)
        u = expert_u[eidx]
        act = jax.nn.gelu(jnp.einsum('thkd,td->thk', u, xb))
        vv = expert_v[eidx]
        return jnp.einsum('thk,thkd->td', (gate * act).astype(vv.dtype), vv)

    y = lax.map(one_block, xp).reshape(-1, D_MODEL)[:n]
    return y.reshape(shape)


def setup_inputs(seed: int = 0) -> dict:
    key = jax.random.key(seed)
    ks = jax.random.split(key, 24)
    n_pages = PAST_LEN // PAGE_SIZE
    n_used = DEC_BATCH * n_pages
    n_pool = (5 * n_used + 3) // 4
    f32 = jnp.float32

    def nrm(k, shape, scale):
        return jax.random.normal(k, shape, f32) * scale

    def gain(k, shape):
        return 1.0 + 0.02 * jax.random.normal(k, shape, f32)

    page_table = jax.random.permutation(ks[0], n_pool)[:n_used].reshape(DEC_BATCH, n_pages).astype(jnp.int32)
    return {
        "x_prompt": nrm(ks[1], (BATCH, SEQ, D_MODEL), 1.0),
        "x_sample": nrm(ks[2], (DEC_BATCH, DEC_SEQ, D_MODEL), 1.0),
        "cache_k": nrm(ks[3], (N_ATT, n_pool, PAGE_SIZE, N_KV_HEADS, HEAD_DIM), 1.0),
        "cache_v": nrm(ks[4], (N_ATT, n_pool, PAGE_SIZE, N_KV_HEADS, HEAD_DIM), 1.0),
        "cache_idx_k": nrm(ks[5], (N_ATT, n_pool, PAGE_SIZE, IDX_DIM), 1.0),
        "page_table": page_table,
        "norm_mix_g": gain(ks[6], (DEPTH, D_MODEL)),
        "norm_ffn_g": gain(ks[7], (DEPTH, D_MODEL)),
        "gmlp_w_in": nrm(ks[8], (N_GMLP, D_MODEL, 2 * GMLP_WIDTH), D_MODEL ** -0.5),
        "gmlp_v_g": gain(ks[9], (N_GMLP, GMLP_WIDTH)),
        "gmlp_w_s": nrm(ks[10], (N_GMLP, GMLP_GROUPS, CHUNK, CHUNK), CHUNK ** -0.5),
        "gmlp_b_s": gain(ks[11], (N_GMLP, GMLP_GROUPS, CHUNK)),
        "gmlp_w_out": nrm(ks[12], (N_GMLP, GMLP_WIDTH, D_MODEL), 0.5 * GMLP_WIDTH ** -0.5),
        "att_w_in": nrm(ks[13], (N_ATT, D_MODEL, ATT_PROJ), D_MODEL ** -0.5),
        "att_q_norm_g": gain(ks[14], (N_ATT, HEAD_DIM)),
        "att_k_norm_g": gain(ks[15], (N_ATT, HEAD_DIM)),
        "att_idx_k_norm_g": gain(ks[16], (N_ATT, IDX_DIM)),
        "att_w_out": nrm(ks[17], (N_ATT, N_HEADS * HEAD_DIM, D_MODEL), (N_HEADS * HEAD_DIM) ** -0.5),
        "peer_w_q": nrm(ks[18], (DEPTH, D_MODEL, PEER_HEADS * PEER_KEY_DIM), D_MODEL ** -0.5),
        "peer_sub_k1": nrm(ks[19], (DEPTH, N_KEYS, PEER_HALF), PEER_HALF ** -0.5),
        "peer_sub_k2": nrm(ks[20], (DEPTH, N_KEYS, PEER_HALF), PEER_HALF ** -0.5),
        "peer_u": nrm(ks[21], (DEPTH, N_EXPERTS, D_MODEL), D_MODEL ** -0.5),
        "peer_v": nrm(ks[22], (DEPTH, N_EXPERTS, D_MODEL), 0.1),
    }


def reference(x_prompt, x_sample, cache_k, cache_v, cache_idx_k, page_table,
              norm_mix_g, norm_ffn_g,
              gmlp_w_in, gmlp_v_g, gmlp_w_s, gmlp_b_s, gmlp_w_out,
              att_w_in, att_q_norm_g, att_k_norm_g, att_idx_k_norm_g, att_w_out,
              peer_w_q, peer_sub_k1, peer_sub_k2, peer_u, peer_v):
    xp, xs = x_prompt, x_sample
    kp_l, vp_l, ip_l, ks_l, vs_l, is_l, cv_l = [], [], [], [], [], [], []
    for layer in range(DEPTH):
        a = layer // N_MIXERS
        hp = _rms(xp, norm_mix_g[layer])
        hs = _rms(xs, norm_mix_g[layer])
        if layer % N_MIXERS == 0:
            op, _ = _gmlp_mix(hp, gmlp_w_in[a], gmlp_v_g[a], gmlp_w_s[a], gmlp_b_s[a], gmlp_w_out[a])
            os_, v_rows = _gmlp_mix(hs, gmlp_w_in[a], gmlp_v_g[a], gmlp_w_s[a], gmlp_b_s[a], gmlp_w_out[a])
            cv_l.append(v_rows)
        else:
            op, kp, vp, ip = _att_prompt(hp, att_w_in[a], att_q_norm_g[a], att_k_norm_g[a],
                                         att_idx_k_norm_g[a], att_w_out[a])
            os_, ksm, vsm, ism = _att_sample(hs, cache_k[a], cache_v[a], cache_idx_k[a], page_table,
                                             att_w_in[a], att_q_norm_g[a], att_k_norm_g[a],
                                             att_idx_k_norm_g[a], att_w_out[a])
            kp_l.append(kp); vp_l.append(vp); ip_l.append(ip)
            ks_l.append(ksm); vs_l.append(vsm); is_l.append(ism)
        xp = xp + op
        xs = xs + os_
        xp = xp + _peer(_rms(xp, norm_ffn_g[layer]), peer_w_q[layer], peer_sub_k1[layer],
                        peer_sub_k2[layer], peer_u[layer], peer_v[layer])
        xs = xs + _peer(_rms(xs, norm_ffn_g[layer]), peer_w_q[layer], peer_sub_k1[layer],
                        peer_sub_k2[layer], peer_u[layer], peer_v[layer])
    new_k_prompt = jnp.stack(kp_l)
    new_v_prompt = jnp.stack(vp_l)
    new_idx_k_prompt = jnp.stack(ip_l)
    new_k_sample = jnp.stack(ks_l)
    new_v_sample = jnp.stack(vs_l)
    new_idx_k_sample = jnp.stack(is_l)
    new_chunk_v_sample = jnp.stack(cv_l)
    return (xp, xs, new_k_prompt, new_v_prompt, new_idx_k_prompt,
            new_k_sample, new_v_sample, new_idx_k_sample, new_chunk_v_sample)
```

```python
import functools
import math

import jax
import jax.numpy as jnp
import numpy as np
from jax import lax
from jax.experimental import pallas as pl
from jax.experimental.pallas import tpu as pltpu

F32 = jnp.float32
BF16 = jnp.bfloat16
I32 = jnp.int32

D_MODEL = 1024
EPS = 1e-6

LANES = 128
SUBLANES = 8
VMEM_BYTES_V7X = 64 * 1024 * 1024

PEER_HEADS = 8
N_KEYS = 128
N_EXPERTS = N_KEYS * N_KEYS
PEER_TOPK = 16
PEER_KEY_DIM = 256
PEER_HALF = PEER_KEY_DIM // 2
PEER_PICKS = PEER_HEADS * PEER_TOPK
ROW_WORDS = D_MODEL // 2
ROW_SUB = ROW_WORDS // LANES
PEER_TOKENS = 128


def _nt_dot(a, b, **kw):
    return lax.dot_general(a, b, (((1,), (1,)), ((), ())), **kw)


def _topk_rows(s, k, iota):
    n_rows = s.shape[0]
    vals, idxs = [], []
    for _ in range(k):
        m = jnp.max(s, axis=0, keepdims=True)
        first = jnp.min(jnp.where(s == m, iota, n_rows), axis=0, keepdims=True)
        vals.append(m)
        idxs.append(first)
        s = jnp.where(iota == first, -jnp.inf, s)
    return jnp.concatenate(vals, axis=0), jnp.concatenate(idxs, axis=0)


def _peer_router_kernel(x_ref, g_ref, wq_ref, k1_ref, k2_ref, h_ref, eidx_ref, gate_ref):
    x = x_ref[...]
    h = x * lax.rsqrt(jnp.mean(x * x, axis=-1, keepdims=True) + EPS) * g_ref[...]
    h_ref[...] = h
    q = jnp.dot(h.astype(BF16), wq_ref[...], preferred_element_type=F32)
    n_tok = x.shape[0]
    iota_k = lax.broadcasted_iota(I32, (N_KEYS, n_tok), 0)
    iota_c = lax.broadcasted_iota(I32, (PEER_TOPK * PEER_TOPK, n_tok), 0)
    k1 = k1_ref[...]
    k2 = k2_ref[...]
    eidx_rows, gate_rows = [], []
    for head in range(PEER_HEADS):
        base = head * PEER_KEY_DIM
        q1 = q[:, base:base + PEER_HALF].astype(BF16)
        q2 = q[:, base + PEER_HALF:base + PEER_KEY_DIM].astype(BF16)
        s1 = _nt_dot(k1, q1, preferred_element_type=F32)
        s2 = _nt_dot(k2, q2, preferred_element_type=F32)
        v1, i1 = _topk_rows(s1, PEER_TOPK, iota_k)
        v2, i2 = _topk_rows(s2, PEER_TOPK, iota_k)
        cand = jnp.concatenate([v1[a:a + 1] + v2 for a in range(PEER_TOPK)], axis=0)
        cidx = jnp.concatenate([i1[a:a + 1] * N_KEYS + i2 for a in range(PEER_TOPK)], axis=0)
        top_s, top_e = [], []
        for _ in range(PEER_TOPK):
            m = jnp.max(cand, axis=0, keepdims=True)
            first = jnp.min(jnp.where(cand == m, iota_c, PEER_TOPK * PEER_TOPK), axis=0, keepdims=True)
            hit = iota_c == first
            top_s.append(m)
            top_e.append(jnp.sum(jnp.where(hit, cidx, 0), axis=0, keepdims=True))
            cand = jnp.where(hit, -jnp.inf, cand)
        top_s = jnp.concatenate(top_s, axis=0)
        e = jnp.exp(top_s - top_s[0:1])
        gate_rows.append(e / jnp.sum(e, axis=0, keepdims=True))
        eidx_rows.append(jnp.concatenate(top_e, axis=0))
    gate_t = jnp.concatenate(gate_rows, axis=0)
    eidx_t = jnp.concatenate(eidx_rows, axis=0)
    gate_ref[...] = gate_t.T
    eidx_ref[...] = lax.bitcast_convert_type(lax.bitcast_convert_type(eidx_t, F32).T, I32)


def _peer_router(x, g, w_q, sub_k1, sub_k2):
    n = x.shape[0]
    tm = PEER_TOKENS
    full = lambda shape: pl.BlockSpec(shape, lambda i: (0,) * len(shape))
    return pl.pallas_call(
        _peer_router_kernel,
        grid=(n // tm,),
        in_specs=[
            pl.BlockSpec((tm, D_MODEL), lambda i: (i, 0)),
            full((1, D_MODEL)),
            full((D_MODEL, PEER_HEADS * PEER_KEY_DIM)),
            full((N_KEYS, PEER_HALF)),
            full((N_KEYS, PEER_HALF)),
        ],
        out_specs=[
            pl.BlockSpec((tm, D_MODEL), lambda i: (i, 0)),
            pl.BlockSpec((tm, PEER_PICKS), lambda i: (i, 0)),
            pl.BlockSpec((tm, PEER_PICKS), lambda i: (i, 0)),
        ],
        out_shape=[
            jax.ShapeDtypeStruct((n, D_MODEL), F32),
            jax.ShapeDtypeStruct((n, PEER_PICKS), I32),
            jax.ShapeDtypeStruct((n, PEER_PICKS), F32),
        ],
        compiler_params=pltpu.CompilerParams(
            dimension_semantics=("arbitrary",), vmem_limit_bytes=48 * 1024 * 1024),
        name="peer_router",
    )(x, g.reshape(1, D_MODEL), w_q.astype(BF16), sub_k1.astype(BF16), sub_k2.astype(BF16))


def _pack_table(w):
    bits = lax.bitcast_convert_type(w.astype(jnp.bfloat16), jnp.uint16).astype(jnp.uint32)
    bits = bits.reshape(N_EXPERTS, 2, ROW_SUB, LANES)
    return lax.bitcast_convert_type((bits[:, 0] << 16) | bits[:, 1], I32)


def _unpack_row(w):
    hi = lax.bitcast_convert_type(w & jnp.int32(-65536), F32)
    lo = lax.bitcast_convert_type(w << 16, F32)
    return hi, lo


def _peer_act_kernel(eidx_ref, tab_ref, h_ref, gate_ref, coef_ref):
    ones = jnp.ones((SUBLANES, LANES), F32)

    def token(t, carry):
        xa = h_ref[t, 0:ROW_SUB, :]
        xb = h_ref[t, ROW_SUB:2 * ROW_SUB, :]
        rows = []
        for p in range(PEER_PICKS):
            hi, lo = _unpack_row(tab_ref[eidx_ref[t, p]])
            rows.append(jnp.sum(hi * xa + lo * xb, axis=0, keepdims=True))
        part = jnp.concatenate(rows, axis=0)
        act = _nt_dot(ones, part, preferred_element_type=F32, precision=lax.Precision.HIGHEST)[0:1]
        coef_ref[pl.ds(t, 1), :] = gate_ref[pl.ds(t, 1), :] * jax.nn.gelu(act)
        return carry

    lax.fori_loop(0, h_ref.shape[0], token, 0)


def _peer_out_kernel(eidx_ref, coef_ref, tab_ref, x_ref, out_ref):
    n_acc = 4

    def token(t, carry):
        acc_hi = [jnp.zeros((ROW_SUB, LANES), F32) for _ in range(n_acc)]
        acc_lo = [jnp.zeros((ROW_SUB, LANES), F32) for _ in range(n_acc)]
        for p in range(PEER_PICKS):
            hi, lo = _unpack_row(tab_ref[eidx_ref[t, p]])
            c = coef_ref[t, p]
            acc_hi[p % n_acc] = acc_hi[p % n_acc] + c * hi
            acc_lo[p % n_acc] = acc_lo[p % n_acc] + c * lo
        y_hi = (acc_hi[0] + acc_hi[1]) + (acc_hi[2] + acc_hi[3])
        y_lo = (acc_lo[0] + acc_lo[1]) + (acc_lo[2] + acc_lo[3])
        out_ref[t, 0:ROW_SUB, :] = x_ref[t, 0:ROW_SUB, :] + y_hi
        out_ref[t, ROW_SUB:2 * ROW_SUB, :] = x_ref[t, ROW_SUB:2 * ROW_SUB, :] + y_lo
        return carry

    lax.fori_loop(0, x_ref.shape[0], token, 0)


def _table_spec():
    return pl.BlockSpec((N_EXPERTS, ROW_SUB, LANES), lambda i: (0, 0, 0), pipeline_mode=pl.Buffered(1))


_PEER_VMEM_LIMIT = 56 * 1024 * 1024


def _peer_act(eidx, tab_u, h, gate):
    n = h.shape[0]
    tb = PEER_TOKENS
    return pl.pallas_call(
        _peer_act_kernel,
        grid=(n // tb,),
        in_specs=[
            pl.BlockSpec((tb, PEER_PICKS), lambda i: (i, 0), memory_space=pltpu.SMEM),
            _table_spec(),
            pl.BlockSpec((tb, 2 * ROW_SUB, LANES), lambda i: (i, 0, 0)),
            pl.BlockSpec((tb, PEER_PICKS), lambda i: (i, 0)),
        ],
        out_specs=pl.BlockSpec((tb, PEER_PICKS), lambda i: (i, 0)),
        out_shape=jax.ShapeDtypeStruct((n, PEER_PICKS), F32),
        compiler_params=pltpu.CompilerParams(
            dimension_semantics=("arbitrary",), vmem_limit_bytes=_PEER_VMEM_LIMIT),
        name="peer_act",
    )(eidx, tab_u, h.reshape(n, 2 * ROW_SUB, LANES), gate)


def _peer_out(eidx, coef, tab_v, x):
    n = x.shape[0]
    tb = PEER_TOKENS
    y = pl.pallas_call(
        _peer_out_kernel,
        grid=(n // tb,),
        in_specs=[
            pl.BlockSpec((tb, PEER_PICKS), lambda i: (i, 0), memory_space=pltpu.SMEM),
            pl.BlockSpec((tb, PEER_PICKS), lambda i: (i, 0), memory_space=pltpu.SMEM),
            _table_spec(),
            pl.BlockSpec((tb, 2 * ROW_SUB, LANES), lambda i: (i, 0, 0)),
        ],
        out_specs=pl.BlockSpec((tb, 2 * ROW_SUB, LANES), lambda i: (i, 0, 0)),
        out_shape=jax.ShapeDtypeStruct((n, 2 * ROW_SUB, LANES), F32),
        compiler_params=pltpu.CompilerParams(
            dimension_semantics=("arbitrary",), vmem_limit_bytes=_PEER_VMEM_LIMIT),
        name="peer_out",
    )(eidx, coef, tab_v, x.reshape(n, 2 * ROW_SUB, LANES))
    return y.reshape(n, D_MODEL)


def _peer_layer(x, g, w_q, sub_k1, sub_k2, expert_u, expert_v):
    h, eidx, gate = _peer_router(x, g, w_q, sub_k1, sub_k2)
    coef = _peer_act(eidx, _pack_table(expert_u), h, gate)
    return _peer_out(eidx, coef, _pack_table(expert_v), x)


SEQ = 4096
PAST_LEN = 16384
PAGE_SIZE = 128
CHUNK = 128
GMLP_WIDTH = 2 * D_MODEL
GMLP_GROUPS = 8
GMLP_GROUP_DIM = GMLP_WIDTH // GMLP_GROUPS
N_HEADS = 16
HEAD_DIM = D_MODEL // N_HEADS
N_KV_HEADS = 4
GROUP = N_HEADS // N_KV_HEADS
IDX_HEADS = 8
IDX_DIM = 64
TOPK_MAX = 256
Q_BLOCK = 128
ROPE_THETA = 10000.0
IDX_SCALE = (IDX_DIM ** -0.5) * (IDX_HEADS ** -0.5)
ATT_SIZES = (N_HEADS * HEAD_DIM, N_KV_HEADS * HEAD_DIM, N_KV_HEADS * HEAD_DIM,
             IDX_HEADS * IDX_DIM, IDX_DIM, IDX_HEADS)


def _rms(x, g):
    return x * lax.rsqrt(jnp.mean(x * x, axis=-1, keepdims=True) + EPS) * g


def _rope(x, pos):
    half = x.shape[-1] // 2
    inv = jnp.exp(-math.log(ROPE_THETA) * jnp.arange(half, dtype=F32) / half)
    ang = pos.astype(F32)[:, None] * inv[None, :]
    cos = jnp.cos(ang)[:, None, :]
    sin = jnp.sin(ang)[:, None, :]
    x1, x2 = x[..., :half], x[..., half:]
    return jnp.concatenate([x1 * cos - x2 * sin, x2 * cos + x1 * sin], axis=-1)


def _gmlp_mix(h, w_in, v_g, w_s, b_s, w_out):
    B, T, _ = h.shape
    uv = h @ w_in
    u, v = uv[..., :GMLP_WIDTH], uv[..., GMLP_WIDTH:]
    v = _rms(v, v_g)
    n_chunks = -(-T // CHUNK)
    pad = n_chunks * CHUNK - T
    vp = jnp.pad(v, ((0, 0), (0, pad), (0, 0))).reshape(B, n_chunks, CHUNK, GMLP_GROUPS, GMLP_GROUP_DIM)
    causal = jnp.tril(jnp.ones((CHUNK, CHUNK), dtype=bool))
    ws = jnp.where(causal, w_s, 0)
    mixed = jnp.einsum('gts,bcsgd->bctgd', ws, vp) + b_s.T[None, None, :, :, None]
    mixed = mixed.reshape(B, n_chunks * CHUNK, GMLP_WIDTH)[:, :T]
    return (u * mixed) @ w_out, v


def _att_project(h, w_in, qn_g, kn_g, ikn_g, pos):
    B, T, _ = h.shape
    p = h @ w_in
    splits = [int(s) for s in np.cumsum(ATT_SIZES)[:-1]]
    q, k, v, qi, ki, wi = jnp.split(p, splits, axis=-1)
    q = _rope(_rms(q.reshape(B, T, N_HEADS, HEAD_DIM), qn_g), pos)
    k = _rope(_rms(k.reshape(B, T, N_KV_HEADS, HEAD_DIM), kn_g), pos)
    v = v.reshape(B, T, N_KV_HEADS, HEAD_DIM)
    qi = _rope(qi.reshape(B, T, IDX_HEADS, IDX_DIM), pos)
    ki = _rope(_rms(ki, ikn_g)[:, :, None, :], pos)[:, :, 0]
    return q, k, v, qi, ki, wi


def _sparse_attend(q, qi, wi, ki_all, q_pos, n_keep, gather_kv):
    B, Tq = q.shape[0], q.shape[1]
    s = jax.nn.relu(jnp.einsum('bthd,bsd->bths', qi, ki_all))
    score = jnp.einsum('bths,bth->bts', s, wi).astype(F32) * IDX_SCALE
    key_pos = jnp.arange(ki_all.shape[1])
    admissible = key_pos[None, :] <= q_pos[:, None]
    score = jnp.where(admissible[None], score, -jnp.inf)
    _, idx = lax.top_k(score, n_keep)
    sel_ok = idx <= q_pos[None, :, None]
    k_sel, v_sel = gather_kv(idx)
    qg = q.reshape(B, Tq, N_KV_HEADS, GROUP, HEAD_DIM)
    logits = jnp.einsum('bthgd,btnhd->bthgn', qg, k_sel).astype(F32) * (HEAD_DIM ** -0.5)
    logits = jnp.where(sel_ok[:, :, None, None, :], logits, -jnp.inf)
    prob = jax.nn.softmax(logits, axis=-1)
    o = jnp.einsum('bthgn,btnhd->bthgd', prob, v_sel)
    return o.reshape(B, Tq, N_HEADS * HEAD_DIM)


def _take_rows(a, idx):
    return jax.vmap(lambda ab, ib: ab[ib])(a, idx)


def _att_prompt(h, w_in, qn_g, kn_g, ikn_g, w_out):
    B, S, _ = h.shape
    pos = jnp.arange(S)
    q, k, v, qi, ki, wi = _att_project(h, w_in, qn_g, kn_g, ikn_g, pos)
    n_keep = min(TOPK_MAX, S // 4)
    nb = S // Q_BLOCK

    def gather(idx):
        return _take_rows(k, idx), _take_rows(v, idx)

    def blocks(a):
        return a.reshape((B, nb, Q_BLOCK) + a.shape[2:]).swapaxes(0, 1)

    def one_block(args):
        qb, qib, wib, posb = args
        return _sparse_attend(qb, qib, wib, ki, posb, n_keep, gather)

    o = lax.map(one_block, (blocks(q), blocks(qi), blocks(wi), pos.reshape(nb, Q_BLOCK)))
    o = o.swapaxes(0, 1).reshape(B, S, N_HEADS * HEAD_DIM)
    return o @ w_out, k, v, ki


def _att_sample(h, ck, cv, cik, page_table, w_in, qn_g, kn_g, ikn_g, w_out):
    B, T, _ = h.shape
    pos = PAST_LEN + jnp.arange(T)
    q, k, v, qi, ki, wi = _att_project(h, w_in, qn_g, kn_g, ikn_g, pos)
    n_pages = PAST_LEN // PAGE_SIZE
    ki_past = cik[page_table].reshape(B, n_pages * PAGE_SIZE, IDX_DIM)
    ki_all = jnp.concatenate([ki_past, ki], axis=1)
    n_keep = min(TOPK_MAX, (PAST_LEN + T) // 4)

    def gather(idx):
        is_new = (idx >= PAST_LEN)[..., None, None]
        past = jnp.minimum(idx, PAST_LEN - 1)
        phys = jnp.take_along_axis(page_table, (past // PAGE_SIZE).reshape(B, -1), axis=1).reshape(idx.shape)
        off = past % PAGE_SIZE
        new = jnp.clip(idx - PAST_LEN, 0, T - 1)
        k_sel = jnp.where(is_new, _take_rows(k, new), ck[phys, off])
        v_sel = jnp.where(is_new, _take_rows(v, new), cv[phys, off])
        return k_sel, v_sel

    o = _sparse_attend(q, qi, wi, ki_all, pos, n_keep, gather)
    return o @ w_out, k, v, ki


def kernel(x_prompt, x_sample, cache_k, cache_v, cache_idx_k, page_table, norm_mix_g, norm_ffn_g,
           gmlp_w_in, gmlp_v_g, gmlp_w_s, gmlp_b_s, gmlp_w_out,
           att_w_in, att_q_norm_g, att_k_norm_g, att_idx_k_norm_g, att_w_out,
           peer_w_q, peer_sub_k1, peer_sub_k2, peer_u, peer_v):
    n_batch, seq, _ = x_prompt.shape
    n_dec, dec_seq, _ = x_sample.shape
    n_prompt = n_batch * seq
    n_tok = n_prompt + n_dec * dec_seq
    n_pad = -(-n_tok // PEER_TOKENS) * PEER_TOKENS

    def peer(xp, xs, layer):
        x = jnp.concatenate([xp.reshape(n_prompt, D_MODEL), xs.reshape(n_dec * dec_seq, D_MODEL),
                             jnp.zeros((n_pad - n_tok, D_MODEL), F32)], axis=0)
        y = _peer_layer(x, norm_ffn_g[layer], peer_w_q[layer], peer_sub_k1[layer], peer_sub_k2[layer],
                        peer_u[layer], peer_v[layer])
        return y[:n_prompt].reshape(xp.shape), y[n_prompt:n_tok].reshape(xs.shape)

    xp, xs = x_prompt, x_sample
    hp = _rms(xp, norm_mix_g[0])
    hs = _rms(xs, norm_mix_g[0])
    op, _ = _gmlp_mix(hp, gmlp_w_in[0], gmlp_v_g[0], gmlp_w_s[0], gmlp_b_s[0], gmlp_w_out[0])
    os_, v_rows = _gmlp_mix(hs, gmlp_w_in[0], gmlp_v_g[0], gmlp_w_s[0], gmlp_b_s[0], gmlp_w_out[0])
    xp, xs = peer(xp + op, xs + os_, 0)
    hp = _rms(xp, norm_mix_g[1])
    hs = _rms(xs, norm_mix_g[1])
    op, kp, vp, ip = _att_prompt(hp, att_w_in[0], att_q_norm_g[0], att_k_norm_g[0],
                                 att_idx_k_norm_g[0], att_w_out[0])
    os_, ksm, vsm, ism = _att_sample(hs, cache_k[0], cache_v[0], cache_idx_k[0], page_table,
                                     att_w_in[0], att_q_norm_g[0], att_k_norm_g[0],
                                     att_idx_k_norm_g[0], att_w_out[0])
    xp, xs = peer(xp + op, xs + os_, 1)
    return (xp, xs, kp[None], vp[None], ip[None], ksm[None], vsm[None], ism[None], v_rows[None])
```

```python
import functools
import math

import jax
import jax.numpy as jnp
import numpy as np
from jax import lax
from jax.experimental import pallas as pl
from jax.experimental.pallas import tpu as pltpu

F32 = jnp.float32
BF16 = jnp.bfloat16
I32 = jnp.int32

D_MODEL = 1024
EPS = 1e-6

LANES = 128
SUBLANES = 8
VMEM_BYTES_V7X = 64 * 1024 * 1024

PEER_HEADS = 8
N_KEYS = 128
N_EXPERTS = N_KEYS * N_KEYS
PEER_TOPK = 16
PEER_KEY_DIM = 256
PEER_HALF = PEER_KEY_DIM // 2
PEER_PICKS = PEER_HEADS * PEER_TOPK
ROW_WORDS = D_MODEL // 2
ROW_SUB = ROW_WORDS // LANES
PEER_TOKENS = 128


def _nt_dot(a, b, **kw):
    return lax.dot_general(a, b, (((1,), (1,)), ((), ())), **kw)


def _topk_rows(s, k, iota):
    n_rows = s.shape[0]
    vals, idxs = [], []
    for _ in range(k):
        m = jnp.max(s, axis=0, keepdims=True)
        first = jnp.min(jnp.where(s == m, iota, n_rows), axis=0, keepdims=True)
        vals.append(m)
        idxs.append(first)
        s = jnp.where(iota == first, -jnp.inf, s)
    return jnp.concatenate(vals, axis=0), jnp.concatenate(idxs, axis=0)


def _peer_router_kernel(x_ref, g_ref, wq_ref, k1_ref, k2_ref, h_ref, eidx_ref, gate_ref):
    x = x_ref[...]
    h = x * lax.rsqrt(jnp.mean(x * x, axis=-1, keepdims=True) + EPS) * g_ref[...]
    h_ref[...] = h
    q = jnp.dot(h.astype(BF16), wq_ref[...], preferred_element_type=F32)
    n_tok = x.shape[0]
    iota_k = lax.broadcasted_iota(I32, (N_KEYS, n_tok), 0)
    iota_c = lax.broadcasted_iota(I32, (PEER_TOPK * PEER_TOPK, n_tok), 0)
    k1 = k1_ref[...]
    k2 = k2_ref[...]
    eidx_rows, gate_rows = [], []
    for head in range(PEER_HEADS):
        base = head * PEER_KEY_DIM
        q1 = q[:, base:base + PEER_HALF].astype(BF16)
        q2 = q[:, base + PEER_HALF:base + PEER_KEY_DIM].astype(BF16)
        s1 = _nt_dot(k1, q1, preferred_element_type=F32)
        s2 = _nt_dot(k2, q2, preferred_element_type=F32)
        v1, i1 = _topk_rows(s1, PEER_TOPK, iota_k)
        v2, i2 = _topk_rows(s2, PEER_TOPK, iota_k)
        cand = jnp.concatenate([v1[a:a + 1] + v2 for a in range(PEER_TOPK)], axis=0)
        cidx = jnp.concatenate([i1[a:a + 1] * N_KEYS + i2 for a in range(PEER_TOPK)], axis=0)
        top_s, top_e = [], []
        for _ in range(PEER_TOPK):
            m = jnp.max(cand, axis=0, keepdims=True)
            first = jnp.min(jnp.where(cand == m, iota_c, PEER_TOPK * PEER_TOPK), axis=0, keepdims=True)
            hit = iota_c == first
            top_s.append(m)
            top_e.append(jnp.sum(jnp.where(hit, cidx, 0), axis=0, keepdims=True))
            cand = jnp.where(hit, -jnp.inf, cand)
        top_s = jnp.concatenate(top_s, axis=0)
        e = jnp.exp(top_s - top_s[0:1])
        gate_rows.append(e / jnp.sum(e, axis=0, keepdims=True))
        eidx_rows.append(jnp.concatenate(top_e, axis=0))
    gate_t = jnp.concatenate(gate_rows, axis=0)
    eidx_t = jnp.concatenate(eidx_rows, axis=0)
    gate_ref[...] = gate_t.T
    eidx_ref[...] = lax.bitcast_convert_type(lax.bitcast_convert_type(eidx_t, F32).T, I32)


def _peer_router(x, g, w_q, sub_k1, sub_k2):
    n = x.shape[0]
    tm = PEER_TOKENS
    full = lambda shape: pl.BlockSpec(shape, lambda i: (0,) * len(shape))
    return pl.pallas_call(
        _peer_router_kernel,
        grid=(n // tm,),
        in_specs=[
            pl.BlockSpec((tm, D_MODEL), lambda i: (i, 0)),
            full((1, D_MODEL)),
            full((D_MODEL, PEER_HEADS * PEER_KEY_DIM)),
            full((N_KEYS, PEER_HALF)),
            full((N_KEYS, PEER_HALF)),
        ],
        out_specs=[
            pl.BlockSpec((tm, D_MODEL), lambda i: (i, 0)),
            pl.BlockSpec((tm, PEER_PICKS), lambda i: (i, 0)),
            pl.BlockSpec((tm, PEER_PICKS), lambda i: (i, 0)),
        ],
        out_shape=[
            jax.ShapeDtypeStruct((n, D_MODEL), F32),
            jax.ShapeDtypeStruct((n, PEER_PICKS), I32),
            jax.ShapeDtypeStruct((n, PEER_PICKS), F32),
        ],
        compiler_params=pltpu.CompilerParams(
            dimension_semantics=("arbitrary",), vmem_limit_bytes=48 * 1024 * 1024),
        name="peer_router",
    )(x, g.reshape(1, D_MODEL), w_q.astype(BF16), sub_k1.astype(BF16), sub_k2.astype(BF16))


def _pack_table(w):
    bits = lax.bitcast_convert_type(w.astype(jnp.bfloat16), jnp.uint16).astype(jnp.uint32)
    bits = bits.reshape(N_EXPERTS, 2, ROW_SUB, LANES)
    return lax.bitcast_convert_type((bits[:, 0] << 16) | bits[:, 1], I32)


def _unpack_row(w):
    hi = lax.bitcast_convert_type(w & jnp.int32(-65536), F32)
    lo = lax.bitcast_convert_type(w << 16, F32)
    return hi, lo


def _peer_act_kernel(eidx_ref, tab_ref, h_ref, gate_ref, coef_ref):
    ones = jnp.ones((SUBLANES, LANES), F32)

    def token(t, carry):
        xa = h_ref[t, 0:ROW_SUB, :]
        xb = h_ref[t, ROW_SUB:2 * ROW_SUB, :]
        rows = []
        for p in range(PEER_PICKS):
            hi, lo = _unpack_row(tab_ref[eidx_ref[t, p]])
            rows.append(jnp.sum(hi * xa + lo * xb, axis=0, keepdims=True))
        part = jnp.concatenate(rows, axis=0)
        act = _nt_dot(ones, part, preferred_element_type=F32, precision=lax.Precision.HIGHEST)[0:1]
        coef_ref[pl.ds(t, 1), :] = gate_ref[pl.ds(t, 1), :] * jax.nn.gelu(act)
        return carry

    lax.fori_loop(0, h_ref.shape[0], token, 0)


def _peer_out_kernel(eidx_ref, coef_ref, tab_ref, x_ref, out_ref):
    n_acc = 4

    def token(t, carry):
        acc_hi = [jnp.zeros((ROW_SUB, LANES), F32) for _ in range(n_acc)]
        acc_lo = [jnp.zeros((ROW_SUB, LANES), F32) for _ in range(n_acc)]
        for p in range(PEER_PICKS):
            hi, lo = _unpack_row(tab_ref[eidx_ref[t, p]])
            c = coef_ref[t, p]
            acc_hi[p % n_acc] = acc_hi[p % n_acc] + c * hi
            acc_lo[p % n_acc] = acc_lo[p % n_acc] + c * lo
        y_hi = (acc_hi[0] + acc_hi[1]) + (acc_hi[2] + acc_hi[3])
        y_lo = (acc_lo[0] + acc_lo[1]) + (acc_lo[2] + acc_lo[3])
        out_ref[t, 0:ROW_SUB, :] = x_ref[t, 0:ROW_SUB, :] + y_hi
        out_ref[t, ROW_SUB:2 * ROW_SUB, :] = x_ref[t, ROW_SUB:2 * ROW_SUB, :] + y_lo
        return carry

    lax.fori_loop(0, x_ref.shape[0], token, 0)


def _table_spec():
    return pl.BlockSpec((N_EXPERTS, ROW_SUB, LANES), lambda i: (0, 0, 0), pipeline_mode=pl.Buffered(1))


_PEER_VMEM_LIMIT = 56 * 1024 * 1024


def _peer_act(eidx, tab_u, h, gate):
    n = h.shape[0]
    tb = PEER_TOKENS
    return pl.pallas_call(
        _peer_act_kernel,
        grid=(n // tb,),
        in_specs=[
            pl.BlockSpec((tb, PEER_PICKS), lambda i: (i, 0), memory_space=pltpu.SMEM),
            _table_spec(),
            pl.BlockSpec((tb, 2 * ROW_SUB, LANES), lambda i: (i, 0, 0)),
            pl.BlockSpec((tb, PEER_PICKS), lambda i: (i, 0)),
        ],
        out_specs=pl.BlockSpec((tb, PEER_PICKS), lambda i: (i, 0)),
        out_shape=jax.ShapeDtypeStruct((n, PEER_PICKS), F32),
        compiler_params=pltpu.CompilerParams(
            dimension_semantics=("arbitrary",), vmem_limit_bytes=_PEER_VMEM_LIMIT),
        name="peer_act",
    )(eidx, tab_u, h.reshape(n, 2 * ROW_SUB, LANES), gate)


def _peer_out(eidx, coef, tab_v, x):
    n = x.shape[0]
    tb = PEER_TOKENS
    y = pl.pallas_call(
        _peer_out_kernel,
        grid=(n // tb,),
        in_specs=[
            pl.BlockSpec((tb, PEER_PICKS), lambda i: (i, 0), memory_space=pltpu.SMEM),
            pl.BlockSpec((tb, PEER_PICKS), lambda i: (i, 0), memory_space=pltpu.SMEM),
            _table_spec(),
            pl.BlockSpec((tb, 2 * ROW_SUB, LANES), lambda i: (i, 0, 0)),
        ],
        out_specs=pl.BlockSpec((tb, 2 * ROW_SUB, LANES), lambda i: (i, 0, 0)),
        out_shape=jax.ShapeDtypeStruct((n, 2 * ROW_SUB, LANES), F32),
        compiler_params=pltpu.CompilerParams(
            dimension_semantics=("arbitrary",), vmem_limit_bytes=_PEER_VMEM_LIMIT),
        name="peer_out",
    )(eidx, coef, tab_v, x.reshape(n, 2 * ROW_SUB, LANES))
    return y.reshape(n, D_MODEL)


def _peer_layer(x, g, w_q, sub_k1, sub_k2, expert_u, expert_v):
    h, eidx, gate = _peer_router(x, g, w_q, sub_k1, sub_k2)
    coef = _peer_act(eidx, _pack_table(expert_u), h, gate)
    return _peer_out(eidx, coef, _pack_table(expert_v), x)


SEQ = 4096
PAST_LEN = 16384
PAGE_SIZE = 128
CHUNK = 128
GMLP_WIDTH = 2 * D_MODEL
GMLP_GROUPS = 8
GMLP_GROUP_DIM = GMLP_WIDTH // GMLP_GROUPS
N_HEADS = 16
HEAD_DIM = D_MODEL // N_HEADS
N_KV_HEADS = 4
GROUP = N_HEADS // N_KV_HEADS
IDX_HEADS = 8
IDX_DIM = 64
TOPK_MAX = 256
Q_BLOCK = 128
ROPE_THETA = 10000.0
IDX_SCALE = (IDX_DIM ** -0.5) * (IDX_HEADS ** -0.5)
ATT_SIZES = (N_HEADS * HEAD_DIM, N_KV_HEADS * HEAD_DIM, N_KV_HEADS * HEAD_DIM,
             IDX_HEADS * IDX_DIM, IDX_DIM, IDX_HEADS)


def _rms(x, g):
    return x * lax.rsqrt(jnp.mean(x * x, axis=-1, keepdims=True) + EPS) * g


def _rope(x, pos):
    half = x.shape[-1] // 2
    inv = jnp.exp(-math.log(ROPE_THETA) * jnp.arange(half, dtype=F32) / half)
    ang = pos.astype(F32)[:, None] * inv[None, :]
    cos = jnp.cos(ang)[:, None, :]
    sin = jnp.sin(ang)[:, None, :]
    x1, x2 = x[..., :half], x[..., half:]
    return jnp.concatenate([x1 * cos - x2 * sin, x2 * cos + x1 * sin], axis=-1)


def _gmlp_mix(h, w_in, v_g, w_s, b_s, w_out):
    B, T, _ = h.shape
    uv = h @ w_in
    u, v = uv[..., :GMLP_WIDTH], uv[..., GMLP_WIDTH:]
    v = _rms(v, v_g)
    n_chunks = -(-T // CHUNK)
    pad = n_chunks * CHUNK - T
    vp = jnp.pad(v, ((0, 0), (0, pad), (0, 0))).reshape(B, n_chunks, CHUNK, GMLP_GROUPS, GMLP_GROUP_DIM)
    causal = jnp.tril(jnp.ones((CHUNK, CHUNK), dtype=bool))
    ws = jnp.where(causal, w_s, 0)
    mixed = jnp.einsum('gts,bcsgd->bctgd', ws, vp) + b_s.T[None, None, :, :, None]
    mixed = mixed.reshape(B, n_chunks * CHUNK, GMLP_WIDTH)[:, :T]
    return (u * mixed) @ w_out, v


def _att_project(h, w_in, qn_g, kn_g, ikn_g, pos):
    B, T, _ = h.shape
    p = h @ w_in
    splits = [int(s) for s in np.cumsum(ATT_SIZES)[:-1]]
    q, k, v, qi, ki, wi = jnp.split(p, splits, axis=-1)
    q = _rope(_rms(q.reshape(B, T, N_HEADS, HEAD_DIM), qn_g), pos)
    k = _rope(_rms(k.reshape(B, T, N_KV_HEADS, HEAD_DIM), kn_g), pos)
    v = v.reshape(B, T, N_KV_HEADS, HEAD_DIM)
    qi = _rope(qi.reshape(B, T, IDX_HEADS, IDX_DIM), pos)
    ki = _rope(_rms(ki, ikn_g)[:, :, None, :], pos)[:, :, 0]
    return q, k, v, qi, ki, wi


def _sparse_attend(q, qi, wi, ki_all, q_pos, n_keep, gather_kv):
    B, Tq = q.shape[0], q.shape[1]
    s = jax.nn.relu(jnp.einsum('bthd,bsd->bths', qi, ki_all))
    score = jnp.einsum('bths,bth->bts', s, wi).astype(F32) * IDX_SCALE
    key_pos = jnp.arange(ki_all.shape[1])
    admissible = key_pos[None, :] <= q_pos[:, None]
    score = jnp.where(admissible[None], score, -jnp.inf)
    _, idx = lax.top_k(score, n_keep)
    sel_ok = idx <= q_pos[None, :, None]
    k_sel, v_sel = gather_kv(idx)
    qg = q.reshape(B, Tq, N_KV_HEADS, GROUP, HEAD_DIM)
    logits = jnp.einsum('bthgd,btnhd->bthgn', qg, k_sel).astype(F32) * (HEAD_DIM ** -0.5)
    logits = jnp.where(sel_ok[:, :, None, None, :], logits, -jnp.inf)
    prob = jax.nn.softmax(logits, axis=-1)
    o = jnp.einsum('bthgn,btnhd->bthgd', prob, v_sel)
    return o.reshape(B, Tq, N_HEADS * HEAD_DIM)


def _take_rows(a, idx):
    return jax.vmap(lambda ab, ib: ab[ib])(a, idx)


DSA_TQ = 128
INT32_MIN = -(2 ** 31)


def _sortable_key(score):
    bits = lax.bitcast_convert_type(score, I32)
    return bits ^ ((bits >> 31) & jnp.int32(0x7FFFFFFF))


def _dsa_prompt_kernel(qi_ref, wi_ref, ki_ref, q_ref, k_ref, v_ref, x_ref, wo_ref, out_ref,
                       key_ref, bias_ref, *, s_eff, q_block0, n_keep):
    tq = DSA_TQ
    q0 = (q_block0 + pl.program_id(1)) * tq
    ki = ki_ref[...]
    wi = wi_ref[...]
    score = jnp.zeros((tq, s_eff), F32)
    for h in range(IDX_HEADS):
        s = _nt_dot(qi_ref[h], ki, preferred_element_type=F32)
        score = score + wi[:, h:h + 1] * jnp.maximum(s, 0.0)
    score = score * IDX_SCALE
    q_pos = q0 + lax.broadcasted_iota(I32, (tq, 1), 0)
    k_pos = lax.broadcasted_iota(I32, (tq, s_eff), 1)
    score = jnp.where(k_pos <= q_pos, score, -jnp.inf)
    key_ref[...] = _sortable_key(score)

    def search(it, thr):
        cand = thr + jnp.left_shift(jnp.int32(1), 31 - it)
        cnt = jnp.sum(jnp.where(key_ref[...] >= cand, 1, 0), axis=1, keepdims=True)
        return jnp.where(cnt >= n_keep, cand, thr)

    thr = lax.fori_loop(0, 32, search, jnp.full((tq, 1), INT32_MIN, I32))
    n_gt = jnp.sum(jnp.where(key_ref[...] > thr, 1, 0), axis=1, keepdims=True)
    need = (n_keep - n_gt).astype(F32)
    r_i = lax.broadcasted_iota(I32, (LANES, LANES), 0)
    c_i = lax.broadcasted_iota(I32, (LANES, LANES), 1)
    before = jnp.where(r_i < c_i, 1.0, 0.0).astype(BF16)
    run = jnp.zeros((tq, 1), F32)
    for c in range(s_eff // LANES):
        sl = slice(c * LANES, (c + 1) * LANES)
        key_c = key_ref[:, sl]
        tie = jnp.where(key_c == thr, 1.0, 0.0)
        rank = run + jnp.dot(tie.astype(BF16), before, preferred_element_type=F32)
        keep_tie = jnp.where(rank < need, tie, 0.0)
        keep = jnp.where(key_c > thr, 1.0, keep_tie)
        adm = (c * LANES + lax.broadcasted_iota(I32, (tq, LANES), 1)) <= q_pos
        bias_ref[:, sl] = jnp.where(adm, jnp.where(keep > 0.0, 0.0, -jnp.inf), -jnp.inf)
        run = run + jnp.sum(tie, axis=1, keepdims=True)

    outs = []
    for hk in range(N_KV_HEADS):
        k_h = k_ref[hk]
        v_h = v_ref[hk]
        for g in range(GROUP):
            logits = _nt_dot(q_ref[hk * GROUP + g], k_h, preferred_element_type=F32) + bias_ref[...]
            m = jnp.max(logits, axis=1, keepdims=True)
            p = jnp.exp(logits - m)
            den = jnp.sum(p, axis=1, keepdims=True)
            o = jnp.dot(p.astype(BF16), v_h, preferred_element_type=F32)
            outs.append(o / den)
    o_all = jnp.concatenate(outs, axis=1)
    out_ref[...] = x_ref[...] + jnp.dot(o_all.astype(BF16), wo_ref[...], preferred_element_type=F32)


def _dsa_prompt_call(qi, wi, ki, q, k, v, x, w_out, *, s_eff, q_block0, n_qblocks, n_keep):
    n_batch = x.shape[0]
    tq = DSA_TQ
    body = functools.partial(_dsa_prompt_kernel, s_eff=s_eff, q_block0=q_block0, n_keep=n_keep)
    return pl.pallas_call(
        body,
        grid=(n_batch, n_qblocks),
        in_specs=[
            pl.BlockSpec((None, IDX_HEADS, tq, IDX_DIM), lambda b, j: (b, 0, q_block0 + j, 0)),
            pl.BlockSpec((None, tq, IDX_HEADS), lambda b, j: (b, q_block0 + j, 0)),
            pl.BlockSpec((None, s_eff, IDX_DIM), lambda b, j: (b, 0, 0)),
            pl.BlockSpec((None, N_HEADS, tq, HEAD_DIM), lambda b, j: (b, 0, q_block0 + j, 0)),
            pl.BlockSpec((None, N_KV_HEADS, s_eff, HEAD_DIM), lambda b, j: (b, 0, 0, 0)),
            pl.BlockSpec((None, N_KV_HEADS, s_eff, HEAD_DIM), lambda b, j: (b, 0, 0, 0)),
            pl.BlockSpec((None, tq, D_MODEL), lambda b, j: (b, q_block0 + j, 0)),
            pl.BlockSpec((D_MODEL, D_MODEL), lambda b, j: (0, 0)),
        ],
        out_specs=pl.BlockSpec((None, tq, D_MODEL), lambda b, j: (b, j, 0)),
        out_shape=jax.ShapeDtypeStruct((n_batch, n_qblocks * tq, D_MODEL), F32),
        scratch_shapes=[pltpu.VMEM((tq, s_eff), I32), pltpu.VMEM((tq, s_eff), F32)],
        compiler_params=pltpu.CompilerParams(
            dimension_semantics=("arbitrary", "arbitrary"), vmem_limit_bytes=56 * 1024 * 1024),
        name=f"dsa_prompt_{s_eff}",
    )(qi, wi, ki, q, k, v, x, w_out)


DSA_CLASS = 512


def _att_prompt(x, h, w_in, qn_g, kn_g, ikn_g, w_out, class_rows=DSA_CLASS):
    B, S, _ = h.shape
    pos = jnp.arange(S)
    q, k, v, qi, ki, wi = _att_project(h, w_in, qn_g, kn_g, ikn_g, pos)
    n_keep = min(TOPK_MAX, S // 4)
    head_major = lambda a: a.astype(BF16).transpose(0, 2, 1, 3)
    qh = head_major(q * (HEAD_DIM ** -0.5))
    kh, vh, qih = head_major(k), head_major(v), head_major(qi)
    ki_b = ki.astype(BF16)
    wo_b = w_out.astype(BF16)
    per_call = class_rows // DSA_TQ
    pieces = [
        _dsa_prompt_call(qih, wi, ki_b, qh, kh, vh, x, wo_b, s_eff=(c + 1) * class_rows,
                         q_block0=c * per_call, n_qblocks=per_call, n_keep=n_keep)
        for c in range(S // class_rows)
    ]
    return jnp.concatenate(pieces, axis=1), k, v, ki


def _att_sample(h, ck, cv, cik, page_table, w_in, qn_g, kn_g, ikn_g, w_out):
    B, T, _ = h.shape
    pos = PAST_LEN + jnp.arange(T)
    q, k, v, qi, ki, wi = _att_project(h, w_in, qn_g, kn_g, ikn_g, pos)
    n_pages = PAST_LEN // PAGE_SIZE
    ki_past = cik[page_table].reshape(B, n_pages * PAGE_SIZE, IDX_DIM)
    ki_all = jnp.concatenate([ki_past, ki], axis=1)
    n_keep = min(TOPK_MAX, (PAST_LEN + T) // 4)

    def gather(idx):
        is_new = (idx >= PAST_LEN)[..., None, None]
        past = jnp.minimum(idx, PAST_LEN - 1)
        phys = jnp.take_along_axis(page_table, (past // PAGE_SIZE).reshape(B, -1), axis=1).reshape(idx.shape)
        off = past % PAGE_SIZE
        new = jnp.clip(idx - PAST_LEN, 0, T - 1)
        k_sel = jnp.where(is_new, _take_rows(k, new), ck[phys, off])
        v_sel = jnp.where(is_new, _take_rows(v, new), cv[phys, off])
        return k_sel, v_sel

    o = _sparse_attend(q, qi, wi, ki_all, pos, n_keep, gather)
    return o @ w_out, k, v, ki


def kernel(x_prompt, x_sample, cache_k, cache_v, cache_idx_k, page_table, norm_mix_g, norm_ffn_g,
           gmlp_w_in, gmlp_v_g, gmlp_w_s, gmlp_b_s, gmlp_w_out,
           att_w_in, att_q_norm_g, att_k_norm_g, att_idx_k_norm_g, att_w_out,
           peer_w_q, peer_sub_k1, peer_sub_k2, peer_u, peer_v):
    n_batch, seq, _ = x_prompt.shape
    n_dec, dec_seq, _ = x_sample.shape
    n_prompt = n_batch * seq
    n_tok = n_prompt + n_dec * dec_seq
    n_pad = -(-n_tok // PEER_TOKENS) * PEER_TOKENS

    def peer(xp, xs, layer):
        x = jnp.concatenate([xp.reshape(n_prompt, D_MODEL), xs.reshape(n_dec * dec_seq, D_MODEL),
                             jnp.zeros((n_pad - n_tok, D_MODEL), F32)], axis=0)
        y = _peer_layer(x, norm_ffn_g[layer], peer_w_q[layer], peer_sub_k1[layer], peer_sub_k2[layer],
                        peer_u[layer], peer_v[layer])
        return y[:n_prompt].reshape(xp.shape), y[n_prompt:n_tok].reshape(xs.shape)

    xp, xs = x_prompt, x_sample
    hp = _rms(xp, norm_mix_g[0])
    hs = _rms(xs, norm_mix_g[0])
    op, _ = _gmlp_mix(hp, gmlp_w_in[0], gmlp_v_g[0], gmlp_w_s[0], gmlp_b_s[0], gmlp_w_out[0])
    os_, v_rows = _gmlp_mix(hs, gmlp_w_in[0], gmlp_v_g[0], gmlp_w_s[0], gmlp_b_s[0], gmlp_w_out[0])
    xp, xs = peer(xp + op, xs + os_, 0)
    hp = _rms(xp, norm_mix_g[1])
    hs = _rms(xs, norm_mix_g[1])
    xp, kp, vp, ip = _att_prompt(xp, hp, att_w_in[0], att_q_norm_g[0], att_k_norm_g[0],
                                 att_idx_k_norm_g[0], att_w_out[0])
    os_, ksm, vsm, ism = _att_sample(hs, cache_k[0], cache_v[0], cache_idx_k[0], page_table,
                                     att_w_in[0], att_q_norm_g[0], att_k_norm_g[0],
                                     att_idx_k_norm_g[0], att_w_out[0])
    xp, xs = peer(xp, xs + os_, 1)
    return (xp, xs, kp[None], vp[None], ip[None], ksm[None], vsm[None], ism[None], v_rows[None])
```

```python
import functools
import math

import jax
import jax.numpy as jnp
import numpy as np
from jax import lax
from jax.experimental import pallas as pl
from jax.experimental.pallas import tpu as pltpu

F32 = jnp.float32
BF16 = jnp.bfloat16
I32 = jnp.int32

D_MODEL = 1024
EPS = 1e-6

LANES = 128
SUBLANES = 8
VMEM_BYTES_V7X = 64 * 1024 * 1024

PEER_HEADS = 8
N_KEYS = 128
N_EXPERTS = N_KEYS * N_KEYS
PEER_TOPK = 16
PEER_KEY_DIM = 256
PEER_HALF = PEER_KEY_DIM // 2
PEER_PICKS = PEER_HEADS * PEER_TOPK
ROW_WORDS = D_MODEL // 2
ROW_SUB = ROW_WORDS // LANES
PEER_TOKENS = 128
N_CAND = sum(PEER_TOPK // (a + 1) for a in range(PEER_TOPK))


def _nt_dot(a, b, **kw):
    return lax.dot_general(a, b, (((1,), (1,)), ((), ())), **kw)


def _topk_rows(s, k, iota):
    n_rows = s.shape[0]
    vals, idxs = [], []
    for _ in range(k):
        m = jnp.max(s, axis=0, keepdims=True)
        first = jnp.min(jnp.where(s == m, iota, n_rows), axis=0, keepdims=True)
        vals.append(m)
        idxs.append(first)
        s = jnp.where(iota == first, -jnp.inf, s)
    return jnp.concatenate(vals, axis=0), jnp.concatenate(idxs, axis=0)


def _peer_router_kernel(x_ref, g_ref, wq_ref, k1_ref, k2_ref, h_ref, eidx_ref, gate_ref):
    x = x_ref[...]
    h = x * lax.rsqrt(jnp.mean(x * x, axis=-1, keepdims=True) + EPS) * g_ref[...]
    h_ref[...] = h
    q = jnp.dot(h.astype(BF16), wq_ref[...], preferred_element_type=F32)
    n_tok = x.shape[0]
    iota_k = lax.broadcasted_iota(I32, (N_KEYS, n_tok), 0)
    iota_c = lax.broadcasted_iota(I32, (N_CAND, n_tok), 0)
    k1 = k1_ref[...]
    k2 = k2_ref[...]
    eidx_rows, gate_rows = [], []
    for head in range(PEER_HEADS):
        base = head * PEER_KEY_DIM
        q1 = q[:, base:base + PEER_HALF].astype(BF16)
        q2 = q[:, base + PEER_HALF:base + PEER_KEY_DIM].astype(BF16)
        s1 = _nt_dot(k1, q1, preferred_element_type=F32)
        s2 = _nt_dot(k2, q2, preferred_element_type=F32)
        v1, i1 = _topk_rows(s1, PEER_TOPK, iota_k)
        v2, i2 = _topk_rows(s2, PEER_TOPK, iota_k)
        width = [PEER_TOPK // (a + 1) for a in range(PEER_TOPK)]
        cand = jnp.concatenate([v1[a:a + 1] + v2[:width[a]] for a in range(PEER_TOPK)], axis=0)
        cidx = jnp.concatenate([(i1[a:a + 1] * N_KEYS + i2[:width[a]]) * ROW_SUB for a in range(PEER_TOPK)], axis=0)
        top_s, top_e = [], []
        for _ in range(PEER_TOPK):
            m = jnp.max(cand, axis=0, keepdims=True)
            first = jnp.min(jnp.where(cand == m, iota_c, N_CAND), axis=0, keepdims=True)
            hit = iota_c == first
            top_s.append(m)
            top_e.append(jnp.sum(jnp.where(hit, cidx, 0), axis=0, keepdims=True))
            cand = jnp.where(hit, -jnp.inf, cand)
        top_s = jnp.concatenate(top_s, axis=0)
        e = jnp.exp(top_s - top_s[0:1])
        gate_rows.append(e / jnp.sum(e, axis=0, keepdims=True))
        eidx_rows.append(jnp.concatenate(top_e, axis=0))
    gate_t = jnp.concatenate(gate_rows, axis=0)
    eidx_t = jnp.concatenate(eidx_rows, axis=0)
    gate_ref[...] = gate_t.T
    eidx_ref[...] = lax.bitcast_convert_type(lax.bitcast_convert_type(eidx_t, F32).T, I32)


def _peer_router(x, g, w_q, sub_k1, sub_k2):
    n = x.shape[0]
    tm = PEER_TOKENS
    full = lambda shape: pl.BlockSpec(shape, lambda i: (0,) * len(shape))
    return pl.pallas_call(
        _peer_router_kernel,
        grid=(n // tm,),
        in_specs=[
            pl.BlockSpec((tm, D_MODEL), lambda i: (i, 0)),
            full((1, D_MODEL)),
            full((D_MODEL, PEER_HEADS * PEER_KEY_DIM)),
            full((N_KEYS, PEER_HALF)),
            full((N_KEYS, PEER_HALF)),
        ],
        out_specs=[
            pl.BlockSpec((tm, D_MODEL), lambda i: (i, 0)),
            pl.BlockSpec((tm, PEER_PICKS), lambda i: (i, 0)),
            pl.BlockSpec((tm, PEER_PICKS), lambda i: (i, 0)),
        ],
        out_shape=[
            jax.ShapeDtypeStruct((n, D_MODEL), F32),
            jax.ShapeDtypeStruct((n, PEER_PICKS), I32),
            jax.ShapeDtypeStruct((n, PEER_PICKS), F32),
        ],
        compiler_params=pltpu.CompilerParams(
            dimension_semantics=("arbitrary",), vmem_limit_bytes=48 * 1024 * 1024),
        name="peer_router",
    )(x, g.reshape(1, D_MODEL), w_q.astype(BF16), sub_k1.astype(BF16), sub_k2.astype(BF16))


PACK_ROWS = 512


def _pack_kernel(w_ref, out_ref):
    w = w_ref[...]
    as_bits = lambda a: lax.bitcast_convert_type(a.astype(jnp.bfloat16).astype(F32), I32)
    hi = as_bits(w[:, :ROW_WORDS]) & jnp.int32(-65536)
    lo = (as_bits(w[:, ROW_WORDS:]) >> 16) & jnp.int32(0xFFFF)
    out_ref[...] = hi | lo


def _pack_table(w):
    words = pl.pallas_call(
        _pack_kernel,
        grid=(N_EXPERTS // PACK_ROWS,),
        in_specs=[pl.BlockSpec((PACK_ROWS, D_MODEL), lambda i: (i, 0))],
        out_specs=pl.BlockSpec((PACK_ROWS, ROW_WORDS), lambda i: (i, 0)),
        out_shape=jax.ShapeDtypeStruct((N_EXPERTS, ROW_WORDS), I32),
        compiler_params=pltpu.CompilerParams(dimension_semantics=("arbitrary",)),
        name="peer_pack_table",
    )(w)
    return words.reshape(N_EXPERTS * ROW_SUB, LANES)


PLANE_STRIDE = PEER_PICKS + SUBLANES
GATHER_ROWS = ROW_SUB * PLANE_STRIDE
TOKEN_UNROLL = 2


def _gather_token(eidx_ref, tab_ref, buf_ref, t):
    for p in range(PEER_PICKS):
        row0 = pl.multiple_of(eidx_ref[t, p], ROW_SUB)
        buf_ref[pl.ds(p, ROW_SUB, stride=PLANE_STRIDE), :] = tab_ref[pl.ds(row0, ROW_SUB), :]


def _planes(buf_ref):
    hi, lo = [], []
    for s in range(ROW_SUB):
        w = buf_ref[s * PLANE_STRIDE:s * PLANE_STRIDE + PEER_PICKS, :]
        hi.append(lax.bitcast_convert_type(w & jnp.int32(-65536), F32).astype(BF16))
        lo.append(lax.bitcast_convert_type(w << 16, F32).astype(BF16))
    return hi, lo


def _peer_act_kernel(eidx_ref, tab_ref, h_ref, gate_ref, coef_ref, *bufs):
    def compute(t, buf_ref):
        hi, lo = _planes(buf_ref)
        w = jnp.concatenate(hi + lo, axis=1)
        x = jnp.broadcast_to(h_ref[pl.ds(t, 1), :], (SUBLANES, D_MODEL)).astype(BF16)
        act = _nt_dot(x, w, preferred_element_type=F32)[0:1]
        coef_ref[pl.ds(t, 1), :] = gate_ref[pl.ds(t, 1), :] * jax.nn.gelu(act)

    _token_pipeline(eidx_ref, tab_ref, bufs, h_ref.shape[0], compute)


def _token_pipeline(eidx_ref, tab_ref, bufs, n_tok, compute):
    buf_a, buf_b = bufs
    _gather_token(eidx_ref, tab_ref, buf_a, 0)
    _gather_token(eidx_ref, tab_ref, buf_b, 1)

    def step(i, carry):
        for j in range(TOKEN_UNROLL):
            t = TOKEN_UNROLL * i + j
            buf = bufs[j % 2]
            compute(t, buf)
            _gather_token(eidx_ref, tab_ref, buf, jnp.minimum(t + 2, n_tok - 1))
        return carry

    lax.fori_loop(0, n_tok // TOKEN_UNROLL, step, 0)


def _peer_out_kernel(eidx_ref, tab_ref, coef_ref, x_ref, out_ref, *bufs):
    def compute(t, buf_ref):
        hi, lo = _planes(buf_ref)
        w = jnp.concatenate(hi + lo, axis=1)
        c = jnp.broadcast_to(coef_ref[pl.ds(t, 1), :], (SUBLANES, PEER_PICKS)).astype(BF16)
        y = jnp.dot(c, w, preferred_element_type=F32)[0:1]
        out_ref[pl.ds(t, 1), :] = x_ref[pl.ds(t, 1), :] + y

    _token_pipeline(eidx_ref, tab_ref, bufs, x_ref.shape[0], compute)


def _table_spec():
    return pl.BlockSpec((N_EXPERTS * ROW_SUB, LANES), lambda i: (0, 0), pipeline_mode=pl.Buffered(1))


def _gather_scratch():
    return [pltpu.VMEM((GATHER_ROWS, LANES), I32) for _ in range(2)]


_PEER_VMEM_LIMIT = 56 * 1024 * 1024


def _peer_act(eidx, tab_u, h, gate):
    n = h.shape[0]
    tb = PEER_TOKENS
    return pl.pallas_call(
        _peer_act_kernel,
        grid=(n // tb,),
        in_specs=[
            pl.BlockSpec((tb, PEER_PICKS), lambda i: (i, 0), memory_space=pltpu.SMEM),
            _table_spec(),
            pl.BlockSpec((tb, D_MODEL), lambda i: (i, 0)),
            pl.BlockSpec((tb, PEER_PICKS), lambda i: (i, 0)),
        ],
        out_specs=pl.BlockSpec((tb, PEER_PICKS), lambda i: (i, 0)),
        out_shape=jax.ShapeDtypeStruct((n, PEER_PICKS), F32),
        scratch_shapes=_gather_scratch(),
        compiler_params=pltpu.CompilerParams(
            dimension_semantics=("arbitrary",), vmem_limit_bytes=_PEER_VMEM_LIMIT),
        name="peer_act",
    )(eidx, tab_u, h, gate)


def _peer_out(eidx, coef, tab_v, x):
    n = x.shape[0]
    tb = PEER_TOKENS
    return pl.pallas_call(
        _peer_out_kernel,
        grid=(n // tb,),
        in_specs=[
            pl.BlockSpec((tb, PEER_PICKS), lambda i: (i, 0), memory_space=pltpu.SMEM),
            _table_spec(),
            pl.BlockSpec((tb, PEER_PICKS), lambda i: (i, 0)),
            pl.BlockSpec((tb, D_MODEL), lambda i: (i, 0)),
        ],
        out_specs=pl.BlockSpec((tb, D_MODEL), lambda i: (i, 0)),
        out_shape=jax.ShapeDtypeStruct((n, D_MODEL), F32),
        scratch_shapes=_gather_scratch(),
        compiler_params=pltpu.CompilerParams(
            dimension_semantics=("arbitrary",), vmem_limit_bytes=_PEER_VMEM_LIMIT),
        name="peer_out",
    )(eidx, tab_v, coef, x)


def _peer_layer(x, g, w_q, sub_k1, sub_k2, expert_u, expert_v):
    h, eidx, gate = _peer_router(x, g, w_q, sub_k1, sub_k2)
    coef = _peer_act(eidx, _pack_table(expert_u), h, gate)
    return _peer_out(eidx, coef, _pack_table(expert_v), x)


SEQ = 4096
PAST_LEN = 16384
PAGE_SIZE = 128
CHUNK = 128
GMLP_WIDTH = 2 * D_MODEL
GMLP_GROUPS = 8
GMLP_GROUP_DIM = GMLP_WIDTH // GMLP_GROUPS
N_HEADS = 16
HEAD_DIM = D_MODEL // N_HEADS
N_KV_HEADS = 4
GROUP = N_HEADS // N_KV_HEADS
IDX_HEADS = 8
IDX_DIM = 64
TOPK_MAX = 256
Q_BLOCK = 128
ROPE_THETA = 10000.0
IDX_SCALE = (IDX_DIM ** -0.5) * (IDX_HEADS ** -0.5)
ATT_SIZES = (N_HEADS * HEAD_DIM, N_KV_HEADS * HEAD_DIM, N_KV_HEADS * HEAD_DIM,
             IDX_HEADS * IDX_DIM, IDX_DIM, IDX_HEADS)


def _rms(x, g):
    return x * lax.rsqrt(jnp.mean(x * x, axis=-1, keepdims=True) + EPS) * g


def _rope(x, pos):
    half = x.shape[-1] // 2
    inv = jnp.exp(-math.log(ROPE_THETA) * jnp.arange(half, dtype=F32) / half)
    ang = pos.astype(F32)[:, None] * inv[None, :]
    cos = jnp.cos(ang)[:, None, :]
    sin = jnp.sin(ang)[:, None, :]
    x1, x2 = x[..., :half], x[..., half:]
    return jnp.concatenate([x1 * cos - x2 * sin, x2 * cos + x1 * sin], axis=-1)


def _gmlp_mix(h, w_in, v_g, w_s, b_s, w_out):
    B, T, _ = h.shape
    uv = h @ w_in
    u, v = uv[..., :GMLP_WIDTH], uv[..., GMLP_WIDTH:]
    v = _rms(v, v_g)
    n_chunks = -(-T // CHUNK)
    pad = n_chunks * CHUNK - T
    vp = jnp.pad(v, ((0, 0), (0, pad), (0, 0))).reshape(B, n_chunks, CHUNK, GMLP_GROUPS, GMLP_GROUP_DIM)
    causal = jnp.tril(jnp.ones((CHUNK, CHUNK), dtype=bool))
    ws = jnp.where(causal, w_s, 0)
    mixed = jnp.einsum('gts,bcsgd->bctgd', ws, vp) + b_s.T[None, None, :, :, None]
    mixed = mixed.reshape(B, n_chunks * CHUNK, GMLP_WIDTH)[:, :T]
    return (u * mixed) @ w_out, v


def _att_project(h, w_in, qn_g, kn_g, ikn_g, pos):
    B, T, _ = h.shape
    p = h @ w_in
    splits = [int(s) for s in np.cumsum(ATT_SIZES)[:-1]]
    q, k, v, qi, ki, wi = jnp.split(p, splits, axis=-1)
    q = _rope(_rms(q.reshape(B, T, N_HEADS, HEAD_DIM), qn_g), pos)
    k = _rope(_rms(k.reshape(B, T, N_KV_HEADS, HEAD_DIM), kn_g), pos)
    v = v.reshape(B, T, N_KV_HEADS, HEAD_DIM)
    qi = _rope(qi.reshape(B, T, IDX_HEADS, IDX_DIM), pos)
    ki = _rope(_rms(ki, ikn_g)[:, :, None, :], pos)[:, :, 0]
    return q, k, v, qi, ki, wi


def _sparse_attend(q, qi, wi, ki_all, q_pos, n_keep, gather_kv):
    B, Tq = q.shape[0], q.shape[1]
    s = jax.nn.relu(jnp.einsum('bthd,bsd->bths', qi, ki_all))
    score = jnp.einsum('bths,bth->bts', s, wi).astype(F32) * IDX_SCALE
    key_pos = jnp.arange(ki_all.shape[1])
    admissible = key_pos[None, :] <= q_pos[:, None]
    score = jnp.where(admissible[None], score, -jnp.inf)
    _, idx = lax.top_k(score, n_keep)
    sel_ok = idx <= q_pos[None, :, None]
    k_sel, v_sel = gather_kv(idx)
    qg = q.reshape(B, Tq, N_KV_HEADS, GROUP, HEAD_DIM)
    logits = jnp.einsum('bthgd,btnhd->bthgn', qg, k_sel).astype(F32) * (HEAD_DIM ** -0.5)
    logits = jnp.where(sel_ok[:, :, None, None, :], logits, -jnp.inf)
    prob = jax.nn.softmax(logits, axis=-1)
    o = jnp.einsum('bthgn,btnhd->bthgd', prob, v_sel)
    return o.reshape(B, Tq, N_HEADS * HEAD_DIM)


def _take_rows(a, idx):
    return jax.vmap(lambda ab, ib: ab[ib])(a, idx)


DSA_TQ = 128
INT32_MIN = -(2 ** 31)


def _sortable_key(score):
    bits = lax.bitcast_convert_type(score, I32)
    return bits ^ ((bits >> 31) & jnp.int32(0x7FFFFFFF))


def _dsa_prompt_kernel(qi_ref, wi_ref, ki_ref, q_ref, k_ref, v_ref, x_ref, wo_ref, out_ref,
                       key_ref, bias_ref, *, s_eff, q_block0, n_keep):
    tq = DSA_TQ
    q0 = (q_block0 + pl.program_id(1)) * tq
    ki = ki_ref[...]
    wi = wi_ref[...]
    score = jnp.zeros((tq, s_eff), F32)
    for h in range(IDX_HEADS):
        s = _nt_dot(qi_ref[h], ki, preferred_element_type=F32)
        score = score + wi[:, h:h + 1] * jnp.maximum(s, 0.0)
    score = score * IDX_SCALE
    q_pos = q0 + lax.broadcasted_iota(I32, (tq, 1), 0)
    k_pos = lax.broadcasted_iota(I32, (tq, s_eff), 1)
    score = jnp.where(k_pos <= q_pos, score, -jnp.inf)
    key_ref[...] = _sortable_key(score)

    def search(it, thr):
        cand = thr + jnp.left_shift(jnp.int32(1), 31 - it)
        cnt = jnp.sum(jnp.where(key_ref[...] >= cand, 1, 0), axis=1, keepdims=True)
        return jnp.where(cnt >= n_keep, cand, thr)

    thr = lax.fori_loop(0, 32, search, jnp.full((tq, 1), INT32_MIN, I32))
    n_gt = jnp.sum(jnp.where(key_ref[...] > thr, 1, 0), axis=1, keepdims=True)
    need = (n_keep - n_gt).astype(F32)
    r_i = lax.broadcasted_iota(I32, (LANES, LANES), 0)
    c_i = lax.broadcasted_iota(I32, (LANES, LANES), 1)
    before = jnp.where(r_i < c_i, 1.0, 0.0).astype(BF16)
    run = jnp.zeros((tq, 1), F32)
    for c in range(s_eff // LANES):
        sl = slice(c * LANES, (c + 1) * LANES)
        key_c = key_ref[:, sl]
        tie = jnp.where(key_c == thr, 1.0, 0.0)
        rank = run + jnp.dot(tie.astype(BF16), before, preferred_element_type=F32)
        keep_tie = jnp.where(rank < need, tie, 0.0)
        keep = jnp.where(key_c > thr, 1.0, keep_tie)
        adm = (c * LANES + lax.broadcasted_iota(I32, (tq, LANES), 1)) <= q_pos
        bias_ref[:, sl] = jnp.where(adm, jnp.where(keep > 0.0, 0.0, -jnp.inf), -jnp.inf)
        run = run + jnp.sum(tie, axis=1, keepdims=True)

    outs = []
    for hk in range(N_KV_HEADS):
        k_h = k_ref[hk]
        v_h = v_ref[hk]
        for g in range(GROUP):
            logits = _nt_dot(q_ref[hk * GROUP + g], k_h, preferred_element_type=F32) + bias_ref[...]
            m = jnp.max(logits, axis=1, keepdims=True)
            p = jnp.exp(logits - m)
            den = jnp.sum(p, axis=1, keepdims=True)
            o = jnp.dot(p.astype(BF16), v_h, preferred_element_type=F32)
            outs.append(o / den)
    o_all = jnp.concatenate(outs, axis=1)
    out_ref[...] = x_ref[...] + jnp.dot(o_all.astype(BF16), wo_ref[...], preferred_element_type=F32)


def _dsa_prompt_call(qi, wi, ki, q, k, v, x, w_out, *, s_eff, q_block0, n_qblocks, n_keep):
    n_batch = x.shape[0]
    tq = DSA_TQ
    body = functools.partial(_dsa_prompt_kernel, s_eff=s_eff, q_block0=q_block0, n_keep=n_keep)
    return pl.pallas_call(
        body,
        grid=(n_batch, n_qblocks),
        in_specs=[
            pl.BlockSpec((None, IDX_HEADS, tq, IDX_DIM), lambda b, j: (b, 0, q_block0 + j, 0)),
            pl.BlockSpec((None, tq, IDX_HEADS), lambda b, j: (b, q_block0 + j, 0)),
            pl.BlockSpec((None, s_eff, IDX_DIM), lambda b, j: (b, 0, 0)),
            pl.BlockSpec((None, N_HEADS, tq, HEAD_DIM), lambda b, j: (b, 0, q_block0 + j, 0)),
            pl.BlockSpec((None, N_KV_HEADS, s_eff, HEAD_DIM), lambda b, j: (b, 0, 0, 0)),
            pl.BlockSpec((None, N_KV_HEADS, s_eff, HEAD_DIM), lambda b, j: (b, 0, 0, 0)),
            pl.BlockSpec((None, tq, D_MODEL), lambda b, j: (b, q_block0 + j, 0)),
            pl.BlockSpec((D_MODEL, D_MODEL), lambda b, j: (0, 0)),
        ],
        out_specs=pl.BlockSpec((None, tq, D_MODEL), lambda b, j: (b, j, 0)),
        out_shape=jax.ShapeDtypeStruct((n_batch, n_qblocks * tq, D_MODEL), F32),
        scratch_shapes=[pltpu.VMEM((tq, s_eff), I32), pltpu.VMEM((tq, s_eff), F32)],
        compiler_params=pltpu.CompilerParams(
            dimension_semantics=("arbitrary", "arbitrary"), vmem_limit_bytes=56 * 1024 * 1024),
        name=f"dsa_prompt_{s_eff}",
    )(qi, wi, ki, q, k, v, x, w_out)


DSA_CLASS = 512


def _att_prompt(x, h, w_in, qn_g, kn_g, ikn_g, w_out, class_rows=DSA_CLASS):
    B, S, _ = h.shape
    pos = jnp.arange(S)
    q, k, v, qi, ki, wi = _att_project(h, w_in, qn_g, kn_g, ikn_g, pos)
    n_keep = min(TOPK_MAX, S // 4)
    head_major = lambda a: a.astype(BF16).transpose(0, 2, 1, 3)
    qh = head_major(q * (HEAD_DIM ** -0.5))
    kh, vh, qih = head_major(k), head_major(v), head_major(qi)
    ki_b = ki.astype(BF16)
    wo_b = w_out.astype(BF16)
    per_call = class_rows // DSA_TQ
    pieces = [
        _dsa_prompt_call(qih, wi, ki_b, qh, kh, vh, x, wo_b, s_eff=(c + 1) * class_rows,
                         q_block0=c * per_call, n_qblocks=per_call, n_keep=n_keep)
        for c in range(S // class_rows)
    ]
    return jnp.concatenate(pieces, axis=1), k, v, ki


def _att_sample(h, ck, cv, cik, page_table, w_in, qn_g, kn_g, ikn_g, w_out):
    B, T, _ = h.shape
    pos = PAST_LEN + jnp.arange(T)
    q, k, v, qi, ki, wi = _att_project(h, w_in, qn_g, kn_g, ikn_g, pos)
    n_pages = PAST_LEN // PAGE_SIZE
    ki_past = cik[page_table].reshape(B, n_pages * PAGE_SIZE, IDX_DIM)
    ki_all = jnp.concatenate([ki_past, ki], axis=1)
    n_keep = min(TOPK_MAX, (PAST_LEN + T) // 4)

    def gather(idx):
        is_new = (idx >= PAST_LEN)[..., None, None]
        past = jnp.minimum(idx, PAST_LEN - 1)
        phys = jnp.take_along_axis(page_table, (past // PAGE_SIZE).reshape(B, -1), axis=1).reshape(idx.shape)
        off = past % PAGE_SIZE
        new = jnp.clip(idx - PAST_LEN, 0, T - 1)
        k_sel = jnp.where(is_new, _take_rows(k, new), ck[phys, off])
        v_sel = jnp.where(is_new, _take_rows(v, new), cv[phys, off])
        return k_sel, v_sel

    o = _sparse_attend(q, qi, wi, ki_all, pos, n_keep, gather)
    return o @ w_out, k, v, ki


def kernel(x_prompt, x_sample, cache_k, cache_v, cache_idx_k, page_table, norm_mix_g, norm_ffn_g,
           gmlp_w_in, gmlp_v_g, gmlp_w_s, gmlp_b_s, gmlp_w_out,
           att_w_in, att_q_norm_g, att_k_norm_g, att_idx_k_norm_g, att_w_out,
           peer_w_q, peer_sub_k1, peer_sub_k2, peer_u, peer_v):
    n_batch, seq, _ = x_prompt.shape
    n_dec, dec_seq, _ = x_sample.shape
    n_prompt = n_batch * seq
    n_tok = n_prompt + n_dec * dec_seq
    n_pad = -(-n_tok // PEER_TOKENS) * PEER_TOKENS

    def peer(xp, xs, layer):
        x = jnp.concatenate([xp.reshape(n_prompt, D_MODEL), xs.reshape(n_dec * dec_seq, D_MODEL),
                             jnp.zeros((n_pad - n_tok, D_MODEL), F32)], axis=0)
        y = _peer_layer(x, norm_ffn_g[layer], peer_w_q[layer], peer_sub_k1[layer], peer_sub_k2[layer],
                        peer_u[layer], peer_v[layer])
        return y[:n_prompt].reshape(xp.shape), y[n_prompt:n_tok].reshape(xs.shape)

    xp, xs = x_prompt, x_sample
    hp = _rms(xp, norm_mix_g[0])
    hs = _rms(xs, norm_mix_g[0])
    op, _ = _gmlp_mix(hp, gmlp_w_in[0], gmlp_v_g[0], gmlp_w_s[0], gmlp_b_s[0], gmlp_w_out[0])
    os_, v_rows = _gmlp_mix(hs, gmlp_w_in[0], gmlp_v_g[0], gmlp_w_s[0], gmlp_b_s[0], gmlp_w_out[0])
    xp, xs = peer(xp + op, xs + os_, 0)
    hp = _rms(xp, norm_mix_g[1])
    hs = _rms(xs, norm_mix_g[1])
    xp, kp, vp, ip = _att_prompt(xp, hp, att_w_in[0], att_q_norm_g[0], att_k_norm_g[0],
                                 att_idx_k_norm_g[0], att_w_out[0])
    os_, ksm, vsm, ism = _att_sample(hs, cache_k[0], cache_v[0], cache_idx_k[0], page_table,
                                     att_w_in[0], att_q_norm_g[0], att_k_norm_g[0],
                                     att_idx_k_norm_g[0], att_w_out[0])
    xp, xs = peer(xp, xs + os_, 1)
    return (xp, xs, kp[None], vp[None], ip[None], ksm[None], vsm[None], ism[None], v_rows[None])
```

```python
import functools
import math

import jax
import jax.numpy as jnp
import numpy as np
from jax import lax
from jax.experimental import pallas as pl
from jax.experimental.pallas import tpu as pltpu

F32 = jnp.float32
BF16 = jnp.bfloat16
I32 = jnp.int32

D_MODEL = 1024
EPS = 1e-6

LANES = 128
SUBLANES = 8
MIB = 1024 * 1024
VMEM_LIMIT = 56 * MIB

ROW_BLOCK = 256

PEER_HEADS = 8
N_KEYS = 128
N_EXPERTS = N_KEYS * N_KEYS
PEER_TOPK = 16
PEER_KEY_DIM = 256
PEER_HALF = PEER_KEY_DIM // 2
PEER_PICKS = PEER_HEADS * PEER_TOPK
ROW_WORDS = D_MODEL // 2
ROW_SUB = ROW_WORDS // LANES
PEER_TOKENS = 128
N_CAND = sum(PEER_TOPK // (a + 1) for a in range(PEER_TOPK))

CHUNK = 128
GMLP_WIDTH = 2 * D_MODEL
GMLP_GROUPS = 8
GMLP_GROUP_DIM = GMLP_WIDTH // GMLP_GROUPS

PAST_LEN = 16384
PAGE_SIZE = 128
N_HEADS = 16
HEAD_DIM = D_MODEL // N_HEADS
N_KV_HEADS = 4
GROUP = N_HEADS // N_KV_HEADS
IDX_HEADS = 8
IDX_DIM = 64
TOPK_MAX = 256
ROPE_THETA = 10000.0
IDX_SCALE = (IDX_DIM ** -0.5) * (IDX_HEADS ** -0.5)
ATT_SIZES = (N_HEADS * HEAD_DIM, N_KV_HEADS * HEAD_DIM, N_KV_HEADS * HEAD_DIM,
             IDX_HEADS * IDX_DIM, IDX_DIM, IDX_HEADS)
ATT_COLS = sum(ATT_SIZES)
ATT_COLS_PAD = -(-ATT_COLS // LANES) * LANES
Q_COL, K_COL, V_COL, QI_COL, KI_COL = (int(c) for c in np.cumsum((0,) + ATT_SIZES[:4]))


def _nt_dot(a, b, **kw):
    return lax.dot_general(a, b, (((1,), (1,)), ((), ())), **kw)


def _rms_rows(x, g):
    return x * lax.rsqrt(jnp.mean(x * x, axis=-1, keepdims=True) + EPS) * g


def _full_spec(shape, **kw):
    return pl.BlockSpec(shape, lambda *_: (0,) * len(shape), **kw)


def _topk_rows(s, k, iota):
    n_rows = s.shape[0]
    vals, idxs = [], []
    for _ in range(k):
        m = jnp.max(s, axis=0, keepdims=True)
        first = jnp.min(jnp.where(s == m, iota, n_rows), axis=0, keepdims=True)
        vals.append(m)
        idxs.append(first)
        s = jnp.where(iota == first, -jnp.inf, s)
    return jnp.concatenate(vals, axis=0), jnp.concatenate(idxs, axis=0)


def _peer_router_kernel(x_ref, g_ref, wq_ref, k1_ref, k2_ref, h_ref, eidx_ref, gate_ref):
    h = _rms_rows(x_ref[...], g_ref[...])
    h_ref[...] = h
    q = jnp.dot(h.astype(BF16), wq_ref[...], preferred_element_type=F32)
    n_tok = h.shape[0]
    iota_k = lax.broadcasted_iota(I32, (N_KEYS, n_tok), 0)
    iota_c = lax.broadcasted_iota(I32, (N_CAND, n_tok), 0)
    k1 = k1_ref[...]
    k2 = k2_ref[...]
    eidx_rows, gate_rows = [], []
    for head in range(PEER_HEADS):
        base = head * PEER_KEY_DIM
        q1 = q[:, base:base + PEER_HALF].astype(BF16)
        q2 = q[:, base + PEER_HALF:base + PEER_KEY_DIM].astype(BF16)
        s1 = _nt_dot(k1, q1, preferred_element_type=F32)
        s2 = _nt_dot(k2, q2, preferred_element_type=F32)
        v1, i1 = _topk_rows(s1, PEER_TOPK, iota_k)
        v2, i2 = _topk_rows(s2, PEER_TOPK, iota_k)
        width = [PEER_TOPK // (a + 1) for a in range(PEER_TOPK)]
        cand = jnp.concatenate([v1[a:a + 1] + v2[:width[a]] for a in range(PEER_TOPK)], axis=0)
        cidx = jnp.concatenate([(i1[a:a + 1] * N_KEYS + i2[:width[a]]) * ROW_SUB for a in range(PEER_TOPK)], axis=0)
        top_s, top_e = [], []
        for _ in range(PEER_TOPK):
            m = jnp.max(cand, axis=0, keepdims=True)
            first = jnp.min(jnp.where(cand == m, iota_c, N_CAND), axis=0, keepdims=True)
            hit = iota_c == first
            top_s.append(m)
            top_e.append(jnp.sum(jnp.where(hit, cidx, 0), axis=0, keepdims=True))
            cand = jnp.where(hit, -jnp.inf, cand)
        top_s = jnp.concatenate(top_s, axis=0)
        e = jnp.exp(top_s - top_s[0:1])
        gate_rows.append(e / jnp.sum(e, axis=0, keepdims=True))
        eidx_rows.append(jnp.concatenate(top_e, axis=0))
    gate_t = jnp.concatenate(gate_rows, axis=0)
    eidx_t = jnp.concatenate(eidx_rows, axis=0)
    gate_ref[...] = gate_t.T
    eidx_ref[...] = lax.bitcast_convert_type(lax.bitcast_convert_type(eidx_t, F32).T, I32)


def _peer_router(x, g, w_q, sub_k1, sub_k2):
    n = x.shape[0]
    tm = PEER_TOKENS
    return pl.pallas_call(
        _peer_router_kernel,
        grid=(n // tm,),
        in_specs=[
            pl.BlockSpec((tm, D_MODEL), lambda i: (i, 0)),
            _full_spec((1, D_MODEL)),
            _full_spec((D_MODEL, PEER_HEADS * PEER_KEY_DIM)),
            _full_spec((N_KEYS, PEER_HALF)),
            _full_spec((N_KEYS, PEER_HALF)),
        ],
        out_specs=[
            pl.BlockSpec((tm, D_MODEL), lambda i: (i, 0)),
            pl.BlockSpec((tm, PEER_PICKS), lambda i: (i, 0)),
            pl.BlockSpec((tm, PEER_PICKS), lambda i: (i, 0)),
        ],
        out_shape=[
            jax.ShapeDtypeStruct((n, D_MODEL), F32),
            jax.ShapeDtypeStruct((n, PEER_PICKS), I32),
            jax.ShapeDtypeStruct((n, PEER_PICKS), F32),
        ],
        compiler_params=pltpu.CompilerParams(dimension_semantics=("arbitrary",), vmem_limit_bytes=VMEM_LIMIT),
        name="peer_router",
    )(x, g.reshape(1, D_MODEL), w_q.astype(BF16), sub_k1.astype(BF16), sub_k2.astype(BF16))


PACK_ROWS = 512


def _pack_kernel(w_ref, out_ref):
    w = w_ref[...]
    as_bits = lambda a: lax.bitcast_convert_type(a.astype(jnp.bfloat16).astype(F32), I32)
    hi = as_bits(w[:, :ROW_WORDS]) & jnp.int32(-65536)
    lo = (as_bits(w[:, ROW_WORDS:]) >> 16) & jnp.int32(0xFFFF)
    out_ref[...] = hi | lo


def _pack_table(w):
    words = pl.pallas_call(
        _pack_kernel,
        grid=(N_EXPERTS // PACK_ROWS,),
        in_specs=[pl.BlockSpec((PACK_ROWS, D_MODEL), lambda i: (i, 0))],
        out_specs=pl.BlockSpec((PACK_ROWS, ROW_WORDS), lambda i: (i, 0)),
        out_shape=jax.ShapeDtypeStruct((N_EXPERTS, ROW_WORDS), I32),
        compiler_params=pltpu.CompilerParams(dimension_semantics=("arbitrary",)),
        name="peer_pack_table",
    )(w)
    return words.reshape(N_EXPERTS * ROW_SUB, LANES)


PLANE_STRIDE = PEER_PICKS + SUBLANES
GATHER_ROWS = ROW_SUB * PLANE_STRIDE
TOKEN_UNROLL = 2


def _gather_token(eidx_ref, tab_ref, buf_ref, t):
    for p in range(PEER_PICKS):
        row0 = pl.multiple_of(eidx_ref[t, p], ROW_SUB)
        buf_ref[pl.ds(p, ROW_SUB, stride=PLANE_STRIDE), :] = tab_ref[pl.ds(row0, ROW_SUB), :]


def _gathered_rows(buf_ref):
    hi, lo = [], []
    for s in range(ROW_SUB):
        w = buf_ref[s * PLANE_STRIDE:s * PLANE_STRIDE + PEER_PICKS, :]
        hi.append(lax.bitcast_convert_type(w & jnp.int32(-65536), F32).astype(BF16))
        lo.append(lax.bitcast_convert_type(w << 16, F32).astype(BF16))
    return jnp.concatenate(hi + lo, axis=1)


def _token_pipeline(eidx_ref, tab_ref, bufs, n_tok, compute):
    _gather_token(eidx_ref, tab_ref, bufs[0], 0)
    _gather_token(eidx_ref, tab_ref, bufs[1], 1)

    def step(i, carry):
        for j in range(TOKEN_UNROLL):
            t = TOKEN_UNROLL * i + j
            buf = bufs[j % 2]
            compute(t, buf)
            _gather_token(eidx_ref, tab_ref, buf, jnp.minimum(t + 2, n_tok - 1))
        return carry

    lax.fori_loop(0, n_tok // TOKEN_UNROLL, step, 0)


def _peer_act_kernel(eidx_ref, tab_ref, h_ref, gate_ref, coef_ref, *bufs):
    def compute(t, buf_ref):
        w = _gathered_rows(buf_ref)
        x = jnp.broadcast_to(h_ref[pl.ds(t, 1), :], (SUBLANES, D_MODEL)).astype(BF16)
        act = _nt_dot(x, w, preferred_element_type=F32)[0:1]
        coef_ref[pl.ds(t, 1), :] = gate_ref[pl.ds(t, 1), :] * jax.nn.gelu(act)

    _token_pipeline(eidx_ref, tab_ref, bufs, h_ref.shape[0], compute)


def _peer_out_kernel(eidx_ref, tab_ref, coef_ref, x_ref, out_ref, *bufs):
    def compute(t, buf_ref):
        w = _gathered_rows(buf_ref)
        c = jnp.broadcast_to(coef_ref[pl.ds(t, 1), :], (SUBLANES, PEER_PICKS)).astype(BF16)
        y = jnp.dot(c, w, preferred_element_type=F32)[0:1]
        out_ref[pl.ds(t, 1), :] = x_ref[pl.ds(t, 1), :] + y

    _token_pipeline(eidx_ref, tab_ref, bufs, x_ref.shape[0], compute)


def _peer_gather_call(body, name, eidx, tab, a, b, out_cols):
    n = eidx.shape[0]
    tb = PEER_TOKENS
    rows = lambda cols: pl.BlockSpec((tb, cols), lambda i: (i, 0))
    return pl.pallas_call(
        body,
        grid=(n // tb,),
        in_specs=[
            pl.BlockSpec((tb, PEER_PICKS), lambda i: (i, 0), memory_space=pltpu.SMEM),
            _full_spec((N_EXPERTS * ROW_SUB, LANES), pipeline_mode=pl.Buffered(1)),
            rows(a.shape[1]),
            rows(b.shape[1]),
        ],
        out_specs=rows(out_cols),
        out_shape=jax.ShapeDtypeStruct((n, out_cols), F32),
        scratch_shapes=[pltpu.VMEM((GATHER_ROWS, LANES), I32) for _ in range(2)],
        compiler_params=pltpu.CompilerParams(dimension_semantics=("arbitrary",), vmem_limit_bytes=VMEM_LIMIT),
        name=name,
    )(eidx, tab, a, b)


def _peer_layer(x, g, w_q, sub_k1, sub_k2, expert_u, expert_v):
    h, eidx, gate = _peer_router(x, g, w_q, sub_k1, sub_k2)
    coef = _peer_gather_call(_peer_act_kernel, "peer_act", eidx, _pack_table(expert_u), h, gate, PEER_PICKS)
    return _peer_gather_call(_peer_out_kernel, "peer_out", eidx, _pack_table(expert_v), coef, x, D_MODEL)


def _gmlp_kernel(x_ref, g_ref, win_ref, vg_ref, ws_ref, bst_ref, s0_ref, b0_ref, wout_ref,
                 out_ref, vnew_ref, gated_ref, *, n_chunk_blocks):
    x = x_ref[...]
    h = _rms_rows(x, g_ref[...])
    uv = jnp.dot(h.astype(BF16), win_ref[...], preferred_element_type=F32)
    v = _rms_rows(uv[:, GMLP_WIDTH:], vg_ref[...])
    is_chunked = pl.program_id(0) < n_chunk_blocks

    @pl.when(is_chunked)
    def _():
        r_i = lax.broadcasted_iota(I32, (CHUNK, CHUNK), 0)
        c_i = lax.broadcasted_iota(I32, (CHUNK, CHUNK), 1)
        causal = c_i <= r_i
        for g in range(GMLP_GROUPS):
            cols = slice(g * GMLP_GROUP_DIM, (g + 1) * GMLP_GROUP_DIM)
            ws = jnp.where(causal, ws_ref[g], 0.0).astype(BF16)
            bias = bst_ref[:, g:g + 1]
            for c in range(x.shape[0] // CHUNK):
                rows = slice(c * CHUNK, (c + 1) * CHUNK)
                mixed = jnp.dot(ws, v[rows, cols].astype(BF16), preferred_element_type=F32) + bias
                gated_ref[rows, cols] = (uv[rows, cols] * mixed).astype(BF16)

    @pl.when(jnp.logical_not(is_chunked))
    def _():
        gated_ref[...] = (uv[:, :GMLP_WIDTH] * (v * s0_ref[...] + b0_ref[...])).astype(BF16)
        vnew_ref[...] = v

    out_ref[...] = x + jnp.dot(gated_ref[...], wout_ref[...], preferred_element_type=F32)


def _gmlp_layer(x_all, n_chunk_rows, g, w_in, v_g, w_s, b_s, w_out):
    n = x_all.shape[0]
    tb = ROW_BLOCK
    spread = lambda a: jnp.repeat(a, GMLP_GROUP_DIM).reshape(1, GMLP_WIDTH)
    body = functools.partial(_gmlp_kernel, n_chunk_blocks=n_chunk_rows // tb)
    return pl.pallas_call(
        body,
        grid=(n // tb,),
        in_specs=[
            pl.BlockSpec((tb, D_MODEL), lambda i: (i, 0)),
            _full_spec((1, D_MODEL)),
            _full_spec((D_MODEL, 2 * GMLP_WIDTH), pipeline_mode=pl.Buffered(1)),
            _full_spec((1, GMLP_WIDTH)),
            _full_spec((GMLP_GROUPS, CHUNK, CHUNK)),
            _full_spec((CHUNK, GMLP_GROUPS)),
            _full_spec((1, GMLP_WIDTH)),
            _full_spec((1, GMLP_WIDTH)),
            _full_spec((GMLP_WIDTH, D_MODEL), pipeline_mode=pl.Buffered(1)),
        ],
        out_specs=[
            pl.BlockSpec((tb, D_MODEL), lambda i: (i, 0)),
            _full_spec((tb, GMLP_WIDTH)),
        ],
        out_shape=[
            jax.ShapeDtypeStruct((n, D_MODEL), F32),
            jax.ShapeDtypeStruct((tb, GMLP_WIDTH), F32),
        ],
        scratch_shapes=[pltpu.VMEM((tb, GMLP_WIDTH), BF16)],
        compiler_params=pltpu.CompilerParams(dimension_semantics=("arbitrary",), vmem_limit_bytes=VMEM_LIMIT),
        name="gmlp_mixer",
    )(x_all, g.reshape(1, D_MODEL), w_in.astype(BF16), v_g.reshape(1, GMLP_WIDTH), w_s, b_s.T,
      spread(w_s[:, 0, 0]), spread(b_s[:, 0]), w_out.astype(BF16))


def _att_proj_kernel(x_ref, g_ref, w_ref, qg_ref, kg_ref, ikg_ref, cos_ref, sin_ref,
                     q_ref, k_ref, v_ref, qi_ref, ki_ref, wi_ref, nk_ref, nv_ref, nki_ref):
    h = _rms_rows(x_ref[...], g_ref[...])
    p = jnp.dot(h.astype(BF16), w_ref[...], preferred_element_type=F32)
    n_tok = p.shape[0]
    cos = cos_ref[...]
    sin = sin_ref[...]
    lane = lax.broadcasted_iota(I32, (n_tok, LANES), 1)
    first_half = (lane & (HEAD_DIM // 2)) == 0
    r_i = lax.broadcasted_iota(I32, (LANES, LANES), 0)
    c_i = lax.broadcasted_iota(I32, (LANES, LANES), 1)
    head_mean = jnp.where((r_i // HEAD_DIM) == (c_i // HEAD_DIM), 1.0 / HEAD_DIM, 0.0).astype(BF16)

    def slab(col):
        return p[:, col:col + LANES]

    def head_rms(a, gain):
        ms = jnp.dot((a * a).astype(BF16), head_mean, preferred_element_type=F32)
        return a * lax.rsqrt(ms + EPS) * gain

    def rope(a):
        partner = jnp.where(first_half, pltpu.roll(a, LANES - HEAD_DIM // 2, 1), pltpu.roll(a, HEAD_DIM // 2, 1))
        return a * cos + partner * sin

    def put_heads(ref, i, a):
        ref[2 * i] = a[:, :HEAD_DIM].astype(BF16)
        ref[2 * i + 1] = a[:, HEAD_DIM:].astype(BF16)

    for i in range(N_HEADS // 2):
        put_heads(q_ref, i, rope(head_rms(slab(Q_COL + i * LANES), qg_ref[...])) * (HEAD_DIM ** -0.5))
    for i in range(N_KV_HEADS // 2):
        k = rope(head_rms(slab(K_COL + i * LANES), kg_ref[...]))
        nk_ref[:, i * LANES:(i + 1) * LANES] = k
        put_heads(k_ref, i, k)
        v = slab(V_COL + i * LANES)
        nv_ref[:, i * LANES:(i + 1) * LANES] = v
        put_heads(v_ref, i, v)
    for i in range(IDX_HEADS // 2):
        put_heads(qi_ref, i, rope(slab(QI_COL + i * LANES)))
    tail = slab(KI_COL)
    ki = rope(head_rms(tail, ikg_ref[...]))[:, :IDX_DIM]
    nki_ref[...] = ki
    ki_ref[...] = ki.astype(BF16)
    wi_ref[...] = tail[:, IDX_DIM:IDX_DIM + IDX_HEADS]


def _att_project(x_all, pos, g, w_in, qn_g, kn_g, ikn_g):
    n = x_all.shape[0]
    tb = ROW_BLOCK
    half = HEAD_DIM // 2
    inv = jnp.exp(-math.log(ROPE_THETA) * jnp.arange(half, dtype=F32) / half)
    ang = pos.astype(F32)[:, None] * inv[None, :]
    cos = jnp.tile(jnp.cos(ang), (1, LANES // half))
    sin = jnp.tile(jnp.concatenate([-jnp.sin(ang), jnp.sin(ang)], axis=1), (1, LANES // HEAD_DIM))
    pair = lambda gain: jnp.tile(gain, LANES // HEAD_DIM).reshape(1, LANES)
    w_pad = jnp.pad(w_in, ((0, 0), (0, ATT_COLS_PAD - ATT_COLS))).astype(BF16)
    heads = lambda n_heads: (pl.BlockSpec((n_heads, tb, HEAD_DIM), lambda i: (0, i, 0)),
                             jax.ShapeDtypeStruct((n_heads, n, HEAD_DIM), BF16))
    rows = lambda cols, dt: (pl.BlockSpec((tb, cols), lambda i: (i, 0)), jax.ShapeDtypeStruct((n, cols), dt))
    outs = [heads(N_HEADS), heads(N_KV_HEADS), heads(N_KV_HEADS), heads(IDX_HEADS), rows(IDX_DIM, BF16),
            rows(IDX_HEADS, F32), rows(N_KV_HEADS * HEAD_DIM, F32), rows(N_KV_HEADS * HEAD_DIM, F32),
            rows(IDX_DIM, F32)]
    return pl.pallas_call(
        _att_proj_kernel,
        grid=(n // tb,),
        in_specs=[
            pl.BlockSpec((tb, D_MODEL), lambda i: (i, 0)),
            _full_spec((1, D_MODEL)),
            _full_spec((D_MODEL, ATT_COLS_PAD), pipeline_mode=pl.Buffered(1)),
            _full_spec((1, LANES)),
            _full_spec((1, LANES)),
            _full_spec((1, LANES)),
            pl.BlockSpec((tb, LANES), lambda i: (i, 0)),
            pl.BlockSpec((tb, LANES), lambda i: (i, 0)),
        ],
        out_specs=[o[0] for o in outs],
        out_shape=[o[1] for o in outs],
        compiler_params=pltpu.CompilerParams(dimension_semantics=("arbitrary",), vmem_limit_bytes=VMEM_LIMIT),
        name="att_project",
    )(x_all, g.reshape(1, D_MODEL), w_pad, pair(qn_g), pair(kn_g), pair(ikn_g), cos, sin)


DSA_TQ = 128
DSA_CLASS = 512
INT32_MIN = -(2 ** 31)


def _sortable_key(score):
    bits = lax.bitcast_convert_type(score, I32)
    return bits ^ ((bits >> 31) & jnp.int32(0x7FFFFFFF))


def _dsa_prompt_kernel(qi_ref, wi_ref, ki_ref, q_ref, k_ref, v_ref, x_ref, wo_ref, out_ref,
                       key_ref, bias_ref, *, s_eff, q_block0, n_keep):
    tq = DSA_TQ
    q0 = (q_block0 + pl.program_id(1)) * tq
    ki = ki_ref[...]
    wi = wi_ref[...]
    score = jnp.zeros((tq, s_eff), F32)
    for h in range(IDX_HEADS):
        s = _nt_dot(qi_ref[h], ki, preferred_element_type=F32)
        score = score + wi[:, h:h + 1] * jnp.maximum(s, 0.0)
    score = score * IDX_SCALE
    q_pos = q0 + lax.broadcasted_iota(I32, (tq, 1), 0)
    k_pos = lax.broadcasted_iota(I32, (tq, s_eff), 1)
    score = jnp.where(k_pos <= q_pos, score, -jnp.inf)
    key_ref[...] = _sortable_key(score)

    def search(it, thr):
        cand = thr + jnp.left_shift(jnp.int32(1), 31 - it)
        cnt = jnp.sum(jnp.where(key_ref[...] >= cand, 1, 0), axis=1, keepdims=True)
        return jnp.where(cnt >= n_keep, cand, thr)

    thr = lax.fori_loop(0, 32, search, jnp.full((tq, 1), INT32_MIN, I32))
    n_gt = jnp.sum(jnp.where(key_ref[...] > thr, 1, 0), axis=1, keepdims=True)
    need = (n_keep - n_gt).astype(F32)
    r_i = lax.broadcasted_iota(I32, (LANES, LANES), 0)
    c_i = lax.broadcasted_iota(I32, (LANES, LANES), 1)
    before = jnp.where(r_i < c_i, 1.0, 0.0).astype(BF16)
    run = jnp.zeros((tq, 1), F32)
    for c in range(s_eff // LANES):
        sl = slice(c * LANES, (c + 1) * LANES)
        key_c = key_ref[:, sl]
        tie = jnp.where(key_c == thr, 1.0, 0.0)
        rank = run + jnp.dot(tie.astype(BF16), before, preferred_element_type=F32)
        keep_tie = jnp.where(rank < need, tie, 0.0)
        keep = jnp.where(key_c > thr, 1.0, keep_tie)
        adm = (c * LANES + lax.broadcasted_iota(I32, (tq, LANES), 1)) <= q_pos
        bias_ref[:, sl] = jnp.where(adm, jnp.where(keep > 0.0, 0.0, -jnp.inf), -jnp.inf)
        run = run + jnp.sum(tie, axis=1, keepdims=True)

    outs = []
    for hk in range(N_KV_HEADS):
        k_h = k_ref[hk]
        v_h = v_ref[hk]
        for g in range(GROUP):
            logits = _nt_dot(q_ref[hk * GROUP + g], k_h, preferred_element_type=F32) + bias_ref[...]
            m = jnp.max(logits, axis=1, keepdims=True)
            p = jnp.exp(logits - m)
            den = jnp.sum(p, axis=1, keepdims=True)
            o = jnp.dot(p.astype(BF16), v_h, preferred_element_type=F32)
            outs.append(o / den)
    o_all = jnp.concatenate(outs, axis=1)
    out_ref[...] = x_ref[...] + jnp.dot(o_all.astype(BF16), wo_ref[...], preferred_element_type=F32)


def _dsa_prompt_call(x_all, qi, wi, ki, q, k, v, w_out, *, n_batch, seq, s_eff, q_block0, n_qblocks, n_keep):
    tq = DSA_TQ
    per_seq = seq // tq
    q_rows = lambda b, j: b * per_seq + q_block0 + j
    keys = lambda n_heads: pl.BlockSpec((pl.Element(n_heads), pl.Element(s_eff), pl.Element(HEAD_DIM)),
                                        lambda b, j: (0, b * seq, 0))
    body = functools.partial(_dsa_prompt_kernel, s_eff=s_eff, q_block0=q_block0, n_keep=n_keep)
    return pl.pallas_call(
        body,
        grid=(n_batch, n_qblocks),
        in_specs=[
            pl.BlockSpec((IDX_HEADS, tq, IDX_DIM), lambda b, j: (0, q_rows(b, j), 0)),
            pl.BlockSpec((tq, IDX_HEADS), lambda b, j: (q_rows(b, j), 0)),
            pl.BlockSpec((pl.Element(s_eff), pl.Element(IDX_DIM)), lambda b, j: (b * seq, 0)),
            pl.BlockSpec((N_HEADS, tq, HEAD_DIM), lambda b, j: (0, q_rows(b, j), 0)),
            keys(N_KV_HEADS),
            keys(N_KV_HEADS),
            pl.BlockSpec((tq, D_MODEL), lambda b, j: (q_rows(b, j), 0)),
            _full_spec((D_MODEL, D_MODEL)),
        ],
        out_specs=pl.BlockSpec((tq, D_MODEL), lambda b, j: (q_rows(b, j), 0)),
        out_shape=jax.ShapeDtypeStruct(x_all.shape, F32),
        input_output_aliases={6: 0},
        scratch_shapes=[pltpu.VMEM((tq, s_eff), I32), pltpu.VMEM((tq, s_eff), F32)],
        compiler_params=pltpu.CompilerParams(
            dimension_semantics=("arbitrary", "arbitrary"), vmem_limit_bytes=VMEM_LIMIT),
        name=f"dsa_prompt_{s_eff}",
    )(qi, wi, ki, q, k, v, x_all, w_out)


def _dsa_prompt(x_all, proj, w_out, n_batch, seq, class_rows=DSA_CLASS):
    q, k, v, qi, ki, wi = proj
    n_keep = min(TOPK_MAX, seq // 4)
    wo_b = w_out.astype(BF16)
    per_call = class_rows // DSA_TQ
    for c in range(seq // class_rows):
        x_all = _dsa_prompt_call(x_all, qi, wi, ki, q, k, v, wo_b, n_batch=n_batch, seq=seq,
                                 s_eff=(c + 1) * class_rows, q_block0=c * per_call, n_qblocks=per_call,
                                 n_keep=n_keep)
    return x_all


def _take_rows(a, idx):
    return jax.vmap(lambda ab, ib: ab[ib])(a, idx)


def _att_sample(q, k, v, qi, ki, wi, ck, cv, cik, page_table, w_out):
    B, T = q.shape[0], q.shape[1]
    pos = PAST_LEN + jnp.arange(T)
    n_pages = PAST_LEN // PAGE_SIZE
    ki_past = cik[page_table].reshape(B, n_pages * PAGE_SIZE, IDX_DIM)
    ki_all = jnp.concatenate([ki_past, ki], axis=1)
    n_keep = min(TOPK_MAX, (PAST_LEN + T) // 4)
    s = jax.nn.relu(jnp.einsum('bthd,bsd->bths', qi, ki_all))
    score = jnp.einsum('bths,bth->bts', s, wi).astype(F32) * IDX_SCALE
    key_pos = jnp.arange(ki_all.shape[1])
    score = jnp.where((key_pos[None, :] <= pos[:, None])[None], score, -jnp.inf)
    _, idx = lax.top_k(score, n_keep)
    sel_ok = idx <= pos[None, :, None]
    is_new = (idx >= PAST_LEN)[..., None, None]
    past = jnp.minimum(idx, PAST_LEN - 1)
    phys = jnp.take_along_axis(page_table, (past // PAGE_SIZE).reshape(B, -1), axis=1).reshape(idx.shape)
    off = past % PAGE_SIZE
    new = jnp.clip(idx - PAST_LEN, 0, T - 1)
    k_sel = jnp.where(is_new, _take_rows(k, new), ck[phys, off])
    v_sel = jnp.where(is_new, _take_rows(v, new), cv[phys, off])
    qg = q.reshape(B, T, N_KV_HEADS, GROUP, HEAD_DIM)
    logits = jnp.einsum('bthgd,btnhd->bthgn', qg, k_sel).astype(F32) * (HEAD_DIM ** -0.5)
    logits = jnp.where(sel_ok[:, :, None, None, :], logits, -jnp.inf)
    prob = jax.nn.softmax(logits, axis=-1)
    o = jnp.einsum('bthgn,btnhd->bthgd', prob, v_sel)
    return o.reshape(B, T, N_HEADS * HEAD_DIM) @ w_out


def kernel(x_prompt, x_sample, cache_k, cache_v, cache_idx_k, page_table, norm_mix_g, norm_ffn_g,
           gmlp_w_in, gmlp_v_g, gmlp_w_s, gmlp_b_s, gmlp_w_out,
           att_w_in, att_q_norm_g, att_k_norm_g, att_idx_k_norm_g, att_w_out,
           peer_w_q, peer_sub_k1, peer_sub_k2, peer_u, peer_v):
    n_batch, seq, _ = x_prompt.shape
    n_dec, dec_seq, _ = x_sample.shape
    assert dec_seq == 1 and seq % DSA_CLASS == 0 and (n_batch * seq) % ROW_BLOCK == 0
    n_prompt = n_batch * seq
    n_tok = n_prompt + n_dec
    n_rows = n_prompt + ROW_BLOCK
    assert n_dec <= ROW_BLOCK

    def peer(x, layer):
        return _peer_layer(x, norm_ffn_g[layer], peer_w_q[layer], peer_sub_k1[layer], peer_sub_k2[layer],
                           peer_u[layer], peer_v[layer])

    x_all = jnp.concatenate([x_prompt.reshape(n_prompt, D_MODEL), x_sample.reshape(n_dec, D_MODEL),
                             jnp.zeros((n_rows - n_tok, D_MODEL), F32)], axis=0)
    x_all, v_new = _gmlp_layer(x_all, n_prompt, norm_mix_g[0], gmlp_w_in[0], gmlp_v_g[0], gmlp_w_s[0],
                               gmlp_b_s[0], gmlp_w_out[0])
    x_all = peer(x_all, 0)
    pos = jnp.concatenate([jnp.tile(jnp.arange(seq), n_batch), jnp.full((n_rows - n_prompt,), PAST_LEN)])
    q, k, v, qi, ki, wi, new_k, new_v, new_ki = _att_project(
        x_all, pos, norm_mix_g[1], att_w_in[0], att_q_norm_g[0], att_k_norm_g[0], att_idx_k_norm_g[0])
    smp = slice(n_prompt, n_tok)
    heads = lambda a: a[:, smp].astype(F32).transpose(1, 0, 2)[:, None]
    o_s = _att_sample(heads(q) * (HEAD_DIM ** 0.5), new_k[smp].reshape(n_dec, 1, N_KV_HEADS, HEAD_DIM),
                      new_v[smp].reshape(n_dec, 1, N_KV_HEADS, HEAD_DIM), heads(qi),
                      new_ki[smp][:, None], wi[smp][:, None], cache_k[0], cache_v[0], cache_idx_k[0],
                      page_table, att_w_out[0])
    xs = x_all[smp] + o_s.reshape(n_dec, D_MODEL)
    x_all = _dsa_prompt(x_all, (q, k, v, qi, ki, wi), att_w_out[0], n_batch, seq)
    x_all = lax.dynamic_update_slice(x_all, xs, (n_prompt, 0))
    x_all = peer(x_all, 1)

    kv_shape = (1, n_batch, seq, N_KV_HEADS, HEAD_DIM)
    kv_s_shape = (1, n_dec, 1, N_KV_HEADS, HEAD_DIM)
    return (x_all[:n_prompt].reshape(n_batch, seq, D_MODEL), x_all[smp].reshape(n_dec, 1, D_MODEL),
            new_k[:n_prompt].reshape(kv_shape), new_v[:n_prompt].reshape(kv_shape),
            new_ki[:n_prompt].reshape(1, n_batch, seq, IDX_DIM),
            new_k[smp].reshape(kv_s_shape), new_v[smp].reshape(kv_s_shape),
            new_ki[smp].reshape(1, n_dec, 1, IDX_DIM), v_new[:n_dec].reshape(1, n_dec, 1, GMLP_WIDTH))
```

```python
import functools
import math

import jax
import jax.numpy as jnp
import numpy as np
from jax import lax
from jax.experimental import pallas as pl
from jax.experimental.pallas import tpu as pltpu

F32 = jnp.float32
BF16 = jnp.bfloat16
I32 = jnp.int32

D_MODEL = 1024
EPS = 1e-6

LANES = 128
SUBLANES = 8
MIB = 1024 * 1024
VMEM_LIMIT = 56 * MIB

ROW_BLOCK = 256

PEER_HEADS = 8
N_KEYS = 128
N_EXPERTS = N_KEYS * N_KEYS
PEER_TOPK = 16
PEER_KEY_DIM = 256
PEER_HALF = PEER_KEY_DIM // 2
PEER_PICKS = PEER_HEADS * PEER_TOPK
ROW_WORDS = D_MODEL // 2
ROW_SUB = ROW_WORDS // LANES
PEER_TOKENS = 128
N_CAND = sum(PEER_TOPK // (a + 1) for a in range(PEER_TOPK))

CHUNK = 128
GMLP_WIDTH = 2 * D_MODEL
GMLP_GROUPS = 8
GMLP_GROUP_DIM = GMLP_WIDTH // GMLP_GROUPS

PAST_LEN = 16384
PAGE_SIZE = 128
N_HEADS = 16
HEAD_DIM = D_MODEL // N_HEADS
N_KV_HEADS = 4
GROUP = N_HEADS // N_KV_HEADS
IDX_HEADS = 8
IDX_DIM = 64
TOPK_MAX = 256
ROPE_THETA = 10000.0
IDX_SCALE = (IDX_DIM ** -0.5) * (IDX_HEADS ** -0.5)
ATT_SIZES = (N_HEADS * HEAD_DIM, N_KV_HEADS * HEAD_DIM, N_KV_HEADS * HEAD_DIM,
             IDX_HEADS * IDX_DIM, IDX_DIM, IDX_HEADS)
ATT_COLS = sum(ATT_SIZES)
ATT_COLS_PAD = -(-ATT_COLS // LANES) * LANES
Q_COL, K_COL, V_COL, QI_COL, KI_COL = (int(c) for c in np.cumsum((0,) + ATT_SIZES[:4]))


def _nt_dot(a, b, **kw):
    return lax.dot_general(a, b, (((1,), (1,)), ((), ())), **kw)


def _rms_rows(x, g):
    return x * lax.rsqrt(jnp.mean(x * x, axis=-1, keepdims=True) + EPS) * g


def _full_spec(shape, **kw):
    return pl.BlockSpec(shape, lambda *_: (0,) * len(shape), **kw)


def _topk_rows(s, k, iota):
    n_rows = s.shape[0]
    vals, idxs = [], []
    for _ in range(k):
        m = jnp.max(s, axis=0, keepdims=True)
        first = jnp.min(jnp.where(s == m, iota, n_rows), axis=0, keepdims=True)
        vals.append(m)
        idxs.append(first)
        s = jnp.where(iota == first, -jnp.inf, s)
    return jnp.concatenate(vals, axis=0), jnp.concatenate(idxs, axis=0)


def _peer_router_kernel(x_ref, g_ref, wq_ref, k1_ref, k2_ref, h_ref, eidx_ref, gate_ref):
    h = _rms_rows(x_ref[...], g_ref[...])
    h_ref[...] = h
    q = jnp.dot(h.astype(BF16), wq_ref[...], preferred_element_type=F32)
    n_tok = h.shape[0]
    iota_k = lax.broadcasted_iota(I32, (N_KEYS, n_tok), 0)
    iota_c = lax.broadcasted_iota(I32, (N_CAND, n_tok), 0)
    k1 = k1_ref[...]
    k2 = k2_ref[...]
    eidx_rows, gate_rows = [], []
    for head in range(PEER_HEADS):
        base = head * PEER_KEY_DIM
        q1 = q[:, base:base + PEER_HALF].astype(BF16)
        q2 = q[:, base + PEER_HALF:base + PEER_KEY_DIM].astype(BF16)
        s1 = _nt_dot(k1, q1, preferred_element_type=F32)
        s2 = _nt_dot(k2, q2, preferred_element_type=F32)
        v1, i1 = _topk_rows(s1, PEER_TOPK, iota_k)
        v2, i2 = _topk_rows(s2, PEER_TOPK, iota_k)
        width = [PEER_TOPK // (a + 1) for a in range(PEER_TOPK)]
        cand = jnp.concatenate([v1[a:a + 1] + v2[:width[a]] for a in range(PEER_TOPK)], axis=0)
        cidx = jnp.concatenate([(i1[a:a + 1] * N_KEYS + i2[:width[a]]) * ROW_SUB for a in range(PEER_TOPK)], axis=0)
        top_s, top_e = [], []
        for _ in range(PEER_TOPK):
            m = jnp.max(cand, axis=0, keepdims=True)
            first = jnp.min(jnp.where(cand == m, iota_c, N_CAND), axis=0, keepdims=True)
            hit = iota_c == first
            top_s.append(m)
            top_e.append(jnp.sum(jnp.where(hit, cidx, 0), axis=0, keepdims=True))
            cand = jnp.where(hit, -jnp.inf, cand)
        top_s = jnp.concatenate(top_s, axis=0)
        e = jnp.exp(top_s - top_s[0:1])
        gate_rows.append(e / jnp.sum(e, axis=0, keepdims=True))
        eidx_rows.append(jnp.concatenate(top_e, axis=0))
    gate_t = jnp.concatenate(gate_rows, axis=0)
    eidx_t = jnp.concatenate(eidx_rows, axis=0)
    gate_ref[...] = gate_t.T
    eidx_ref[...] = lax.bitcast_convert_type(lax.bitcast_convert_type(eidx_t, F32).T, I32)


def _peer_router(x, g, w_q, sub_k1, sub_k2):
    n = x.shape[0]
    tm = PEER_TOKENS
    return pl.pallas_call(
        _peer_router_kernel,
        grid=(n // tm,),
        in_specs=[
            pl.BlockSpec((tm, D_MODEL), lambda i: (i, 0)),
            _full_spec((1, D_MODEL)),
            _full_spec((D_MODEL, PEER_HEADS * PEER_KEY_DIM)),
            _full_spec((N_KEYS, PEER_HALF)),
            _full_spec((N_KEYS, PEER_HALF)),
        ],
        out_specs=[
            pl.BlockSpec((tm, D_MODEL), lambda i: (i, 0)),
            pl.BlockSpec((tm, PEER_PICKS), lambda i: (i, 0)),
            pl.BlockSpec((tm, PEER_PICKS), lambda i: (i, 0)),
        ],
        out_shape=[
            jax.ShapeDtypeStruct((n, D_MODEL), F32),
            jax.ShapeDtypeStruct((n, PEER_PICKS), I32),
            jax.ShapeDtypeStruct((n, PEER_PICKS), F32),
        ],
        compiler_params=pltpu.CompilerParams(dimension_semantics=("arbitrary",), vmem_limit_bytes=VMEM_LIMIT),
        name="peer_router",
    )(x, g.reshape(1, D_MODEL), w_q.astype(BF16), sub_k1.astype(BF16), sub_k2.astype(BF16))


PACK_ROWS = 512


def _pack_kernel(w_ref, out_ref):
    w = w_ref[...]
    as_bits = lambda a: lax.bitcast_convert_type(a.astype(jnp.bfloat16).astype(F32), I32)
    hi = as_bits(w[:, :ROW_WORDS]) & jnp.int32(-65536)
    lo = (as_bits(w[:, ROW_WORDS:]) >> 16) & jnp.int32(0xFFFF)
    out_ref[...] = hi | lo


def _pack_table(w):
    words = pl.pallas_call(
        _pack_kernel,
        grid=(N_EXPERTS // PACK_ROWS,),
        in_specs=[pl.BlockSpec((PACK_ROWS, D_MODEL), lambda i: (i, 0))],
        out_specs=pl.BlockSpec((PACK_ROWS, ROW_WORDS), lambda i: (i, 0)),
        out_shape=jax.ShapeDtypeStruct((N_EXPERTS, ROW_WORDS), I32),
        compiler_params=pltpu.CompilerParams(dimension_semantics=("arbitrary",)),
        name="peer_pack_table",
    )(w)
    return words.reshape(N_EXPERTS * ROW_SUB, LANES)


PLANE_STRIDE = PEER_PICKS + SUBLANES
GATHER_ROWS = ROW_SUB * PLANE_STRIDE
TOKEN_UNROLL = 2


def _gather_token(eidx_ref, tab_ref, buf_ref, t):
    for p in range(PEER_PICKS):
        row0 = pl.multiple_of(eidx_ref[t, p], ROW_SUB)
        buf_ref[pl.ds(p, ROW_SUB, stride=PLANE_STRIDE), :] = tab_ref[pl.ds(row0, ROW_SUB), :]


def _gathered_rows(buf_ref):
    hi, lo = [], []
    for s in range(ROW_SUB):
        w = buf_ref[s * PLANE_STRIDE:s * PLANE_STRIDE + PEER_PICKS, :]
        hi.append(lax.bitcast_convert_type(w & jnp.int32(-65536), F32).astype(BF16))
        lo.append(lax.bitcast_convert_type(w << 16, F32).astype(BF16))
    return jnp.concatenate(hi + lo, axis=1)


def _token_pipeline(eidx_ref, tab_ref, bufs, n_tok, compute):
    _gather_token(eidx_ref, tab_ref, bufs[0], 0)
    _gather_token(eidx_ref, tab_ref, bufs[1], 1)

    def step(i, carry):
        for j in range(TOKEN_UNROLL):
            t = TOKEN_UNROLL * i + j
            buf = bufs[j % 2]
            compute(t, buf)
            _gather_token(eidx_ref, tab_ref, buf, jnp.minimum(t + 2, n_tok - 1))
        return carry

    lax.fori_loop(0, n_tok // TOKEN_UNROLL, step, 0)


def _peer_act_kernel(eidx_ref, tab_ref, h_ref, gate_ref, coef_ref, *bufs):
    def compute(t, buf_ref):
        w = _gathered_rows(buf_ref)
        x = jnp.broadcast_to(h_ref[pl.ds(t, 1), :], (SUBLANES, D_MODEL)).astype(BF16)
        act = _nt_dot(x, w, preferred_element_type=F32)[0:1]
        coef_ref[pl.ds(t, 1), :] = gate_ref[pl.ds(t, 1), :] * jax.nn.gelu(act)

    _token_pipeline(eidx_ref, tab_ref, bufs, h_ref.shape[0], compute)


def _peer_out_kernel(eidx_ref, tab_ref, coef_ref, x_ref, out_ref, *bufs):
    def compute(t, buf_ref):
        w = _gathered_rows(buf_ref)
        c = jnp.broadcast_to(coef_ref[pl.ds(t, 1), :], (SUBLANES, PEER_PICKS)).astype(BF16)
        y = jnp.dot(c, w, preferred_element_type=F32)[0:1]
        out_ref[pl.ds(t, 1), :] = x_ref[pl.ds(t, 1), :] + y

    _token_pipeline(eidx_ref, tab_ref, bufs, x_ref.shape[0], compute)


def _peer_gather_call(body, name, eidx, tab, a, b, out_cols):
    n = eidx.shape[0]
    tb = PEER_TOKENS
    rows = lambda cols: pl.BlockSpec((tb, cols), lambda i: (i, 0))
    return pl.pallas_call(
        body,
        grid=(n // tb,),
        in_specs=[
            pl.BlockSpec((tb, PEER_PICKS), lambda i: (i, 0), memory_space=pltpu.SMEM),
            _full_spec((N_EXPERTS * ROW_SUB, LANES), pipeline_mode=pl.Buffered(1)),
            rows(a.shape[1]),
            rows(b.shape[1]),
        ],
        out_specs=rows(out_cols),
        out_shape=jax.ShapeDtypeStruct((n, out_cols), F32),
        scratch_shapes=[pltpu.VMEM((GATHER_ROWS, LANES), I32) for _ in range(2)],
        compiler_params=pltpu.CompilerParams(dimension_semantics=("arbitrary",), vmem_limit_bytes=VMEM_LIMIT),
        name=name,
    )(eidx, tab, a, b)


def _peer_layer(x, g, w_q, sub_k1, sub_k2, expert_u, expert_v):
    h, eidx, gate = _peer_router(x, g, w_q, sub_k1, sub_k2)
    coef = _peer_gather_call(_peer_act_kernel, "peer_act", eidx, _pack_table(expert_u), h, gate, PEER_PICKS)
    return _peer_gather_call(_peer_out_kernel, "peer_out", eidx, _pack_table(expert_v), coef, x, D_MODEL)


def _gmlp_kernel(x_ref, g_ref, win_ref, vg_ref, ws_ref, bst_ref, s0_ref, b0_ref, wout_ref,
                 out_ref, vnew_ref, gated_ref, *, n_chunk_blocks):
    x = x_ref[...]
    h = _rms_rows(x, g_ref[...])
    uv = jnp.dot(h.astype(BF16), win_ref[...], preferred_element_type=F32)
    v = _rms_rows(uv[:, GMLP_WIDTH:], vg_ref[...])
    is_chunked = pl.program_id(0) < n_chunk_blocks

    @pl.when(is_chunked)
    def _():
        r_i = lax.broadcasted_iota(I32, (CHUNK, CHUNK), 0)
        c_i = lax.broadcasted_iota(I32, (CHUNK, CHUNK), 1)
        causal = c_i <= r_i
        for g in range(GMLP_GROUPS):
            cols = slice(g * GMLP_GROUP_DIM, (g + 1) * GMLP_GROUP_DIM)
            ws = jnp.where(causal, ws_ref[g], 0.0).astype(BF16)
            bias = bst_ref[:, g:g + 1]
            for c in range(x.shape[0] // CHUNK):
                rows = slice(c * CHUNK, (c + 1) * CHUNK)
                mixed = jnp.dot(ws, v[rows, cols].astype(BF16), preferred_element_type=F32) + bias
                gated_ref[rows, cols] = (uv[rows, cols] * mixed).astype(BF16)

    @pl.when(jnp.logical_not(is_chunked))
    def _():
        gated_ref[...] = (uv[:, :GMLP_WIDTH] * (v * s0_ref[...] + b0_ref[...])).astype(BF16)
        vnew_ref[...] = v

    out_ref[...] = x + jnp.dot(gated_ref[...], wout_ref[...], preferred_element_type=F32)


def _gmlp_layer(x_all, n_chunk_rows, g, w_in, v_g, w_s, b_s, w_out):
    n = x_all.shape[0]
    tb = ROW_BLOCK
    spread = lambda a: jnp.repeat(a, GMLP_GROUP_DIM).reshape(1, GMLP_WIDTH)
    body = functools.partial(_gmlp_kernel, n_chunk_blocks=n_chunk_rows // tb)
    return pl.pallas_call(
        body,
        grid=(n // tb,),
        in_specs=[
            pl.BlockSpec((tb, D_MODEL), lambda i: (i, 0)),
            _full_spec((1, D_MODEL)),
            _full_spec((D_MODEL, 2 * GMLP_WIDTH), pipeline_mode=pl.Buffered(1)),
            _full_spec((1, GMLP_WIDTH)),
            _full_spec((GMLP_GROUPS, CHUNK, CHUNK)),
            _full_spec((CHUNK, GMLP_GROUPS)),
            _full_spec((1, GMLP_WIDTH)),
            _full_spec((1, GMLP_WIDTH)),
            _full_spec((GMLP_WIDTH, D_MODEL), pipeline_mode=pl.Buffered(1)),
        ],
        out_specs=[
            pl.BlockSpec((tb, D_MODEL), lambda i: (i, 0)),
            _full_spec((tb, GMLP_WIDTH)),
        ],
        out_shape=[
            jax.ShapeDtypeStruct((n, D_MODEL), F32),
            jax.ShapeDtypeStruct((tb, GMLP_WIDTH), F32),
        ],
        scratch_shapes=[pltpu.VMEM((tb, GMLP_WIDTH), BF16)],
        compiler_params=pltpu.CompilerParams(dimension_semantics=("arbitrary",), vmem_limit_bytes=VMEM_LIMIT),
        name="gmlp_mixer",
    )(x_all, g.reshape(1, D_MODEL), w_in.astype(BF16), v_g.reshape(1, GMLP_WIDTH), w_s, b_s.T,
      spread(w_s[:, 0, 0]), spread(b_s[:, 0]), w_out.astype(BF16))


def _att_proj_kernel(x_ref, g_ref, w_ref, qg_ref, kg_ref, ikg_ref, cos_ref, sin_ref,
                     q_ref, k_ref, v_ref, qi_ref, ki_ref, wi_ref, nk_ref, nv_ref, nki_ref):
    h = _rms_rows(x_ref[...], g_ref[...])
    p = jnp.dot(h.astype(BF16), w_ref[...], preferred_element_type=F32)
    n_tok = p.shape[0]
    cos = cos_ref[...]
    sin = sin_ref[...]
    lane = lax.broadcasted_iota(I32, (n_tok, LANES), 1)
    first_half = (lane & (HEAD_DIM // 2)) == 0
    r_i = lax.broadcasted_iota(I32, (LANES, LANES), 0)
    c_i = lax.broadcasted_iota(I32, (LANES, LANES), 1)
    head_mean = jnp.where((r_i // HEAD_DIM) == (c_i // HEAD_DIM), 1.0 / HEAD_DIM, 0.0).astype(BF16)

    def slab(col):
        return p[:, col:col + LANES]

    def head_rms(a, gain):
        ms = jnp.dot((a * a).astype(BF16), head_mean, preferred_element_type=F32)
        return a * lax.rsqrt(ms + EPS) * gain

    def rope(a):
        partner = jnp.where(first_half, pltpu.roll(a, LANES - HEAD_DIM // 2, 1), pltpu.roll(a, HEAD_DIM // 2, 1))
        return a * cos + partner * sin

    def put_heads(ref, i, a):
        ref[2 * i] = a[:, :HEAD_DIM].astype(BF16)
        ref[2 * i + 1] = a[:, HEAD_DIM:].astype(BF16)

    for i in range(N_HEADS // 2):
        put_heads(q_ref, i, rope(head_rms(slab(Q_COL + i * LANES), qg_ref[...])) * (HEAD_DIM ** -0.5))
    for i in range(N_KV_HEADS // 2):
        k = rope(head_rms(slab(K_COL + i * LANES), kg_ref[...]))
        nk_ref[:, i * LANES:(i + 1) * LANES] = k
        put_heads(k_ref, i, k)
        v = slab(V_COL + i * LANES)
        nv_ref[:, i * LANES:(i + 1) * LANES] = v
        put_heads(v_ref, i, v)
    for i in range(IDX_HEADS // 2):
        put_heads(qi_ref, i, rope(slab(QI_COL + i * LANES)))
    tail = slab(KI_COL)
    ki = rope(head_rms(tail, ikg_ref[...]))[:, :IDX_DIM]
    nki_ref[...] = ki
    ki_ref[...] = ki.astype(BF16)
    wi_ref[...] = tail[:, IDX_DIM:IDX_DIM + IDX_HEADS]


def _att_project(x_all, pos, g, w_in, qn_g, kn_g, ikn_g):
    n = x_all.shape[0]
    tb = ROW_BLOCK
    half = HEAD_DIM // 2
    inv = jnp.exp(-math.log(ROPE_THETA) * jnp.arange(half, dtype=F32) / half)
    ang = pos.astype(F32)[:, None] * inv[None, :]
    cos = jnp.tile(jnp.cos(ang), (1, LANES // half))
    sin = jnp.tile(jnp.concatenate([-jnp.sin(ang), jnp.sin(ang)], axis=1), (1, LANES // HEAD_DIM))
    pair = lambda gain: jnp.tile(gain, LANES // HEAD_DIM).reshape(1, LANES)
    w_pad = jnp.pad(w_in, ((0, 0), (0, ATT_COLS_PAD - ATT_COLS))).astype(BF16)
    heads = lambda n_heads: (pl.BlockSpec((n_heads, tb, HEAD_DIM), lambda i: (0, i, 0)),
                             jax.ShapeDtypeStruct((n_heads, n, HEAD_DIM), BF16))
    rows = lambda cols, dt: (pl.BlockSpec((tb, cols), lambda i: (i, 0)), jax.ShapeDtypeStruct((n, cols), dt))
    outs = [heads(N_HEADS), heads(N_KV_HEADS), heads(N_KV_HEADS), heads(IDX_HEADS), rows(IDX_DIM, BF16),
            rows(IDX_HEADS, F32), rows(N_KV_HEADS * HEAD_DIM, F32), rows(N_KV_HEADS * HEAD_DIM, F32),
            rows(IDX_DIM, F32)]
    return pl.pallas_call(
        _att_proj_kernel,
        grid=(n // tb,),
        in_specs=[
            pl.BlockSpec((tb, D_MODEL), lambda i: (i, 0)),
            _full_spec((1, D_MODEL)),
            _full_spec((D_MODEL, ATT_COLS_PAD), pipeline_mode=pl.Buffered(1)),
            _full_spec((1, LANES)),
            _full_spec((1, LANES)),
            _full_spec((1, LANES)),
            pl.BlockSpec((tb, LANES), lambda i: (i, 0)),
            pl.BlockSpec((tb, LANES), lambda i: (i, 0)),
        ],
        out_specs=[o[0] for o in outs],
        out_shape=[o[1] for o in outs],
        compiler_params=pltpu.CompilerParams(dimension_semantics=("arbitrary",), vmem_limit_bytes=VMEM_LIMIT),
        name="att_project",
    )(x_all, g.reshape(1, D_MODEL), w_pad, pair(qn_g), pair(kn_g), pair(ikn_g), cos, sin)


DSA_TQ = 128
DSA_CLASS = 512
INT32_MIN = -(2 ** 31)


def _sortable_key(score):
    bits = lax.bitcast_convert_type(score, I32)
    return bits ^ ((bits >> 31) & jnp.int32(0x7FFFFFFF))


def _dsa_prompt_kernel(qi_ref, wi_ref, ki_ref, q_ref, k_ref, v_ref, x_ref, wo_ref, out_ref,
                       key_ref, bias_ref, *, s_eff, q_block0, n_keep):
    tq = DSA_TQ
    q0 = (q_block0 + pl.program_id(1)) * tq
    ki = ki_ref[...]
    wi = wi_ref[...]
    score = jnp.zeros((tq, s_eff), F32)
    for h in range(IDX_HEADS):
        s = _nt_dot(qi_ref[h], ki, preferred_element_type=F32)
        score = score + wi[:, h:h + 1] * jnp.maximum(s, 0.0)
    score = score * IDX_SCALE
    q_pos = q0 + lax.broadcasted_iota(I32, (tq, 1), 0)
    k_pos = lax.broadcasted_iota(I32, (tq, s_eff), 1)
    score = jnp.where(k_pos <= q_pos, score, -jnp.inf)
    key_ref[...] = _sortable_key(score)

    def search(it, thr):
        cand = thr + jnp.left_shift(jnp.int32(1), 31 - it)
        cnt = jnp.sum(jnp.where(key_ref[...] >= cand, 1, 0), axis=1, keepdims=True)
        return jnp.where(cnt >= n_keep, cand, thr)

    thr = lax.fori_loop(0, 32, search, jnp.full((tq, 1), INT32_MIN, I32))
    n_gt = jnp.sum(jnp.where(key_ref[...] > thr, 1, 0), axis=1, keepdims=True)
    need = (n_keep - n_gt).astype(F32)
    r_i = lax.broadcasted_iota(I32, (LANES, LANES), 0)
    c_i = lax.broadcasted_iota(I32, (LANES, LANES), 1)
    before = jnp.where(r_i < c_i, 1.0, 0.0).astype(BF16)
    run = jnp.zeros((tq, 1), F32)
    for c in range(s_eff // LANES):
        sl = slice(c * LANES, (c + 1) * LANES)
        key_c = key_ref[:, sl]
        tie = jnp.where(key_c == thr, 1.0, 0.0)
        rank = run + jnp.dot(tie.astype(BF16), before, preferred_element_type=F32)
        keep_tie = jnp.where(rank < need, tie, 0.0)
        keep = jnp.where(key_c > thr, 1.0, keep_tie)
        adm = (c * LANES + lax.broadcasted_iota(I32, (tq, LANES), 1)) <= q_pos
        bias_ref[:, sl] = jnp.where(adm, jnp.where(keep > 0.0, 0.0, -jnp.inf), -jnp.inf)
        run = run + jnp.sum(tie, axis=1, keepdims=True)

    outs = []
    for hk in range(N_KV_HEADS):
        k_h = k_ref[hk]
        v_h = v_ref[hk]
        for g in range(GROUP):
            logits = _nt_dot(q_ref[hk * GROUP + g], k_h, preferred_element_type=F32) + bias_ref[...]
            m = jnp.max(logits, axis=1, keepdims=True)
            p = jnp.exp(logits - m)
            den = jnp.sum(p, axis=1, keepdims=True)
            o = jnp.dot(p.astype(BF16), v_h, preferred_element_type=F32)
            outs.append(o / den)
    o_all = jnp.concatenate(outs, axis=1)
    out_ref[...] = x_ref[...] + jnp.dot(o_all.astype(BF16), wo_ref[...], preferred_element_type=F32)


def _dsa_prompt_call(x_all, qi, wi, ki, q, k, v, w_out, *, n_batch, seq, s_eff, q_block0, n_qblocks, n_keep):
    tq = DSA_TQ
    per_seq = seq // tq
    q_rows = lambda b, j: b * per_seq + q_block0 + j
    keys = lambda n_heads: pl.BlockSpec((pl.Element(n_heads), pl.Element(s_eff), pl.Element(HEAD_DIM)),
                                        lambda b, j: (0, b * seq, 0))
    body = functools.partial(_dsa_prompt_kernel, s_eff=s_eff, q_block0=q_block0, n_keep=n_keep)
    return pl.pallas_call(
        body,
        grid=(n_batch, n_qblocks),
        in_specs=[
            pl.BlockSpec((IDX_HEADS, tq, IDX_DIM), lambda b, j: (0, q_rows(b, j), 0)),
            pl.BlockSpec((tq, IDX_HEADS), lambda b, j: (q_rows(b, j), 0)),
            pl.BlockSpec((pl.Element(s_eff), pl.Element(IDX_DIM)), lambda b, j: (b * seq, 0)),
            pl.BlockSpec((N_HEADS, tq, HEAD_DIM), lambda b, j: (0, q_rows(b, j), 0)),
            keys(N_KV_HEADS),
            keys(N_KV_HEADS),
            pl.BlockSpec((tq, D_MODEL), lambda b, j: (q_rows(b, j), 0)),
            _full_spec((D_MODEL, D_MODEL)),
        ],
        out_specs=pl.BlockSpec((tq, D_MODEL), lambda b, j: (q_rows(b, j), 0)),
        out_shape=jax.ShapeDtypeStruct(x_all.shape, F32),
        input_output_aliases={6: 0},
        scratch_shapes=[pltpu.VMEM((tq, s_eff), I32), pltpu.VMEM((tq, s_eff), F32)],
        compiler_params=pltpu.CompilerParams(
            dimension_semantics=("arbitrary", "arbitrary"), vmem_limit_bytes=VMEM_LIMIT),
        name=f"dsa_prompt_{s_eff}",
    )(qi, wi, ki, q, k, v, x_all, w_out)


def _dsa_prompt(x_all, proj, w_out, n_batch, seq, class_rows=DSA_CLASS):
    q, k, v, qi, ki, wi = proj
    n_keep = min(TOPK_MAX, seq // 4)
    wo_b = w_out.astype(BF16)
    per_call = class_rows // DSA_TQ
    for c in range(seq // class_rows):
        x_all = _dsa_prompt_call(x_all, qi, wi, ki, q, k, v, wo_b, n_batch=n_batch, seq=seq,
                                 s_eff=(c + 1) * class_rows, q_block0=c * per_call, n_qblocks=per_call,
                                 n_keep=n_keep)
    return x_all


PAGE_GROUP = 8


def _dsa_sample_kernel(pt_ref, *refs, n_keep, n_pages):
    del pt_ref
    pg = PAGE_GROUP
    ki_pages, k_pages, v_pages = refs[:pg], refs[pg:2 * pg], refs[2 * pg:3 * pg]
    (qi_ref, wi_ref, kin_ref, q_ref, kn_ref, vn_ref, x_ref, wo_ref, out_ref,
     score_ref, bias_ref, bnew_ref, m_ref, l_ref, acc_ref) = refs[3 * pg:]
    phase = pl.program_id(1)
    j = pl.program_id(2)
    last = pl.num_programs(2) - 1
    total = lambda a: jnp.sum(jnp.sum(a, axis=1, keepdims=True), axis=0, keepdims=True)
    head_group = lax.broadcasted_iota(I32, (N_HEADS, 1), 0) // GROUP

    def weigh(s):
        return jnp.sum(wi_ref[...] * jnp.maximum(s, 0.0), axis=0, keepdims=True) * IDX_SCALE

    def select():
        kin = kin_ref[...].astype(BF16).astype(F32)
        s_new = weigh(jnp.sum(qi_ref[...].astype(F32) * kin, axis=1, keepdims=True))
        key = _sortable_key(score_ref[...])
        key_new = _sortable_key(s_new)

        def search(it, thr):
            cand = thr + jnp.left_shift(jnp.int32(1), 31 - it)
            cnt = total(jnp.where(key >= cand, 1, 0)) + jnp.where(key_new >= cand, 1, 0)
            return jnp.where(cnt >= n_keep, cand, thr)

        thr = lax.fori_loop(0, 32, search, jnp.full((1, 1), INT32_MIN, I32))
        n_gt = total(jnp.where(key > thr, 1, 0)) + jnp.where(key_new > thr, 1, 0)
        need = (n_keep - n_gt).astype(F32)
        tie = jnp.where(key == thr, 1.0, 0.0)
        r_i = lax.broadcasted_iota(I32, (LANES, LANES), 0)
        c_i = lax.broadcasted_iota(I32, (LANES, LANES), 1)
        before = jnp.where(r_i < c_i, 1.0, 0.0).astype(BF16)
        rr = lax.broadcasted_iota(I32, (n_pages, n_pages), 0)
        cc = lax.broadcasted_iota(I32, (n_pages, n_pages), 1)
        rows_above = jnp.where(cc < rr, 1.0, 0.0).astype(BF16)
        tie_b = tie.astype(BF16)
        rank = (jnp.dot(tie_b, before, preferred_element_type=F32)
                + jnp.sum(jnp.dot(rows_above, tie_b, preferred_element_type=F32), axis=1, keepdims=True))
        keep = jnp.where(key > thr, 1.0, jnp.where(rank < need, tie, 0.0))
        bias_ref[...] = jnp.where(keep > 0.0, 0.0, -jnp.inf)
        keep_new = jnp.where(key_new > thr, 1.0,
                             jnp.where(key_new == thr, jnp.where(total(tie) < need, 1.0, 0.0), 0.0))
        bnew_ref[...] = jnp.where(keep_new > 0.0, 0.0, -jnp.inf)

    @pl.when(phase == 0)
    def _():
        for i in range(pg):
            s = _nt_dot(qi_ref[...], ki_pages[i][...].astype(BF16), preferred_element_type=F32)
            score_ref[pl.ds(j * pg + i, 1), :] = weigh(s)
        pl.when(j == last)(select)

    def flash_update(logits, pv_of):
        m_old = m_ref[...]
        m_new = jnp.maximum(m_old, jnp.max(logits, axis=1, keepdims=True))
        m_safe = jnp.where(m_new == -jnp.inf, 0.0, m_new)
        alpha = jnp.exp(m_old - m_safe)
        p = jnp.exp(logits - m_safe)
        m_ref[...] = m_new
        l_ref[...] = alpha * l_ref[...] + jnp.sum(p, axis=1, keepdims=True)
        acc_ref[...] = alpha * acc_ref[...] + pv_of(p)

    def by_group(per_kv_head):
        out = per_kv_head(0)
        for hk in range(1, N_KV_HEADS):
            out = jnp.where(head_group == hk, per_kv_head(hk), out)
        return out

    @pl.when(phase == 1)
    def _():
        @pl.when(j == 0)
        def _():
            m_ref[...] = jnp.full(m_ref.shape, -jnp.inf, F32)
            l_ref[...] = jnp.zeros(l_ref.shape, F32)
            acc_ref[...] = jnp.zeros(acc_ref.shape, F32)

        q = q_ref[...]
        head = lambda a, hk: a[:, hk * HEAD_DIM:(hk + 1) * HEAD_DIM]
        k_pgs = [k_pages[i][...].astype(BF16) for i in range(pg)]
        v_pgs = [v_pages[i][...].astype(BF16) for i in range(pg)]
        logits = jnp.concatenate(
            [by_group(lambda hk: _nt_dot(q, head(k_pgs[i], hk), preferred_element_type=F32))
             + bias_ref[pl.ds(j * pg + i, 1), :] for i in range(pg)], axis=1)

        def pv_of(p):
            p = p.astype(BF16)
            page = lambda i: p[:, i * PAGE_SIZE:(i + 1) * PAGE_SIZE]
            return by_group(lambda hk: sum(
                jnp.dot(page(i), head(v_pgs[i], hk), preferred_element_type=F32) for i in range(pg)))

        flash_update(logits, pv_of)

        @pl.when(j == last)
        def _():
            rows = lambda ref: by_group(lambda hk: ref[hk:hk + 1, :]).astype(BF16).astype(F32)
            logit_new = jnp.sum(q.astype(F32) * rows(kn_ref), axis=1, keepdims=True) + bnew_ref[...]
            flash_update(logit_new, lambda p: p.astype(BF16).astype(F32) * rows(vn_ref))
            o = (acc_ref[...] / l_ref[...]).astype(BF16)
            y = x_ref[...]
            for h in range(N_HEADS):
                o_h = jnp.broadcast_to(o[h:h + 1, :], (2 * SUBLANES, HEAD_DIM))
                y = y + jnp.dot(o_h, wo_ref[h], preferred_element_type=F32)[0:1]
            out_ref[...] = y


def _dsa_sample(x_s, q_s, qi_s, wi_s, ki_new, k_new, v_new, cache_k, cache_v, cache_idx_k, page_table, w_out):
    n_dec = x_s.shape[0]
    n_pages = page_table.shape[1]
    n_pool = cache_k.shape[0]
    pg = PAGE_GROUP
    n_groups = n_pages // pg
    n_keep = min(TOPK_MAX, (n_pages * PAGE_SIZE + 1) // 4)
    kv_cols = N_KV_HEADS * HEAD_DIM

    def page_spec(cols, i, live_phase):
        def index(b, ph, j, pt):
            idle = i if live_phase == 1 else (n_groups - 1) * pg + i
            return (pt[b, jnp.where(ph == live_phase, j * pg + i, idle)], 0, 0)
        return pl.BlockSpec((None, PAGE_SIZE, cols), index)

    per_seq = lambda *shape: pl.BlockSpec((None,) + shape, lambda b, ph, j, pt: (b,) + (0,) * len(shape))
    grid_spec = pltpu.PrefetchScalarGridSpec(
        num_scalar_prefetch=1,
        grid=(n_dec, 2, n_groups),
        in_specs=([page_spec(IDX_DIM, i, 0) for i in range(pg)]
                  + [page_spec(kv_cols, i, 1) for i in range(pg)]
                  + [page_spec(kv_cols, i, 1) for i in range(pg)]
                  + [per_seq(IDX_HEADS, IDX_DIM), per_seq(IDX_HEADS, 1), per_seq(1, IDX_DIM),
                     per_seq(N_HEADS, HEAD_DIM), per_seq(N_KV_HEADS, HEAD_DIM), per_seq(N_KV_HEADS, HEAD_DIM),
                     per_seq(1, D_MODEL),
                     pl.BlockSpec((N_HEADS, HEAD_DIM, D_MODEL), lambda b, ph, j, pt: (0, 0, 0))]),
        out_specs=per_seq(1, D_MODEL),
        scratch_shapes=[pltpu.VMEM((n_pages, PAGE_SIZE), F32), pltpu.VMEM((n_pages, PAGE_SIZE), F32),
                        pltpu.VMEM((1, 1), F32), pltpu.VMEM((N_HEADS, 1), F32), pltpu.VMEM((N_HEADS, 1), F32),
                        pltpu.VMEM((N_HEADS, HEAD_DIM), F32)],
    )
    ck = cache_k.reshape(n_pool, PAGE_SIZE, kv_cols)
    cv = cache_v.reshape(n_pool, PAGE_SIZE, kv_cols)
    out = pl.pallas_call(
        functools.partial(_dsa_sample_kernel, n_keep=n_keep, n_pages=n_pages),
        grid_spec=grid_spec,
        out_shape=jax.ShapeDtypeStruct((n_dec, 1, D_MODEL), F32),
        compiler_params=pltpu.CompilerParams(
            dimension_semantics=("arbitrary", "arbitrary", "arbitrary"), vmem_limit_bytes=VMEM_LIMIT),
        name="dsa_sample",
    )(page_table, *([cache_idx_k] * pg), *([ck] * pg), *([cv] * pg),
      qi_s, wi_s.reshape(n_dec, IDX_HEADS, 1), ki_new.reshape(n_dec, 1, IDX_DIM), q_s,
      k_new.reshape(n_dec, N_KV_HEADS, HEAD_DIM), v_new.reshape(n_dec, N_KV_HEADS, HEAD_DIM),
      x_s.reshape(n_dec, 1, D_MODEL), w_out.astype(BF16).reshape(N_HEADS, HEAD_DIM, D_MODEL))
    return out.reshape(n_dec, D_MODEL)


def kernel(x_prompt, x_sample, cache_k, cache_v, cache_idx_k, page_table, norm_mix_g, norm_ffn_g,
           gmlp_w_in, gmlp_v_g, gmlp_w_s, gmlp_b_s, gmlp_w_out,
           att_w_in, att_q_norm_g, att_k_norm_g, att_idx_k_norm_g, att_w_out,
           peer_w_q, peer_sub_k1, peer_sub_k2, peer_u, peer_v):
    n_batch, seq, _ = x_prompt.shape
    n_dec, dec_seq, _ = x_sample.shape
    assert dec_seq == 1 and seq % DSA_CLASS == 0 and (n_batch * seq) % ROW_BLOCK == 0
    n_prompt = n_batch * seq
    n_tok = n_prompt + n_dec
    n_rows = n_prompt + ROW_BLOCK
    assert n_dec <= ROW_BLOCK

    def peer(x, layer):
        return _peer_layer(x, norm_ffn_g[layer], peer_w_q[layer], peer_sub_k1[layer], peer_sub_k2[layer],
                           peer_u[layer], peer_v[layer])

    x_all = jnp.concatenate([x_prompt.reshape(n_prompt, D_MODEL), x_sample.reshape(n_dec, D_MODEL),
                             jnp.zeros((n_rows - n_tok, D_MODEL), F32)], axis=0)
    x_all, v_new = _gmlp_layer(x_all, n_prompt, norm_mix_g[0], gmlp_w_in[0], gmlp_v_g[0], gmlp_w_s[0],
                               gmlp_b_s[0], gmlp_w_out[0])
    x_all = peer(x_all, 0)
    pos = jnp.concatenate([jnp.tile(jnp.arange(seq), n_batch), jnp.full((n_rows - n_prompt,), PAST_LEN)])
    q, k, v, qi, ki, wi, new_k, new_v, new_ki = _att_project(
        x_all, pos, norm_mix_g[1], att_w_in[0], att_q_norm_g[0], att_k_norm_g[0], att_idx_k_norm_g[0])
    smp = slice(n_prompt, n_tok)
    heads = lambda a: a[:, smp].transpose(1, 0, 2)
    xs = _dsa_sample(x_all[smp], heads(q), heads(qi), wi[smp], new_ki[smp], new_k[smp], new_v[smp],
                     cache_k[0], cache_v[0], cache_idx_k[0], page_table, att_w_out[0])
    x_all = _dsa_prompt(x_all, (q, k, v, qi, ki, wi), att_w_out[0], n_batch, seq)
    x_all = lax.dynamic_update_slice(x_all, xs, (n_prompt, 0))
    x_all = peer(x_all, 1)

    kv_shape = (1, n_batch, seq, N_KV_HEADS, HEAD_DIM)
    kv_s_shape = (1, n_dec, 1, N_KV_HEADS, HEAD_DIM)
    return (x_all[:n_prompt].reshape(n_batch, seq, D_MODEL), x_all[smp].reshape(n_dec, 1, D_MODEL),
            new_k[:n_prompt].reshape(kv_shape), new_v[:n_prompt].reshape(kv_shape),
            new_ki[:n_prompt].reshape(1, n_batch, seq, IDX_DIM),
            new_k[smp].reshape(kv_s_shape), new_v[smp].reshape(kv_s_shape),
            new_ki[smp].reshape(1, n_dec, 1, IDX_DIM), v_new[:n_dec].reshape(1, n_dec, 1, GMLP_WIDTH))
```

```python
import functools
import math

import jax
import jax.numpy as jnp
import numpy as np
from jax import lax
from jax.experimental import pallas as pl
from jax.experimental.pallas import tpu as pltpu

F32 = jnp.float32
BF16 = jnp.bfloat16
I32 = jnp.int32

D_MODEL = 1024
EPS = 1e-6

LANES = 128
SUBLANES = 8
MIB = 1024 * 1024
VMEM_LIMIT = 56 * MIB

ROW_BLOCK = 256

PEER_HEADS = 8
N_KEYS = 128
N_EXPERTS = N_KEYS * N_KEYS
PEER_TOPK = 16
PEER_KEY_DIM = 256
PEER_HALF = PEER_KEY_DIM // 2
PEER_PICKS = PEER_HEADS * PEER_TOPK
ROW_WORDS = D_MODEL // 2
ROW_SUB = ROW_WORDS // LANES
PEER_TOKENS = 128
N_CAND = sum(PEER_TOPK // (a + 1) for a in range(PEER_TOPK))

CHUNK = 128
GMLP_WIDTH = 2 * D_MODEL
GMLP_GROUPS = 8
GMLP_GROUP_DIM = GMLP_WIDTH // GMLP_GROUPS

PAST_LEN = 16384
PAGE_SIZE = 128
N_HEADS = 16
HEAD_DIM = D_MODEL // N_HEADS
N_KV_HEADS = 4
GROUP = N_HEADS // N_KV_HEADS
IDX_HEADS = 8
IDX_DIM = 64
TOPK_MAX = 256
ROPE_THETA = 10000.0
IDX_SCALE = (IDX_DIM ** -0.5) * (IDX_HEADS ** -0.5)
ATT_SIZES = (N_HEADS * HEAD_DIM, N_KV_HEADS * HEAD_DIM, N_KV_HEADS * HEAD_DIM,
             IDX_HEADS * IDX_DIM, IDX_DIM, IDX_HEADS)
ATT_COLS = sum(ATT_SIZES)
ATT_COLS_PAD = -(-ATT_COLS // LANES) * LANES
Q_COL, K_COL, V_COL, QI_COL, KI_COL = (int(c) for c in np.cumsum((0,) + ATT_SIZES[:4]))


def _nt_dot(a, b, **kw):
    return lax.dot_general(a, b, (((1,), (1,)), ((), ())), **kw)


def _rms_rows(x, g):
    return x * lax.rsqrt(jnp.mean(x * x, axis=-1, keepdims=True) + EPS) * g


def _full_spec(shape, **kw):
    return pl.BlockSpec(shape, lambda *_: (0,) * len(shape), **kw)


def _topk_rows(s, k, iota):
    n_rows = s.shape[0]
    vals, idxs = [], []
    for _ in range(k):
        m = jnp.max(s, axis=0, keepdims=True)
        first = jnp.min(jnp.where(s == m, iota, n_rows), axis=0, keepdims=True)
        vals.append(m)
        idxs.append(first)
        s = jnp.where(iota == first, -jnp.inf, s)
    return jnp.concatenate(vals, axis=0), jnp.concatenate(idxs, axis=0)


def _peer_router_kernel(x_ref, g_ref, wq_ref, k1_ref, k2_ref, h_ref, eidx_ref, gate_ref):
    h = _rms_rows(x_ref[...], g_ref[...])
    h_ref[...] = h
    q = jnp.dot(h.astype(BF16), wq_ref[...], preferred_element_type=F32)
    n_tok = h.shape[0]
    iota_k = lax.broadcasted_iota(I32, (N_KEYS, n_tok), 0)
    iota_c = lax.broadcasted_iota(I32, (N_CAND, n_tok), 0)
    k1 = k1_ref[...]
    k2 = k2_ref[...]
    eidx_rows, gate_rows = [], []
    for head in range(PEER_HEADS):
        base = head * PEER_KEY_DIM
        q1 = q[:, base:base + PEER_HALF].astype(BF16)
        q2 = q[:, base + PEER_HALF:base + PEER_KEY_DIM].astype(BF16)
        s1 = _nt_dot(k1, q1, preferred_element_type=F32)
        s2 = _nt_dot(k2, q2, preferred_element_type=F32)
        v1, i1 = _topk_rows(s1, PEER_TOPK, iota_k)
        v2, i2 = _topk_rows(s2, PEER_TOPK, iota_k)
        width = [PEER_TOPK // (a + 1) for a in range(PEER_TOPK)]
        cand = jnp.concatenate([v1[a:a + 1] + v2[:width[a]] for a in range(PEER_TOPK)], axis=0)
        cidx = jnp.concatenate([(i1[a:a + 1] * N_KEYS + i2[:width[a]]) * ROW_SUB for a in range(PEER_TOPK)], axis=0)
        top_s, top_e = [], []
        for _ in range(PEER_TOPK):
            m = jnp.max(cand, axis=0, keepdims=True)
            first = jnp.min(jnp.where(cand == m, iota_c, N_CAND), axis=0, keepdims=True)
            hit = iota_c == first
            top_s.append(m)
            top_e.append(jnp.sum(jnp.where(hit, cidx, 0), axis=0, keepdims=True))
            cand = jnp.where(hit, -jnp.inf, cand)
        top_s = jnp.concatenate(top_s, axis=0)
        e = jnp.exp(top_s - top_s[0:1])
        gate_rows.append(e / jnp.sum(e, axis=0, keepdims=True))
        eidx_rows.append(jnp.concatenate(top_e, axis=0))
    gate_t = jnp.concatenate(gate_rows, axis=0)
    eidx_t = jnp.concatenate(eidx_rows, axis=0)
    gate_ref[...] = gate_t.T
    eidx_ref[...] = lax.bitcast_convert_type(lax.bitcast_convert_type(eidx_t, F32).T, I32)


def _peer_router(x, g, w_q, sub_k1, sub_k2):
    n = x.shape[0]
    tm = PEER_TOKENS
    return pl.pallas_call(
        _peer_router_kernel,
        grid=(n // tm,),
        in_specs=[
            pl.BlockSpec((tm, D_MODEL), lambda i: (i, 0)),
            _full_spec((1, D_MODEL)),
            _full_spec((D_MODEL, PEER_HEADS * PEER_KEY_DIM)),
            _full_spec((N_KEYS, PEER_HALF)),
            _full_spec((N_KEYS, PEER_HALF)),
        ],
        out_specs=[
            pl.BlockSpec((tm, D_MODEL), lambda i: (i, 0)),
            pl.BlockSpec((tm, PEER_PICKS), lambda i: (i, 0)),
            pl.BlockSpec((tm, PEER_PICKS), lambda i: (i, 0)),
        ],
        out_shape=[
            jax.ShapeDtypeStruct((n, D_MODEL), F32),
            jax.ShapeDtypeStruct((n, PEER_PICKS), I32),
            jax.ShapeDtypeStruct((n, PEER_PICKS), F32),
        ],
        compiler_params=pltpu.CompilerParams(dimension_semantics=("arbitrary",), vmem_limit_bytes=VMEM_LIMIT),
        name="peer_router",
    )(x, g.reshape(1, D_MODEL), w_q.astype(BF16), sub_k1.astype(BF16), sub_k2.astype(BF16))


PACK_ROWS = 512


def _pack_kernel(w_ref, out_ref):
    w = w_ref[...]
    as_bits = lambda a: lax.bitcast_convert_type(a.astype(jnp.bfloat16).astype(F32), I32)
    hi = as_bits(w[:, :ROW_WORDS]) & jnp.int32(-65536)
    lo = (as_bits(w[:, ROW_WORDS:]) >> 16) & jnp.int32(0xFFFF)
    out_ref[...] = hi | lo


def _pack_table(w):
    words = pl.pallas_call(
        _pack_kernel,
        grid=(N_EXPERTS // PACK_ROWS,),
        in_specs=[pl.BlockSpec((PACK_ROWS, D_MODEL), lambda i: (i, 0))],
        out_specs=pl.BlockSpec((PACK_ROWS, ROW_WORDS), lambda i: (i, 0)),
        out_shape=jax.ShapeDtypeStruct((N_EXPERTS, ROW_WORDS), I32),
        compiler_params=pltpu.CompilerParams(dimension_semantics=("arbitrary",)),
        name="peer_pack_table",
    )(w)
    return words.reshape(N_EXPERTS * ROW_SUB, LANES)


PLANE_STRIDE = PEER_PICKS + SUBLANES
GATHER_ROWS = ROW_SUB * PLANE_STRIDE
TOKEN_GROUP = 2
N_GATHER_BUFS = 2 * TOKEN_GROUP


def _gather_token(eidx_ref, tab_ref, buf_ref, t):
    for p in range(PEER_PICKS):
        row0 = pl.multiple_of(eidx_ref[t, p], ROW_SUB)
        buf_ref[pl.ds(p, ROW_SUB, stride=PLANE_STRIDE), :] = tab_ref[pl.ds(row0, ROW_SUB), :]


def _gathered_rows(buf_ref):
    hi, lo = [], []
    for s in range(ROW_SUB):
        w = buf_ref[s * PLANE_STRIDE:s * PLANE_STRIDE + PEER_PICKS, :]
        hi.append(lax.bitcast_convert_type(w & jnp.int32(-65536), F32).astype(BF16))
        lo.append(lax.bitcast_convert_type(w << 16, F32).astype(BF16))
    return jnp.concatenate(hi + lo, axis=1)


def _token_pipeline(eidx_ref, tab_ref, bufs, n_tok, compute):
    g = TOKEN_GROUP
    groups = (bufs[:g], bufs[g:])
    for k, buf in enumerate(bufs):
        _gather_token(eidx_ref, tab_ref, buf, k)

    def step(i, carry):
        for j, group in enumerate(groups):
            t = 2 * g * i + g * j
            compute(t, group)
            for k, buf in enumerate(group):
                _gather_token(eidx_ref, tab_ref, buf, jnp.minimum(t + 2 * g + k, n_tok - 1))
        return carry

    lax.fori_loop(0, n_tok // (2 * g), step, 0)


def _group_rows(group):
    return jnp.concatenate([_gathered_rows(buf) for buf in group], axis=0)


def _token_rows(ref, t):
    return [ref[pl.ds(t + k, 1), :] for k in range(TOKEN_GROUP)]


def _own_block(n_rows):
    r = lax.broadcasted_iota(I32, (n_rows, TOKEN_GROUP * PEER_PICKS), 0)
    c = lax.broadcasted_iota(I32, (n_rows, TOKEN_GROUP * PEER_PICKS), 1)
    return r == c // PEER_PICKS


def _peer_act_kernel(eidx_ref, tab_ref, h_ref, gate_ref, coef_ref, *bufs):
    g = TOKEN_GROUP
    pad = jnp.zeros((2 * SUBLANES - g, D_MODEL), F32)

    def compute(t, group):
        x = jnp.concatenate(_token_rows(h_ref, t) + [pad], axis=0).astype(BF16)
        dots = _nt_dot(x, _group_rows(group), preferred_element_type=F32)
        act = jnp.concatenate([dots[k:k + 1, k * PEER_PICKS:(k + 1) * PEER_PICKS] for k in range(g)], axis=0)
        coef = jnp.concatenate(_token_rows(gate_ref, t), axis=0) * jax.nn.gelu(act)
        for k in range(g):
            coef_ref[pl.ds(t + k, 1), :] = coef[k:k + 1]

    _token_pipeline(eidx_ref, tab_ref, bufs, h_ref.shape[0], compute)


def _peer_out_kernel(eidx_ref, tab_ref, coef_ref, x_ref, out_ref, *bufs):
    g = TOKEN_GROUP
    own = _own_block(g)
    pad = jnp.zeros((2 * SUBLANES - g, g * PEER_PICKS), F32)

    def compute(t, group):
        c = jnp.concatenate(_token_rows(coef_ref, t), axis=0)
        lhs = jnp.concatenate([jnp.where(own, jnp.concatenate([c] * g, axis=1), 0.0), pad], axis=0)
        y = jnp.dot(lhs.astype(BF16), _group_rows(group), preferred_element_type=F32)
        for k in range(g):
            out_ref[pl.ds(t + k, 1), :] = x_ref[pl.ds(t + k, 1), :] + y[k:k + 1]

    _token_pipeline(eidx_ref, tab_ref, bufs, x_ref.shape[0], compute)


def _peer_gather_call(body, name, eidx, tab, a, b, out_cols):
    n = eidx.shape[0]
    tb = PEER_TOKENS
    rows = lambda cols: pl.BlockSpec((tb, cols), lambda i: (i, 0))
    return pl.pallas_call(
        body,
        grid=(n // tb,),
        in_specs=[
            pl.BlockSpec((tb, PEER_PICKS), lambda i: (i, 0), memory_space=pltpu.SMEM),
            _full_spec((N_EXPERTS * ROW_SUB, LANES), pipeline_mode=pl.Buffered(1)),
            rows(a.shape[1]),
            rows(b.shape[1]),
        ],
        out_specs=rows(out_cols),
        out_shape=jax.ShapeDtypeStruct((n, out_cols), F32),
        scratch_shapes=[pltpu.VMEM((GATHER_ROWS, LANES), I32) for _ in range(N_GATHER_BUFS)],
        compiler_params=pltpu.CompilerParams(dimension_semantics=("arbitrary",), vmem_limit_bytes=VMEM_LIMIT),
        name=name,
    )(eidx, tab, a, b)


def _peer_layer(x, g, w_q, sub_k1, sub_k2, expert_u, expert_v):
    h, eidx, gate = _peer_router(x, g, w_q, sub_k1, sub_k2)
    coef = _peer_gather_call(_peer_act_kernel, "peer_act", eidx, _pack_table(expert_u), h, gate, PEER_PICKS)
    return _peer_gather_call(_peer_out_kernel, "peer_out", eidx, _pack_table(expert_v), coef, x, D_MODEL)


def _gmlp_kernel(x_ref, g_ref, win_ref, vg_ref, ws_ref, bst_ref, s0_ref, b0_ref, wout_ref,
                 out_ref, vnew_ref, gated_ref, *, n_chunk_blocks):
    x = x_ref[...]
    h = _rms_rows(x, g_ref[...])
    uv = jnp.dot(h.astype(BF16), win_ref[...], preferred_element_type=F32)
    v = _rms_rows(uv[:, GMLP_WIDTH:], vg_ref[...])
    is_chunked = pl.program_id(0) < n_chunk_blocks

    @pl.when(is_chunked)
    def _():
        r_i = lax.broadcasted_iota(I32, (CHUNK, CHUNK), 0)
        c_i = lax.broadcasted_iota(I32, (CHUNK, CHUNK), 1)
        causal = c_i <= r_i
        for g in range(GMLP_GROUPS):
            cols = slice(g * GMLP_GROUP_DIM, (g + 1) * GMLP_GROUP_DIM)
            ws = jnp.where(causal, ws_ref[g], 0.0).astype(BF16)
            bias = bst_ref[:, g:g + 1]
            for c in range(x.shape[0] // CHUNK):
                rows = slice(c * CHUNK, (c + 1) * CHUNK)
                mixed = jnp.dot(ws, v[rows, cols].astype(BF16), preferred_element_type=F32) + bias
                gated_ref[rows, cols] = (uv[rows, cols] * mixed).astype(BF16)

    @pl.when(jnp.logical_not(is_chunked))
    def _():
        gated_ref[...] = (uv[:, :GMLP_WIDTH] * (v * s0_ref[...] + b0_ref[...])).astype(BF16)
        vnew_ref[...] = v

    out_ref[...] = x + jnp.dot(gated_ref[...], wout_ref[...], preferred_element_type=F32)


def _gmlp_layer(x_all, n_chunk_rows, g, w_in, v_g, w_s, b_s, w_out):
    n = x_all.shape[0]
    tb = ROW_BLOCK
    spread = lambda a: jnp.repeat(a, GMLP_GROUP_DIM).reshape(1, GMLP_WIDTH)
    body = functools.partial(_gmlp_kernel, n_chunk_blocks=n_chunk_rows // tb)
    return pl.pallas_call(
        body,
        grid=(n // tb,),
        in_specs=[
            pl.BlockSpec((tb, D_MODEL), lambda i: (i, 0)),
            _full_spec((1, D_MODEL)),
            _full_spec((D_MODEL, 2 * GMLP_WIDTH), pipeline_mode=pl.Buffered(1)),
            _full_spec((1, GMLP_WIDTH)),
            _full_spec((GMLP_GROUPS, CHUNK, CHUNK)),
            _full_spec((CHUNK, GMLP_GROUPS)),
            _full_spec((1, GMLP_WIDTH)),
            _full_spec((1, GMLP_WIDTH)),
            _full_spec((GMLP_WIDTH, D_MODEL), pipeline_mode=pl.Buffered(1)),
        ],
        out_specs=[
            pl.BlockSpec((tb, D_MODEL), lambda i: (i, 0)),
            _full_spec((tb, GMLP_WIDTH)),
        ],
        out_shape=[
            jax.ShapeDtypeStruct((n, D_MODEL), F32),
            jax.ShapeDtypeStruct((tb, GMLP_WIDTH), F32),
        ],
        scratch_shapes=[pltpu.VMEM((tb, GMLP_WIDTH), BF16)],
        compiler_params=pltpu.CompilerParams(dimension_semantics=("arbitrary",), vmem_limit_bytes=VMEM_LIMIT),
        name="gmlp_mixer",
    )(x_all, g.reshape(1, D_MODEL), w_in.astype(BF16), v_g.reshape(1, GMLP_WIDTH), w_s, b_s.T,
      spread(w_s[:, 0, 0]), spread(b_s[:, 0]), w_out.astype(BF16))


def _att_proj_kernel(x_ref, g_ref, w_ref, qg_ref, kg_ref, ikg_ref, cos_ref, sin_ref,
                     q_ref, k_ref, v_ref, qi_ref, ki_ref, wi_ref, nk_ref, nv_ref, nki_ref):
    h = _rms_rows(x_ref[...], g_ref[...])
    p = jnp.dot(h.astype(BF16), w_ref[...], preferred_element_type=F32)
    n_tok = p.shape[0]
    cos = cos_ref[...]
    sin = sin_ref[...]
    lane = lax.broadcasted_iota(I32, (n_tok, LANES), 1)
    first_half = (lane & (HEAD_DIM // 2)) == 0
    r_i = lax.broadcasted_iota(I32, (LANES, LANES), 0)
    c_i = lax.broadcasted_iota(I32, (LANES, LANES), 1)
    head_mean = jnp.where((r_i // HEAD_DIM) == (c_i // HEAD_DIM), 1.0 / HEAD_DIM, 0.0).astype(BF16)

    def slab(col):
        return p[:, col:col + LANES]

    def head_rms(a, gain):
        ms = jnp.dot((a * a).astype(BF16), head_mean, preferred_element_type=F32)
        return a * lax.rsqrt(ms + EPS) * gain

    def rope(a):
        partner = jnp.where(first_half, pltpu.roll(a, LANES - HEAD_DIM // 2, 1), pltpu.roll(a, HEAD_DIM // 2, 1))
        return a * cos + partner * sin

    def put_heads(ref, i, a):
        ref[2 * i] = a[:, :HEAD_DIM].astype(BF16)
        ref[2 * i + 1] = a[:, HEAD_DIM:].astype(BF16)

    for i in range(N_HEADS // 2):
        put_heads(q_ref, i, rope(head_rms(slab(Q_COL + i * LANES), qg_ref[...])) * (HEAD_DIM ** -0.5))
    for i in range(N_KV_HEADS // 2):
        k = rope(head_rms(slab(K_COL + i * LANES), kg_ref[...]))
        nk_ref[:, i * LANES:(i + 1) * LANES] = k
        put_heads(k_ref, i, k)
        v = slab(V_COL + i * LANES)
        nv_ref[:, i * LANES:(i + 1) * LANES] = v
        put_heads(v_ref, i, v)
    for i in range(IDX_HEADS // 2):
        put_heads(qi_ref, i, rope(slab(QI_COL + i * LANES)))
    tail = slab(KI_COL)
    ki = rope(head_rms(tail, ikg_ref[...]))[:, :IDX_DIM]
    nki_ref[...] = ki
    ki_ref[...] = ki.astype(BF16)
    wi_ref[...] = tail[:, IDX_DIM:IDX_DIM + IDX_HEADS]


def _att_project(x_all, pos, g, w_in, qn_g, kn_g, ikn_g):
    n = x_all.shape[0]
    tb = ROW_BLOCK
    half = HEAD_DIM // 2
    inv = jnp.exp(-math.log(ROPE_THETA) * jnp.arange(half, dtype=F32) / half)
    ang = pos.astype(F32)[:, None] * inv[None, :]
    cos = jnp.tile(jnp.cos(ang), (1, LANES // half))
    sin = jnp.tile(jnp.concatenate([-jnp.sin(ang), jnp.sin(ang)], axis=1), (1, LANES // HEAD_DIM))
    pair = lambda gain: jnp.tile(gain, LANES // HEAD_DIM).reshape(1, LANES)
    w_pad = jnp.pad(w_in, ((0, 0), (0, ATT_COLS_PAD - ATT_COLS))).astype(BF16)
    heads = lambda n_heads: (pl.BlockSpec((n_heads, tb, HEAD_DIM), lambda i: (0, i, 0)),
                             jax.ShapeDtypeStruct((n_heads, n, HEAD_DIM), BF16))
    rows = lambda cols, dt: (pl.BlockSpec((tb, cols), lambda i: (i, 0)), jax.ShapeDtypeStruct((n, cols), dt))
    outs = [heads(N_HEADS), heads(N_KV_HEADS), heads(N_KV_HEADS), heads(IDX_HEADS), rows(IDX_DIM, BF16),
            rows(IDX_HEADS, F32), rows(N_KV_HEADS * HEAD_DIM, F32), rows(N_KV_HEADS * HEAD_DIM, F32),
            rows(IDX_DIM, F32)]
    return pl.pallas_call(
        _att_proj_kernel,
        grid=(n // tb,),
        in_specs=[
            pl.BlockSpec((tb, D_MODEL), lambda i: (i, 0)),
            _full_spec((1, D_MODEL)),
            _full_spec((D_MODEL, ATT_COLS_PAD), pipeline_mode=pl.Buffered(1)),
            _full_spec((1, LANES)),
            _full_spec((1, LANES)),
            _full_spec((1, LANES)),
            pl.BlockSpec((tb, LANES), lambda i: (i, 0)),
            pl.BlockSpec((tb, LANES), lambda i: (i, 0)),
        ],
        out_specs=[o[0] for o in outs],
        out_shape=[o[1] for o in outs],
        compiler_params=pltpu.CompilerParams(dimension_semantics=("arbitrary",), vmem_limit_bytes=VMEM_LIMIT),
        name="att_project",
    )(x_all, g.reshape(1, D_MODEL), w_pad, pair(qn_g), pair(kn_g), pair(ikn_g), cos, sin)


DSA_TQ = 256
DSA_CLASS = 512
INT32_MIN = -(2 ** 31)


def _sortable_key(score):
    bits = lax.bitcast_convert_type(score, I32)
    return bits ^ ((bits >> 31) & jnp.int32(0x7FFFFFFF))


def _dsa_prompt_kernel(qi_ref, wi_ref, ki_ref, q_ref, k_ref, v_ref, x_ref, wo_ref, out_ref,
                       key_ref, bias_ref, *, s_eff, q_block0, n_keep):
    tq = DSA_TQ
    q0 = (q_block0 + pl.program_id(1)) * tq
    ki = ki_ref[...]
    wi = wi_ref[...]
    score = jnp.zeros((tq, s_eff), F32)
    for h in range(IDX_HEADS):
        s = _nt_dot(qi_ref[h], ki, preferred_element_type=F32)
        score = score + wi[:, h:h + 1] * jnp.maximum(s, 0.0)
    score = score * IDX_SCALE
    q_pos = q0 + lax.broadcasted_iota(I32, (tq, 1), 0)
    k_pos = lax.broadcasted_iota(I32, (tq, s_eff), 1)
    score = jnp.where(k_pos <= q_pos, score, -jnp.inf)
    key_ref[...] = _sortable_key(score)

    def search(it, thr):
        cand = thr + jnp.left_shift(jnp.int32(1), 31 - it)
        cnt = jnp.sum(jnp.where(key_ref[...] >= cand, 1, 0), axis=1, keepdims=True)
        return jnp.where(cnt >= n_keep, cand, thr)

    thr = lax.fori_loop(0, 32, search, jnp.full((tq, 1), INT32_MIN, I32))
    n_gt = jnp.sum(jnp.where(key_ref[...] > thr, 1, 0), axis=1, keepdims=True)
    need = (n_keep - n_gt).astype(F32)
    r_i = lax.broadcasted_iota(I32, (LANES, LANES), 0)
    c_i = lax.broadcasted_iota(I32, (LANES, LANES), 1)
    before = jnp.where(r_i < c_i, 1.0, 0.0).astype(BF16)
    run = jnp.zeros((tq, 1), F32)
    for c in range(s_eff // LANES):
        sl = slice(c * LANES, (c + 1) * LANES)
        key_c = key_ref[:, sl]
        tie = jnp.where(key_c == thr, 1.0, 0.0)
        rank = run + jnp.dot(tie.astype(BF16), before, preferred_element_type=F32)
        keep_tie = jnp.where(rank < need, tie, 0.0)
        keep = jnp.where(key_c > thr, 1.0, keep_tie)
        adm = (c * LANES + lax.broadcasted_iota(I32, (tq, LANES), 1)) <= q_pos
        bias_ref[:, sl] = jnp.where(adm, jnp.where(keep > 0.0, 0.0, -jnp.inf), -jnp.inf)
        run = run + jnp.sum(tie, axis=1, keepdims=True)

    outs = []
    for hk in range(N_KV_HEADS):
        k_h = k_ref[hk]
        v_h = v_ref[hk]
        for g in range(GROUP):
            logits = _nt_dot(q_ref[hk * GROUP + g], k_h, preferred_element_type=F32) + bias_ref[...]
            m = jnp.max(logits, axis=1, keepdims=True)
            p = jnp.exp(logits - m)
            den = jnp.sum(p, axis=1, keepdims=True)
            o = jnp.dot(p.astype(BF16), v_h, preferred_element_type=F32)
            outs.append(o / den)
    o_all = jnp.concatenate(outs, axis=1)
    out_ref[...] = x_ref[...] + jnp.dot(o_all.astype(BF16), wo_ref[...], preferred_element_type=F32)


def _dsa_prompt_call(x_all, qi, wi, ki, q, k, v, w_out, *, n_batch, seq, s_eff, q_block0, n_qblocks, n_keep):
    tq = DSA_TQ
    per_seq = seq // tq
    q_rows = lambda b, j: b * per_seq + q_block0 + j
    keys = lambda n_heads: pl.BlockSpec((pl.Element(n_heads), pl.Element(s_eff), pl.Element(HEAD_DIM)),
                                        lambda b, j: (0, b * seq, 0), pipeline_mode=pl.Buffered(1))
    body = functools.partial(_dsa_prompt_kernel, s_eff=s_eff, q_block0=q_block0, n_keep=n_keep)
    return pl.pallas_call(
        body,
        grid=(n_batch, n_qblocks),
        in_specs=[
            pl.BlockSpec((IDX_HEADS, tq, IDX_DIM), lambda b, j: (0, q_rows(b, j), 0)),
            pl.BlockSpec((tq, IDX_HEADS), lambda b, j: (q_rows(b, j), 0)),
            pl.BlockSpec((pl.Element(s_eff), pl.Element(IDX_DIM)), lambda b, j: (b * seq, 0),
                         pipeline_mode=pl.Buffered(1)),
            pl.BlockSpec((N_HEADS, tq, HEAD_DIM), lambda b, j: (0, q_rows(b, j), 0)),
            keys(N_KV_HEADS),
            keys(N_KV_HEADS),
            pl.BlockSpec((tq, D_MODEL), lambda b, j: (q_rows(b, j), 0)),
            _full_spec((D_MODEL, D_MODEL), pipeline_mode=pl.Buffered(1)),
        ],
        out_specs=pl.BlockSpec((tq, D_MODEL), lambda b, j: (q_rows(b, j), 0)),
        out_shape=jax.ShapeDtypeStruct(x_all.shape, F32),
        input_output_aliases={6: 0},
        scratch_shapes=[pltpu.VMEM((tq, s_eff), I32), pltpu.VMEM((tq, s_eff), F32)],
        compiler_params=pltpu.CompilerParams(
            dimension_semantics=("arbitrary", "arbitrary"), vmem_limit_bytes=VMEM_LIMIT),
        name=f"dsa_prompt_{s_eff}",
    )(qi, wi, ki, q, k, v, x_all, w_out)


def _dsa_prompt(x_all, proj, w_out, n_batch, seq, class_rows=DSA_CLASS):
    q, k, v, qi, ki, wi = proj
    n_keep = min(TOPK_MAX, seq // 4)
    wo_b = w_out.astype(BF16)
    per_call = class_rows // DSA_TQ
    for c in range(seq // class_rows):
        x_all = _dsa_prompt_call(x_all, qi, wi, ki, q, k, v, wo_b, n_batch=n_batch, seq=seq,
                                 s_eff=(c + 1) * class_rows, q_block0=c * per_call, n_qblocks=per_call,
                                 n_keep=n_keep)
    return x_all


PAGE_GROUP = 8


def _dsa_sample_kernel(pt_ref, *refs, n_keep, n_pages):
    del pt_ref
    pg = PAGE_GROUP
    ki_pages, k_pages, v_pages = refs[:pg], refs[pg:2 * pg], refs[2 * pg:3 * pg]
    (qi_ref, wi_ref, kin_ref, q_ref, kn_ref, vn_ref, x_ref, wo_ref, out_ref,
     score_ref, bias_ref, bnew_ref, m_ref, l_ref, acc_ref) = refs[3 * pg:]
    phase = pl.program_id(1)
    j = pl.program_id(2)
    last = pl.num_programs(2) - 1
    total = lambda a: jnp.sum(jnp.sum(a, axis=1, keepdims=True), axis=0, keepdims=True)
    head_group = lax.broadcasted_iota(I32, (N_HEADS, 1), 0) // GROUP

    def weigh(s):
        return jnp.sum(wi_ref[...] * jnp.maximum(s, 0.0), axis=0, keepdims=True) * IDX_SCALE

    def select():
        kin = kin_ref[...].astype(BF16).astype(F32)
        s_new = weigh(jnp.sum(qi_ref[...].astype(F32) * kin, axis=1, keepdims=True))
        key = _sortable_key(score_ref[...])
        key_new = _sortable_key(s_new)

        def search(it, thr):
            cand = thr + jnp.left_shift(jnp.int32(1), 31 - it)
            cnt = total(jnp.where(key >= cand, 1, 0)) + jnp.where(key_new >= cand, 1, 0)
            return jnp.where(cnt >= n_keep, cand, thr)

        thr = lax.fori_loop(0, 32, search, jnp.full((1, 1), INT32_MIN, I32))
        n_gt = total(jnp.where(key > thr, 1, 0)) + jnp.where(key_new > thr, 1, 0)
        need = (n_keep - n_gt).astype(F32)
        tie = jnp.where(key == thr, 1.0, 0.0)
        r_i = lax.broadcasted_iota(I32, (LANES, LANES), 0)
        c_i = lax.broadcasted_iota(I32, (LANES, LANES), 1)
        before = jnp.where(r_i < c_i, 1.0, 0.0).astype(BF16)
        rr = lax.broadcasted_iota(I32, (n_pages, n_pages), 0)
        cc = lax.broadcasted_iota(I32, (n_pages, n_pages), 1)
        rows_above = jnp.where(cc < rr, 1.0, 0.0).astype(BF16)
        tie_b = tie.astype(BF16)
        rank = (jnp.dot(tie_b, before, preferred_element_type=F32)
                + jnp.sum(jnp.dot(rows_above, tie_b, preferred_element_type=F32), axis=1, keepdims=True))
        keep = jnp.where(key > thr, 1.0, jnp.where(rank < need, tie, 0.0))
        bias_ref[...] = jnp.where(keep > 0.0, 0.0, -jnp.inf)
        keep_new = jnp.where(key_new > thr, 1.0,
                             jnp.where(key_new == thr, jnp.where(total(tie) < need, 1.0, 0.0), 0.0))
        bnew_ref[...] = jnp.where(keep_new > 0.0, 0.0, -jnp.inf)

    @pl.when(phase == 0)
    def _():
        for i in range(pg):
            s = _nt_dot(qi_ref[...], ki_pages[i][...].astype(BF16), preferred_element_type=F32)
            score_ref[pl.ds(j * pg + i, 1), :] = weigh(s)
        pl.when(j == last)(select)

    def flash_update(logits, pv_of):
        m_old = m_ref[...]
        m_new = jnp.maximum(m_old, jnp.max(logits, axis=1, keepdims=True))
        m_safe = jnp.where(m_new == -jnp.inf, 0.0, m_new)
        alpha = jnp.exp(m_old - m_safe)
        p = jnp.exp(logits - m_safe)
        m_ref[...] = m_new
        l_ref[...] = alpha * l_ref[...] + jnp.sum(p, axis=1, keepdims=True)
        acc_ref[...] = alpha * acc_ref[...] + pv_of(p)

    def by_group(per_kv_head):
        out = per_kv_head(0)
        for hk in range(1, N_KV_HEADS):
            out = jnp.where(head_group == hk, per_kv_head(hk), out)
        return out

    @pl.when(phase == 1)
    def _():
        @pl.when(j == 0)
        def _():
            m_ref[...] = jnp.full(m_ref.shape, -jnp.inf, F32)
            l_ref[...] = jnp.zeros(l_ref.shape, F32)
            acc_ref[...] = jnp.zeros(acc_ref.shape, F32)

        q = q_ref[...]
        head = lambda a, hk: a[:, hk * HEAD_DIM:(hk + 1) * HEAD_DIM]
        k_pgs = [k_pages[i][...].astype(BF16) for i in range(pg)]
        v_pgs = [v_pages[i][...].astype(BF16) for i in range(pg)]
        logits = jnp.concatenate(
            [by_group(lambda hk: _nt_dot(q, head(k_pgs[i], hk), preferred_element_type=F32))
             + bias_ref[pl.ds(j * pg + i, 1), :] for i in range(pg)], axis=1)

        def pv_of(p):
            p = p.astype(BF16)
            page = lambda i: p[:, i * PAGE_SIZE:(i + 1) * PAGE_SIZE]
            return by_group(lambda hk: sum(
                jnp.dot(page(i), head(v_pgs[i], hk), preferred_element_type=F32) for i in range(pg)))

        flash_update(logits, pv_of)

        @pl.when(j == last)
        def _():
            rows = lambda ref: by_group(lambda hk: ref[hk:hk + 1, :]).astype(BF16).astype(F32)
            logit_new = jnp.sum(q.astype(F32) * rows(kn_ref), axis=1, keepdims=True) + bnew_ref[...]
            flash_update(logit_new, lambda p: p.astype(BF16).astype(F32) * rows(vn_ref))
            o = (acc_ref[...] / l_ref[...]).astype(BF16)
            y = x_ref[...]
            for h in range(N_HEADS):
                o_h = jnp.broadcast_to(o[h:h + 1, :], (2 * SUBLANES, HEAD_DIM))
                y = y + jnp.dot(o_h, wo_ref[h], preferred_element_type=F32)[0:1]
            out_ref[...] = y


def _dsa_sample(x_s, q_s, qi_s, wi_s, ki_new, k_new, v_new, cache_k, cache_v, cache_idx_k, page_table, w_out):
    n_dec = x_s.shape[0]
    n_pages = page_table.shape[1]
    n_pool = cache_k.shape[0]
    pg = PAGE_GROUP
    n_groups = n_pages // pg
    n_keep = min(TOPK_MAX, (n_pages * PAGE_SIZE + 1) // 4)
    idx_page = (IDX_DIM,)
    kv_page = (N_KV_HEADS * HEAD_DIM,)

    def page_spec(tail, i, live_phase):
        def index(b, ph, j, pt):
            idle = i if live_phase == 1 else (n_groups - 1) * pg + i
            return (pt[b, jnp.where(ph == live_phase, j * pg + i, idle)], 0) + (0,) * len(tail)
        return pl.BlockSpec((None, PAGE_SIZE) + tail, index)

    per_seq = lambda *shape: pl.BlockSpec((None,) + shape, lambda b, ph, j, pt: (b,) + (0,) * len(shape))
    grid_spec = pltpu.PrefetchScalarGridSpec(
        num_scalar_prefetch=1,
        grid=(n_dec, 2, n_groups),
        in_specs=([page_spec(idx_page, i, 0) for i in range(pg)]
                  + [page_spec(kv_page, i, 1) for i in range(pg)]
                  + [page_spec(kv_page, i, 1) for i in range(pg)]
                  + [per_seq(IDX_HEADS, IDX_DIM), per_seq(IDX_HEADS, 1), per_seq(1, IDX_DIM),
                     per_seq(N_HEADS, HEAD_DIM), per_seq(N_KV_HEADS, HEAD_DIM), per_seq(N_KV_HEADS, HEAD_DIM),
                     per_seq(1, D_MODEL),
                     pl.BlockSpec((N_HEADS, HEAD_DIM, D_MODEL), lambda b, ph, j, pt: (0, 0, 0))]),
        out_specs=per_seq(1, D_MODEL),
        scratch_shapes=[pltpu.VMEM((n_pages, PAGE_SIZE), F32), pltpu.VMEM((n_pages, PAGE_SIZE), F32),
                        pltpu.VMEM((1, 1), F32), pltpu.VMEM((N_HEADS, 1), F32), pltpu.VMEM((N_HEADS, 1), F32),
                        pltpu.VMEM((N_HEADS, HEAD_DIM), F32)],
    )
    ck = cache_k.reshape((n_pool, PAGE_SIZE) + kv_page)
    cv = cache_v.reshape((n_pool, PAGE_SIZE) + kv_page)
    out = pl.pallas_call(
        functools.partial(_dsa_sample_kernel, n_keep=n_keep, n_pages=n_pages),
        grid_spec=grid_spec,
        out_shape=jax.ShapeDtypeStruct((n_dec, 1, D_MODEL), F32),
        compiler_params=pltpu.CompilerParams(
            dimension_semantics=("arbitrary", "arbitrary", "arbitrary"), vmem_limit_bytes=VMEM_LIMIT),
        name="dsa_sample",
    )(page_table, *([cache_idx_k] * pg), *([ck] * pg), *([cv] * pg),
      qi_s, wi_s.reshape(n_dec, IDX_HEADS, 1), ki_new.reshape(n_dec, 1, IDX_DIM), q_s,
      k_new.reshape(n_dec, N_KV_HEADS, HEAD_DIM), v_new.reshape(n_dec, N_KV_HEADS, HEAD_DIM),
      x_s.reshape(n_dec, 1, D_MODEL), w_out.astype(BF16).reshape(N_HEADS, HEAD_DIM, D_MODEL))
    return out.reshape(n_dec, D_MODEL)


def kernel(x_prompt, x_sample, cache_k, cache_v, cache_idx_k, page_table, norm_mix_g, norm_ffn_g,
           gmlp_w_in, gmlp_v_g, gmlp_w_s, gmlp_b_s, gmlp_w_out,
           att_w_in, att_q_norm_g, att_k_norm_g, att_idx_k_norm_g, att_w_out,
           peer_w_q, peer_sub_k1, peer_sub_k2, peer_u, peer_v):
    n_batch, seq, _ = x_prompt.shape
    n_dec, dec_seq, _ = x_sample.shape
    assert dec_seq == 1 and seq % DSA_CLASS == 0 and (n_batch * seq) % ROW_BLOCK == 0
    n_prompt = n_batch * seq
    n_tok = n_prompt + n_dec
    n_rows = n_prompt + ROW_BLOCK
    assert n_dec <= ROW_BLOCK

    def peer(x, layer):
        return _peer_layer(x, norm_ffn_g[layer], peer_w_q[layer], peer_sub_k1[layer], peer_sub_k2[layer],
                           peer_u[layer], peer_v[layer])

    x_all = jnp.concatenate([x_prompt.reshape(n_prompt, D_MODEL), x_sample.reshape(n_dec, D_MODEL),
                             jnp.zeros((n_rows - n_tok, D_MODEL), F32)], axis=0)
    x_all, v_new = _gmlp_layer(x_all, n_prompt, norm_mix_g[0], gmlp_w_in[0], gmlp_v_g[0], gmlp_w_s[0],
                               gmlp_b_s[0], gmlp_w_out[0])
    x_all = peer(x_all, 0)
    pos = jnp.concatenate([jnp.tile(jnp.arange(seq), n_batch), jnp.full((n_rows - n_prompt,), PAST_LEN)])
    q, k, v, qi, ki, wi, new_k, new_v, new_ki = _att_project(
        x_all, pos, norm_mix_g[1], att_w_in[0], att_q_norm_g[0], att_k_norm_g[0], att_idx_k_norm_g[0])
    smp = slice(n_prompt, n_tok)
    heads = lambda a: a[:, smp].transpose(1, 0, 2)
    xs = _dsa_sample(x_all[smp], heads(q), heads(qi), wi[smp], new_ki[smp], new_k[smp], new_v[smp],
                     cache_k[0], cache_v[0], cache_idx_k[0], page_table, att_w_out[0])
    x_all = _dsa_prompt(x_all, (q, k, v, qi, ki, wi), att_w_out[0], n_batch, seq)
    x_all = lax.dynamic_update_slice(x_all, xs, (n_prompt, 0))
    x_all = peer(x_all, 1)

    kv_shape = (1, n_batch, seq, N_KV_HEADS, HEAD_DIM)
    kv_s_shape = (1, n_dec, 1, N_KV_HEADS, HEAD_DIM)
    return (x_all[:n_prompt].reshape(n_batch, seq, D_MODEL), x_all[smp].reshape(n_dec, 1, D_MODEL),
            new_k[:n_prompt].reshape(kv_shape), new_v[:n_prompt].reshape(kv_shape),
            new_ki[:n_prompt].reshape(1, n_batch, seq, IDX_DIM),
            new_k[smp].reshape(kv_s_shape), new_v[smp].reshape(kv_s_shape),
            new_ki[smp].reshape(1, n_dec, 1, IDX_DIM), v_new[:n_dec].reshape(1, n_dec, 1, GMLP_WIDTH))
```

```python
import functools
import math

import jax
import jax.numpy as jnp
import numpy as np
from jax import lax
from jax.experimental import pallas as pl
from jax.experimental.pallas import tpu as pltpu

F32 = jnp.float32
BF16 = jnp.bfloat16
I32 = jnp.int32

D_MODEL = 1024
EPS = 1e-6

LANES = 128
SUBLANES = 8
MIB = 1024 * 1024
VMEM_LIMIT = 56 * MIB

ROW_BLOCK = 256

PEER_HEADS = 8
N_KEYS = 128
N_EXPERTS = N_KEYS * N_KEYS
PEER_TOPK = 16
PEER_KEY_DIM = 256
PEER_HALF = PEER_KEY_DIM // 2
PEER_PICKS = PEER_HEADS * PEER_TOPK
ROW_WORDS = D_MODEL // 2
ROW_SUB = ROW_WORDS // LANES
PEER_TOKENS = 128
N_CAND = sum(PEER_TOPK // (a + 1) for a in range(PEER_TOPK))

CHUNK = 128
GMLP_WIDTH = 2 * D_MODEL
GMLP_GROUPS = 8
GMLP_GROUP_DIM = GMLP_WIDTH // GMLP_GROUPS

PAST_LEN = 16384
PAGE_SIZE = 128
N_HEADS = 16
HEAD_DIM = D_MODEL // N_HEADS
N_KV_HEADS = 4
GROUP = N_HEADS // N_KV_HEADS
IDX_HEADS = 8
IDX_DIM = 64
TOPK_MAX = 256
ROPE_THETA = 10000.0
IDX_SCALE = (IDX_DIM ** -0.5) * (IDX_HEADS ** -0.5)
ATT_SIZES = (N_HEADS * HEAD_DIM, N_KV_HEADS * HEAD_DIM, N_KV_HEADS * HEAD_DIM,
             IDX_HEADS * IDX_DIM, IDX_DIM, IDX_HEADS)
ATT_COLS = sum(ATT_SIZES)
ATT_COLS_PAD = -(-ATT_COLS // LANES) * LANES
Q_COL, K_COL, V_COL, QI_COL, KI_COL = (int(c) for c in np.cumsum((0,) + ATT_SIZES[:4]))


def _nt_dot(a, b, **kw):
    return lax.dot_general(a, b, (((1,), (1,)), ((), ())), **kw)


def _rms_rows(x, g):
    return x * lax.rsqrt(jnp.mean(x * x, axis=-1, keepdims=True) + EPS) * g


def _full_spec(shape, **kw):
    return pl.BlockSpec(shape, lambda *_: (0,) * len(shape), **kw)


def _topk_rows(s, k, iota):
    n_rows = s.shape[0]
    vals, idxs = [], []
    for _ in range(k):
        m = jnp.max(s, axis=0, keepdims=True)
        first = jnp.min(jnp.where(s == m, iota, n_rows), axis=0, keepdims=True)
        vals.append(m)
        idxs.append(first)
        s = jnp.where(iota == first, -jnp.inf, s)
    return jnp.concatenate(vals, axis=0), jnp.concatenate(idxs, axis=0)


def _peer_router_kernel(x_ref, g_ref, wq_ref, k1_ref, k2_ref, h_ref, eidx_ref, gate_ref):
    h = _rms_rows(x_ref[...], g_ref[...])
    h_ref[...] = h
    q = jnp.dot(h.astype(BF16), wq_ref[...], preferred_element_type=F32)
    n_tok = h.shape[0]
    iota_k = lax.broadcasted_iota(I32, (N_KEYS, n_tok), 0)
    iota_c = lax.broadcasted_iota(I32, (N_CAND, n_tok), 0)
    k1 = k1_ref[...]
    k2 = k2_ref[...]
    eidx_rows, gate_rows = [], []
    for head in range(PEER_HEADS):
        base = head * PEER_KEY_DIM
        q1 = q[:, base:base + PEER_HALF].astype(BF16)
        q2 = q[:, base + PEER_HALF:base + PEER_KEY_DIM].astype(BF16)
        s1 = _nt_dot(k1, q1, preferred_element_type=F32)
        s2 = _nt_dot(k2, q2, preferred_element_type=F32)
        v1, i1 = _topk_rows(s1, PEER_TOPK, iota_k)
        v2, i2 = _topk_rows(s2, PEER_TOPK, iota_k)
        width = [PEER_TOPK // (a + 1) for a in range(PEER_TOPK)]
        cand = jnp.concatenate([v1[a:a + 1] + v2[:width[a]] for a in range(PEER_TOPK)], axis=0)
        cidx = jnp.concatenate([(i1[a:a + 1] * N_KEYS + i2[:width[a]]) * ROW_SUB for a in range(PEER_TOPK)], axis=0)
        top_s, top_e = [], []
        for _ in range(PEER_TOPK):
            m = jnp.max(cand, axis=0, keepdims=True)
            first = jnp.min(jnp.where(cand == m, iota_c, N_CAND), axis=0, keepdims=True)
            hit = iota_c == first
            top_s.append(m)
            top_e.append(jnp.sum(jnp.where(hit, cidx, 0), axis=0, keepdims=True))
            cand = jnp.where(hit, -jnp.inf, cand)
        top_s = jnp.concatenate(top_s, axis=0)
        e = jnp.exp(top_s - top_s[0:1])
        gate_rows.append(e / jnp.sum(e, axis=0, keepdims=True))
        eidx_rows.append(jnp.concatenate(top_e, axis=0))
    gate_t = jnp.concatenate(gate_rows, axis=0)
    eidx_t = jnp.concatenate(eidx_rows, axis=0)
    gate_ref[...] = gate_t.T
    eidx_ref[...] = lax.bitcast_convert_type(lax.bitcast_convert_type(eidx_t, F32).T, I32)


def _peer_router(x, g, w_q, sub_k1, sub_k2):
    n = x.shape[0]
    tm = PEER_TOKENS
    return pl.pallas_call(
        _peer_router_kernel,
        grid=(n // tm,),
        in_specs=[
            pl.BlockSpec((tm, D_MODEL), lambda i: (i, 0)),
            _full_spec((1, D_MODEL)),
            _full_spec((D_MODEL, PEER_HEADS * PEER_KEY_DIM)),
            _full_spec((N_KEYS, PEER_HALF)),
            _full_spec((N_KEYS, PEER_HALF)),
        ],
        out_specs=[
            pl.BlockSpec((tm, D_MODEL), lambda i: (i, 0)),
            pl.BlockSpec((tm, PEER_PICKS), lambda i: (i, 0)),
            pl.BlockSpec((tm, PEER_PICKS), lambda i: (i, 0)),
        ],
        out_shape=[
            jax.ShapeDtypeStruct((n, D_MODEL), F32),
            jax.ShapeDtypeStruct((n, PEER_PICKS), I32),
            jax.ShapeDtypeStruct((n, PEER_PICKS), F32),
        ],
        compiler_params=pltpu.CompilerParams(dimension_semantics=("arbitrary",), vmem_limit_bytes=VMEM_LIMIT),
        name="peer_router",
    )(x, g.reshape(1, D_MODEL), w_q.astype(BF16), sub_k1.astype(BF16), sub_k2.astype(BF16))


PACK_ROWS = 512


def _pack_kernel(w_ref, out_ref):
    w = w_ref[...]
    as_bits = lambda a: lax.bitcast_convert_type(a.astype(jnp.bfloat16).astype(F32), I32)
    hi = as_bits(w[:, :ROW_WORDS]) & jnp.int32(-65536)
    lo = (as_bits(w[:, ROW_WORDS:]) >> 16) & jnp.int32(0xFFFF)
    out_ref[...] = hi | lo


def _pack_table(w):
    words = pl.pallas_call(
        _pack_kernel,
        grid=(N_EXPERTS // PACK_ROWS,),
        in_specs=[pl.BlockSpec((PACK_ROWS, D_MODEL), lambda i: (i, 0))],
        out_specs=pl.BlockSpec((PACK_ROWS, ROW_WORDS), lambda i: (i, 0)),
        out_shape=jax.ShapeDtypeStruct((N_EXPERTS, ROW_WORDS), I32),
        compiler_params=pltpu.CompilerParams(dimension_semantics=("arbitrary",)),
        name="peer_pack_table",
    )(w)
    return words.reshape(N_EXPERTS * ROW_SUB, LANES)


PLANE_STRIDE = PEER_PICKS + SUBLANES
GATHER_ROWS = ROW_SUB * PLANE_STRIDE
TOKEN_GROUP = 2
N_GATHER_BUFS = 2 * TOKEN_GROUP


def _gather_token(eidx_ref, tab_ref, buf_ref, t):
    for p in range(PEER_PICKS):
        row0 = pl.multiple_of(eidx_ref[t, p], ROW_SUB)
        buf_ref[pl.ds(p, ROW_SUB, stride=PLANE_STRIDE), :] = tab_ref[pl.ds(row0, ROW_SUB), :]


def _gathered_rows(buf_ref):
    hi, lo = [], []
    for s in range(ROW_SUB):
        w = buf_ref[s * PLANE_STRIDE:s * PLANE_STRIDE + PEER_PICKS, :]
        hi.append(lax.bitcast_convert_type(w & jnp.int32(-65536), F32).astype(BF16))
        lo.append(lax.bitcast_convert_type(w << 16, F32).astype(BF16))
    return jnp.concatenate(hi + lo, axis=1)


def _token_pipeline(eidx_ref, tab_ref, bufs, n_tok, compute):
    g = TOKEN_GROUP
    groups = (bufs[:g], bufs[g:])
    for k, buf in enumerate(bufs):
        _gather_token(eidx_ref, tab_ref, buf, k)

    def step(i, carry):
        for j, group in enumerate(groups):
            t = 2 * g * i + g * j
            compute(t, group)
            for k, buf in enumerate(group):
                _gather_token(eidx_ref, tab_ref, buf, jnp.minimum(t + 2 * g + k, n_tok - 1))
        return carry

    lax.fori_loop(0, n_tok // (2 * g), step, 0)


def _group_rows(group):
    return jnp.concatenate([_gathered_rows(buf) for buf in group], axis=0)


def _token_rows(ref, t):
    return [ref[pl.ds(t + k, 1), :] for k in range(TOKEN_GROUP)]


def _own_block(n_rows):
    r = lax.broadcasted_iota(I32, (n_rows, TOKEN_GROUP * PEER_PICKS), 0)
    c = lax.broadcasted_iota(I32, (n_rows, TOKEN_GROUP * PEER_PICKS), 1)
    return r == c // PEER_PICKS


def _peer_act_kernel(eidx_ref, tab_ref, h_ref, gate_ref, coef_ref, *bufs):
    g = TOKEN_GROUP
    pad = jnp.zeros((2 * SUBLANES - g, D_MODEL), F32)

    def compute(t, group):
        x = jnp.concatenate(_token_rows(h_ref, t) + [pad], axis=0).astype(BF16)
        dots = _nt_dot(x, _group_rows(group), preferred_element_type=F32)
        act = jnp.concatenate([dots[k:k + 1, k * PEER_PICKS:(k + 1) * PEER_PICKS] for k in range(g)], axis=0)
        coef = jnp.concatenate(_token_rows(gate_ref, t), axis=0) * jax.nn.gelu(act)
        for k in range(g):
            coef_ref[pl.ds(t + k, 1), :] = coef[k:k + 1]

    _token_pipeline(eidx_ref, tab_ref, bufs, h_ref.shape[0], compute)


def _peer_out_kernel(eidx_ref, tab_ref, coef_ref, x_ref, out_ref, *bufs):
    g = TOKEN_GROUP
    own = _own_block(g)
    pad = jnp.zeros((2 * SUBLANES - g, g * PEER_PICKS), F32)

    def compute(t, group):
        c = jnp.concatenate(_token_rows(coef_ref, t), axis=0)
        lhs = jnp.concatenate([jnp.where(own, jnp.concatenate([c] * g, axis=1), 0.0), pad], axis=0)
        y = jnp.dot(lhs.astype(BF16), _group_rows(group), preferred_element_type=F32)
        for k in range(g):
            out_ref[pl.ds(t + k, 1), :] = x_ref[pl.ds(t + k, 1), :] + y[k:k + 1]

    _token_pipeline(eidx_ref, tab_ref, bufs, x_ref.shape[0], compute)


def _peer_gather_call(body, name, eidx, tab, a, b, out_cols):
    n = eidx.shape[0]
    tb = PEER_TOKENS
    rows = lambda cols: pl.BlockSpec((tb, cols), lambda i: (i, 0))
    return pl.pallas_call(
        body,
        grid=(n // tb,),
        in_specs=[
            pl.BlockSpec((tb, PEER_PICKS), lambda i: (i, 0), memory_space=pltpu.SMEM),
            _full_spec((N_EXPERTS * ROW_SUB, LANES), pipeline_mode=pl.Buffered(1)),
            rows(a.shape[1]),
            rows(b.shape[1]),
        ],
        out_specs=rows(out_cols),
        out_shape=jax.ShapeDtypeStruct((n, out_cols), F32),
        scratch_shapes=[pltpu.VMEM((GATHER_ROWS, LANES), I32) for _ in range(N_GATHER_BUFS)],
        compiler_params=pltpu.CompilerParams(dimension_semantics=("arbitrary",), vmem_limit_bytes=VMEM_LIMIT),
        name=name,
    )(eidx, tab, a, b)


def _peer_layer(x, g, w_q, sub_k1, sub_k2, expert_u, expert_v):
    h, eidx, gate = _peer_router(x, g, w_q, sub_k1, sub_k2)
    coef = _peer_gather_call(_peer_act_kernel, "peer_act", eidx, _pack_table(expert_u), h, gate, PEER_PICKS)
    return _peer_gather_call(_peer_out_kernel, "peer_out", eidx, _pack_table(expert_v), coef, x, D_MODEL)


def _gmlp_kernel(x_ref, g_ref, win_ref, vg_ref, ws_ref, bst_ref, s0_ref, b0_ref, wout_ref,
                 out_ref, vnew_ref, gated_ref, *, n_chunk_blocks):
    x = x_ref[...]
    h = _rms_rows(x, g_ref[...])
    uv = jnp.dot(h.astype(BF16), win_ref[...], preferred_element_type=F32)
    v = _rms_rows(uv[:, GMLP_WIDTH:], vg_ref[...])
    is_chunked = pl.program_id(0) < n_chunk_blocks

    @pl.when(is_chunked)
    def _():
        r_i = lax.broadcasted_iota(I32, (CHUNK, CHUNK), 0)
        c_i = lax.broadcasted_iota(I32, (CHUNK, CHUNK), 1)
        causal = c_i <= r_i
        for g in range(GMLP_GROUPS):
            cols = slice(g * GMLP_GROUP_DIM, (g + 1) * GMLP_GROUP_DIM)
            ws = jnp.where(causal, ws_ref[g], 0.0).astype(BF16)
            bias = bst_ref[:, g:g + 1]
            for c in range(x.shape[0] // CHUNK):
                rows = slice(c * CHUNK, (c + 1) * CHUNK)
                mixed = jnp.dot(ws, v[rows, cols].astype(BF16), preferred_element_type=F32) + bias
                gated_ref[rows, cols] = (uv[rows, cols] * mixed).astype(BF16)

    @pl.when(jnp.logical_not(is_chunked))
    def _():
        gated_ref[...] = (uv[:, :GMLP_WIDTH] * (v * s0_ref[...] + b0_ref[...])).astype(BF16)
        vnew_ref[...] = v

    out_ref[...] = x + jnp.dot(gated_ref[...], wout_ref[...], preferred_element_type=F32)


def _gmlp_layer(x_all, n_chunk_rows, g, w_in, v_g, w_s, b_s, w_out):
    n = x_all.shape[0]
    tb = ROW_BLOCK
    spread = lambda a: jnp.repeat(a, GMLP_GROUP_DIM).reshape(1, GMLP_WIDTH)
    body = functools.partial(_gmlp_kernel, n_chunk_blocks=n_chunk_rows // tb)
    return pl.pallas_call(
        body,
        grid=(n // tb,),
        in_specs=[
            pl.BlockSpec((tb, D_MODEL), lambda i: (i, 0)),
            _full_spec((1, D_MODEL)),
            _full_spec((D_MODEL, 2 * GMLP_WIDTH), pipeline_mode=pl.Buffered(1)),
            _full_spec((1, GMLP_WIDTH)),
            _full_spec((GMLP_GROUPS, CHUNK, CHUNK)),
            _full_spec((CHUNK, GMLP_GROUPS)),
            _full_spec((1, GMLP_WIDTH)),
            _full_spec((1, GMLP_WIDTH)),
            _full_spec((GMLP_WIDTH, D_MODEL), pipeline_mode=pl.Buffered(1)),
        ],
        out_specs=[
            pl.BlockSpec((tb, D_MODEL), lambda i: (i, 0)),
            _full_spec((tb, GMLP_WIDTH)),
        ],
        out_shape=[
            jax.ShapeDtypeStruct((n, D_MODEL), F32),
            jax.ShapeDtypeStruct((tb, GMLP_WIDTH), F32),
        ],
        scratch_shapes=[pltpu.VMEM((tb, GMLP_WIDTH), BF16)],
        compiler_params=pltpu.CompilerParams(dimension_semantics=("arbitrary",), vmem_limit_bytes=VMEM_LIMIT),
        name="gmlp_mixer",
    )(x_all, g.reshape(1, D_MODEL), w_in.astype(BF16), v_g.reshape(1, GMLP_WIDTH), w_s, b_s.T,
      spread(w_s[:, 0, 0]), spread(b_s[:, 0]), w_out.astype(BF16))


def _att_proj_kernel(x_ref, g_ref, w_ref, qg_ref, kg_ref, ikg_ref, cos_ref, sin_ref,
                     q_ref, k_ref, v_ref, qi_ref, ki_ref, wi_ref, nk_ref, nv_ref, nki_ref):
    h = _rms_rows(x_ref[...], g_ref[...])
    p = jnp.dot(h.astype(BF16), w_ref[...], preferred_element_type=F32)
    n_tok = p.shape[0]
    cos = cos_ref[...]
    sin = sin_ref[...]
    lane = lax.broadcasted_iota(I32, (n_tok, LANES), 1)
    first_half = (lane & (HEAD_DIM // 2)) == 0
    r_i = lax.broadcasted_iota(I32, (LANES, LANES), 0)
    c_i = lax.broadcasted_iota(I32, (LANES, LANES), 1)
    head_mean = jnp.where((r_i // HEAD_DIM) == (c_i // HEAD_DIM), 1.0 / HEAD_DIM, 0.0).astype(BF16)

    def slab(col):
        return p[:, col:col + LANES]

    def head_rms(a, gain):
        ms = jnp.dot((a * a).astype(BF16), head_mean, preferred_element_type=F32)
        return a * lax.rsqrt(ms + EPS) * gain

    def rope(a):
        partner = jnp.where(first_half, pltpu.roll(a, LANES - HEAD_DIM // 2, 1), pltpu.roll(a, HEAD_DIM // 2, 1))
        return a * cos + partner * sin

    def put_heads(ref, i, a):
        ref[2 * i] = a[:, :HEAD_DIM].astype(BF16)
        ref[2 * i + 1] = a[:, HEAD_DIM:].astype(BF16)

    for i in range(N_HEADS // 2):
        put_heads(q_ref, i, rope(head_rms(slab(Q_COL + i * LANES), qg_ref[...])) * (HEAD_DIM ** -0.5))
    for i in range(N_KV_HEADS // 2):
        k = rope(head_rms(slab(K_COL + i * LANES), kg_ref[...]))
        nk_ref[:, i * LANES:(i + 1) * LANES] = k
        put_heads(k_ref, i, k)
        v = slab(V_COL + i * LANES)
        nv_ref[:, i * LANES:(i + 1) * LANES] = v
        put_heads(v_ref, i, v)
    for i in range(IDX_HEADS // 2):
        put_heads(qi_ref, i, rope(slab(QI_COL + i * LANES)))
    tail = slab(KI_COL)
    ki = rope(head_rms(tail, ikg_ref[...]))[:, :IDX_DIM]
    nki_ref[...] = ki
    ki_ref[...] = ki.astype(BF16)
    wi_ref[...] = tail[:, IDX_DIM:IDX_DIM + IDX_HEADS]


def _att_project(x_all, pos, g, w_in, qn_g, kn_g, ikn_g):
    n = x_all.shape[0]
    tb = ROW_BLOCK
    half = HEAD_DIM // 2
    inv = jnp.exp(-math.log(ROPE_THETA) * jnp.arange(half, dtype=F32) / half)
    ang = pos.astype(F32)[:, None] * inv[None, :]
    cos = jnp.tile(jnp.cos(ang), (1, LANES // half))
    sin = jnp.tile(jnp.concatenate([-jnp.sin(ang), jnp.sin(ang)], axis=1), (1, LANES // HEAD_DIM))
    pair = lambda gain: jnp.tile(gain, LANES // HEAD_DIM).reshape(1, LANES)
    w_pad = jnp.pad(w_in, ((0, 0), (0, ATT_COLS_PAD - ATT_COLS))).astype(BF16)
    heads = lambda n_heads: (pl.BlockSpec((n_heads, tb, HEAD_DIM), lambda i: (0, i, 0)),
                             jax.ShapeDtypeStruct((n_heads, n, HEAD_DIM), BF16))
    rows = lambda cols, dt: (pl.BlockSpec((tb, cols), lambda i: (i, 0)), jax.ShapeDtypeStruct((n, cols), dt))
    outs = [heads(N_HEADS), heads(N_KV_HEADS), heads(N_KV_HEADS), heads(IDX_HEADS), rows(IDX_DIM, BF16),
            rows(IDX_HEADS, F32), rows(N_KV_HEADS * HEAD_DIM, F32), rows(N_KV_HEADS * HEAD_DIM, F32),
            rows(IDX_DIM, F32)]
    return pl.pallas_call(
        _att_proj_kernel,
        grid=(n // tb,),
        in_specs=[
            pl.BlockSpec((tb, D_MODEL), lambda i: (i, 0)),
            _full_spec((1, D_MODEL)),
            _full_spec((D_MODEL, ATT_COLS_PAD), pipeline_mode=pl.Buffered(1)),
            _full_spec((1, LANES)),
            _full_spec((1, LANES)),
            _full_spec((1, LANES)),
            pl.BlockSpec((tb, LANES), lambda i: (i, 0)),
            pl.BlockSpec((tb, LANES), lambda i: (i, 0)),
        ],
        out_specs=[o[0] for o in outs],
        out_shape=[o[1] for o in outs],
        compiler_params=pltpu.CompilerParams(dimension_semantics=("arbitrary",), vmem_limit_bytes=VMEM_LIMIT),
        name="att_project",
    )(x_all, g.reshape(1, D_MODEL), w_pad, pair(qn_g), pair(kn_g), pair(ikn_g), cos, sin)


DSA_TQ = 256
DSA_CLASS = 512
INT32_MIN = -(2 ** 31)


def _sortable_key(score):
    bits = lax.bitcast_convert_type(score, I32)
    return bits ^ ((bits >> 31) & jnp.int32(0x7FFFFFFF))


def _dsa_prompt_kernel(qi_ref, wi_ref, ki_ref, q_ref, k_ref, v_ref, x_ref, wo_ref, out_ref,
                       key_ref, bias_ref, *, s_eff, q_block0, n_keep):
    tq = DSA_TQ
    q0 = (q_block0 + pl.program_id(1)) * tq
    ki = ki_ref[...]
    wi = wi_ref[...]
    score = jnp.zeros((tq, s_eff), F32)
    for h in range(IDX_HEADS):
        s = _nt_dot(qi_ref[h], ki, preferred_element_type=F32)
        score = score + wi[:, h:h + 1] * jnp.maximum(s, 0.0)
    score = score * IDX_SCALE
    q_pos = q0 + lax.broadcasted_iota(I32, (tq, 1), 0)
    k_pos = lax.broadcasted_iota(I32, (tq, s_eff), 1)
    score = jnp.where(k_pos <= q_pos, score, -jnp.inf)
    key_ref[...] = _sortable_key(score)

    def search(it, thr):
        cand = thr + jnp.left_shift(jnp.int32(1), 31 - it)
        cnt = jnp.sum(jnp.where(key_ref[...] >= cand, 1, 0), axis=1, keepdims=True)
        return jnp.where(cnt >= n_keep, cand, thr)

    thr = lax.fori_loop(0, 32, search, jnp.full((tq, 1), INT32_MIN, I32))
    n_gt = jnp.sum(jnp.where(key_ref[...] > thr, 1, 0), axis=1, keepdims=True)
    need = (n_keep - n_gt).astype(F32)
    r_i = lax.broadcasted_iota(I32, (LANES, LANES), 0)
    c_i = lax.broadcasted_iota(I32, (LANES, LANES), 1)
    before = jnp.where(r_i < c_i, 1.0, 0.0).astype(BF16)
    run = jnp.zeros((tq, 1), F32)
    for c in range(s_eff // LANES):
        sl = slice(c * LANES, (c + 1) * LANES)
        key_c = key_ref[:, sl]
        tie = jnp.where(key_c == thr, 1.0, 0.0)
        rank = run + jnp.dot(tie.astype(BF16), before, preferred_element_type=F32)
        keep_tie = jnp.where(rank < need, tie, 0.0)
        keep = jnp.where(key_c > thr, 1.0, keep_tie)
        adm = (c * LANES + lax.broadcasted_iota(I32, (tq, LANES), 1)) <= q_pos
        bias_ref[:, sl] = jnp.where(adm, jnp.where(keep > 0.0, 0.0, -jnp.inf), -jnp.inf)
        run = run + jnp.sum(tie, axis=1, keepdims=True)

    outs = []
    for hk in range(N_KV_HEADS):
        k_h = k_ref[hk]
        v_h = v_ref[hk]
        for g in range(GROUP):
            logits = _nt_dot(q_ref[hk * GROUP + g], k_h, preferred_element_type=F32) + bias_ref[...]
            m = jnp.max(logits, axis=1, keepdims=True)
            p = jnp.exp(logits - m)
            den = jnp.sum(p, axis=1, keepdims=True)
            o = jnp.dot(p.astype(BF16), v_h, preferred_element_type=F32)
            outs.append(o / den)
    o_all = jnp.concatenate(outs, axis=1)
    out_ref[...] = x_ref[...] + jnp.dot(o_all.astype(BF16), wo_ref[...], preferred_element_type=F32)


def _dsa_prompt_call(x_all, qi, wi, ki, q, k, v, w_out, *, n_batch, seq, s_eff, q_block0, n_qblocks, n_keep):
    tq = DSA_TQ
    per_seq = seq // tq
    q_rows = lambda b, j: b * per_seq + q_block0 + j
    keys = lambda n_heads: pl.BlockSpec((pl.Element(n_heads), pl.Element(s_eff), pl.Element(HEAD_DIM)),
                                        lambda b, j: (0, b * seq, 0), pipeline_mode=pl.Buffered(1))
    body = functools.partial(_dsa_prompt_kernel, s_eff=s_eff, q_block0=q_block0, n_keep=n_keep)
    return pl.pallas_call(
        body,
        grid=(n_batch, n_qblocks),
        in_specs=[
            pl.BlockSpec((IDX_HEADS, tq, IDX_DIM), lambda b, j: (0, q_rows(b, j), 0)),
            pl.BlockSpec((tq, IDX_HEADS), lambda b, j: (q_rows(b, j), 0)),
            pl.BlockSpec((pl.Element(s_eff), pl.Element(IDX_DIM)), lambda b, j: (b * seq, 0),
                         pipeline_mode=pl.Buffered(1)),
            pl.BlockSpec((N_HEADS, tq, HEAD_DIM), lambda b, j: (0, q_rows(b, j), 0)),
            keys(N_KV_HEADS),
            keys(N_KV_HEADS),
            pl.BlockSpec((tq, D_MODEL), lambda b, j: (q_rows(b, j), 0)),
            _full_spec((D_MODEL, D_MODEL), pipeline_mode=pl.Buffered(1)),
        ],
        out_specs=pl.BlockSpec((tq, D_MODEL), lambda b, j: (q_rows(b, j), 0)),
        out_shape=jax.ShapeDtypeStruct(x_all.shape, F32),
        input_output_aliases={6: 0},
        scratch_shapes=[pltpu.VMEM((tq, s_eff), I32), pltpu.VMEM((tq, s_eff), F32)],
        compiler_params=pltpu.CompilerParams(
            dimension_semantics=("arbitrary", "arbitrary"), vmem_limit_bytes=VMEM_LIMIT),
        name=f"dsa_prompt_{s_eff}",
    )(qi, wi, ki, q, k, v, x_all, w_out)


def _dsa_prompt(x_all, proj, w_out, n_batch, seq, class_rows=DSA_CLASS):
    q, k, v, qi, ki, wi = proj
    n_keep = min(TOPK_MAX, seq // 4)
    wo_b = w_out.astype(BF16)
    per_call = class_rows // DSA_TQ
    for c in range(seq // class_rows):
        x_all = _dsa_prompt_call(x_all, qi, wi, ki, q, k, v, wo_b, n_batch=n_batch, seq=seq,
                                 s_eff=(c + 1) * class_rows, q_block0=c * per_call, n_qblocks=per_call,
                                 n_keep=n_keep)
    return x_all


PAGE_GROUP = 8


def _dsa_sample_kernel(pt_ref, *refs, n_keep, n_pages):
    del pt_ref
    pg = PAGE_GROUP
    ki_pages, k_pages, v_pages = refs[:pg], refs[pg:2 * pg], refs[2 * pg:3 * pg]
    (qi_ref, wi_ref, kin_ref, q_ref, kn_ref, vn_ref, x_ref, wo_ref, out_ref,
     score_ref, bias_ref, bnew_ref, m_ref, l_ref, acc_ref) = refs[3 * pg:]
    phase = pl.program_id(1)
    j = pl.program_id(2)
    last = pl.num_programs(2) - 1
    total = lambda a: jnp.sum(jnp.sum(a, axis=1, keepdims=True), axis=0, keepdims=True)
    head_group = lax.broadcasted_iota(I32, (N_HEADS, 1), 0) // GROUP

    def weigh(s):
        return jnp.sum(wi_ref[...] * jnp.maximum(s, 0.0), axis=0, keepdims=True) * IDX_SCALE

    def select():
        kin = kin_ref[...].astype(BF16).astype(F32)
        s_new = weigh(jnp.sum(qi_ref[...].astype(F32) * kin, axis=1, keepdims=True))
        key = _sortable_key(score_ref[...])
        key_new = _sortable_key(s_new)

        def search(it, thr):
            cand = thr + jnp.left_shift(jnp.int32(1), 31 - it)
            cnt = total(jnp.where(key >= cand, 1, 0)) + jnp.where(key_new >= cand, 1, 0)
            return jnp.where(cnt >= n_keep, cand, thr)

        thr = lax.fori_loop(0, 32, search, jnp.full((1, 1), INT32_MIN, I32))
        n_gt = total(jnp.where(key > thr, 1, 0)) + jnp.where(key_new > thr, 1, 0)
        need = (n_keep - n_gt).astype(F32)
        tie = jnp.where(key == thr, 1.0, 0.0)
        r_i = lax.broadcasted_iota(I32, (LANES, LANES), 0)
        c_i = lax.broadcasted_iota(I32, (LANES, LANES), 1)
        before = jnp.where(r_i < c_i, 1.0, 0.0).astype(BF16)
        rr = lax.broadcasted_iota(I32, (n_pages, n_pages), 0)
        cc = lax.broadcasted_iota(I32, (n_pages, n_pages), 1)
        rows_above = jnp.where(cc < rr, 1.0, 0.0).astype(BF16)
        tie_b = tie.astype(BF16)
        rank = (jnp.dot(tie_b, before, preferred_element_type=F32)
                + jnp.sum(jnp.dot(rows_above, tie_b, preferred_element_type=F32), axis=1, keepdims=True))
        keep = jnp.where(key > thr, 1.0, jnp.where(rank < need, tie, 0.0))
        bias_ref[...] = jnp.where(keep > 0.0, 0.0, -jnp.inf)
        keep_new = jnp.where(key_new > thr, 1.0,
                             jnp.where(key_new == thr, jnp.where(total(tie) < need, 1.0, 0.0), 0.0))
        bnew_ref[...] = jnp.where(keep_new > 0.0, 0.0, -jnp.inf)

    @pl.when(phase == 0)
    def _():
        for i in range(pg):
            s = jnp.dot(qi_ref[...], ki_pages[i][...].astype(BF16), preferred_element_type=F32)
            score_ref[pl.ds(j * pg + i, 1), :] = weigh(s)
        pl.when(j == last)(select)

    def flash_update(logits, pv_of):
        m_old = m_ref[...]
        m_new = jnp.maximum(m_old, jnp.max(logits, axis=1, keepdims=True))
        m_safe = jnp.where(m_new == -jnp.inf, 0.0, m_new)
        alpha = jnp.exp(m_old - m_safe)
        p = jnp.exp(logits - m_safe)
        m_ref[...] = m_new
        l_ref[...] = alpha * l_ref[...] + jnp.sum(p, axis=1, keepdims=True)
        acc_ref[...] = alpha * acc_ref[...] + pv_of(p)

    def by_group(per_kv_head):
        out = per_kv_head(0)
        for hk in range(1, N_KV_HEADS):
            out = jnp.where(head_group == hk, per_kv_head(hk), out)
        return out

    @pl.when(phase == 1)
    def _():
        @pl.when(j == 0)
        def _():
            m_ref[...] = jnp.full(m_ref.shape, -jnp.inf, F32)
            l_ref[...] = jnp.zeros(l_ref.shape, F32)
            acc_ref[...] = jnp.zeros(acc_ref.shape, F32)

        q = q_ref[...]
        logits = jnp.concatenate(
            [by_group(lambda hk: jnp.dot(q, k_pages[i][hk].astype(BF16), preferred_element_type=F32))
             + bias_ref[pl.ds(j * pg + i, 1), :] for i in range(pg)], axis=1)

        def pv_of(p):
            p = p.astype(BF16)
            page = lambda i: p[:, i * PAGE_SIZE:(i + 1) * PAGE_SIZE]
            return by_group(lambda hk: sum(
                _nt_dot(page(i), v_pages[i][hk].astype(BF16), preferred_element_type=F32) for i in range(pg)))

        flash_update(logits, pv_of)

        @pl.when(j == last)
        def _():
            rows = lambda ref: by_group(lambda hk: ref[hk:hk + 1, :]).astype(BF16).astype(F32)
            logit_new = jnp.sum(q.astype(F32) * rows(kn_ref), axis=1, keepdims=True) + bnew_ref[...]
            flash_update(logit_new, lambda p: p.astype(BF16).astype(F32) * rows(vn_ref))
            o = (acc_ref[...] / l_ref[...]).astype(BF16)
            y = x_ref[...]
            for h in range(N_HEADS):
                o_h = jnp.broadcast_to(o[h:h + 1, :], (2 * SUBLANES, HEAD_DIM))
                y = y + jnp.dot(o_h, wo_ref[h], preferred_element_type=F32)[0:1]
            out_ref[...] = y


def _dsa_sample(x_s, q_s, qi_s, wi_s, ki_new, k_new, v_new, cache_k, cache_v, cache_idx_k, page_table, w_out):
    n_dec = x_s.shape[0]
    n_pages = page_table.shape[1]
    n_pool = cache_k.shape[0]
    pg = PAGE_GROUP
    n_groups = n_pages // pg
    n_keep = min(TOPK_MAX, (n_pages * PAGE_SIZE + 1) // 4)
    def page_spec(page_shape, i, live_phase):
        def index(b, ph, j, pt):
            idle = i if live_phase == 1 else (n_groups - 1) * pg + i
            return (pt[b, jnp.where(ph == live_phase, j * pg + i, idle)],) + (0,) * len(page_shape)
        return pl.BlockSpec((None,) + page_shape, index)

    idx_page = (IDX_DIM, PAGE_SIZE)
    kv_page = (N_KV_HEADS, HEAD_DIM, PAGE_SIZE)
    cik = cache_idx_k.transpose(0, 2, 1)
    ck = cache_k.transpose(0, 2, 3, 1)
    cv = cache_v.transpose(0, 2, 3, 1)

    per_seq = lambda *shape: pl.BlockSpec((None,) + shape, lambda b, ph, j, pt: (b,) + (0,) * len(shape))
    grid_spec = pltpu.PrefetchScalarGridSpec(
        num_scalar_prefetch=1,
        grid=(n_dec, 2, n_groups),
        in_specs=([page_spec(idx_page, i, 0) for i in range(pg)]
                  + [page_spec(kv_page, i, 1) for i in range(pg)]
                  + [page_spec(kv_page, i, 1) for i in range(pg)]
                  + [per_seq(IDX_HEADS, IDX_DIM), per_seq(IDX_HEADS, 1), per_seq(1, IDX_DIM),
                     per_seq(N_HEADS, HEAD_DIM), per_seq(N_KV_HEADS, HEAD_DIM), per_seq(N_KV_HEADS, HEAD_DIM),
                     per_seq(1, D_MODEL),
                     pl.BlockSpec((N_HEADS, HEAD_DIM, D_MODEL), lambda b, ph, j, pt: (0, 0, 0))]),
        out_specs=per_seq(1, D_MODEL),
        scratch_shapes=[pltpu.VMEM((n_pages, PAGE_SIZE), F32), pltpu.VMEM((n_pages, PAGE_SIZE), F32),
                        pltpu.VMEM((1, 1), F32), pltpu.VMEM((N_HEADS, 1), F32), pltpu.VMEM((N_HEADS, 1), F32),
                        pltpu.VMEM((N_HEADS, HEAD_DIM), F32)],
    )
    out = pl.pallas_call(
        functools.partial(_dsa_sample_kernel, n_keep=n_keep, n_pages=n_pages),
        grid_spec=grid_spec,
        out_shape=jax.ShapeDtypeStruct((n_dec, 1, D_MODEL), F32),
        compiler_params=pltpu.CompilerParams(
            dimension_semantics=("arbitrary", "arbitrary", "arbitrary"), vmem_limit_bytes=VMEM_LIMIT),
        name="dsa_sample",
    )(page_table, *([cik] * pg), *([ck] * pg), *([cv] * pg),
      qi_s, wi_s.reshape(n_dec, IDX_HEADS, 1), ki_new.reshape(n_dec, 1, IDX_DIM), q_s,
      k_new.reshape(n_dec, N_KV_HEADS, HEAD_DIM), v_new.reshape(n_dec, N_KV_HEADS, HEAD_DIM),
      x_s.reshape(n_dec, 1, D_MODEL), w_out.astype(BF16).reshape(N_HEADS, HEAD_DIM, D_MODEL))
    return out.reshape(n_dec, D_MODEL)


def kernel(x_prompt, x_sample, cache_k, cache_v, cache_idx_k, page_table, norm_mix_g, norm_ffn_g,
           gmlp_w_in, gmlp_v_g, gmlp_w_s, gmlp_b_s, gmlp_w_out,
           att_w_in, att_q_norm_g, att_k_norm_g, att_idx_k_norm_g, att_w_out,
           peer_w_q, peer_sub_k1, peer_sub_k2, peer_u, peer_v):
    n_batch, seq, _ = x_prompt.shape
    n_dec, dec_seq, _ = x_sample.shape
    assert dec_seq == 1 and seq % DSA_CLASS == 0 and (n_batch * seq) % ROW_BLOCK == 0
    n_prompt = n_batch * seq
    n_tok = n_prompt + n_dec
    n_rows = n_prompt + ROW_BLOCK
    assert n_dec <= ROW_BLOCK

    def peer(x, layer):
        return _peer_layer(x, norm_ffn_g[layer], peer_w_q[layer], peer_sub_k1[layer], peer_sub_k2[layer],
                           peer_u[layer], peer_v[layer])

    x_all = jnp.concatenate([x_prompt.reshape(n_prompt, D_MODEL), x_sample.reshape(n_dec, D_MODEL),
                             jnp.zeros((n_rows - n_tok, D_MODEL), F32)], axis=0)
    x_all, v_new = _gmlp_layer(x_all, n_prompt, norm_mix_g[0], gmlp_w_in[0], gmlp_v_g[0], gmlp_w_s[0],
                               gmlp_b_s[0], gmlp_w_out[0])
    x_all = peer(x_all, 0)
    pos = jnp.concatenate([jnp.tile(jnp.arange(seq), n_batch), jnp.full((n_rows - n_prompt,), PAST_LEN)])
    q, k, v, qi, ki, wi, new_k, new_v, new_ki = _att_project(
        x_all, pos, norm_mix_g[1], att_w_in[0], att_q_norm_g[0], att_k_norm_g[0], att_idx_k_norm_g[0])
    smp = slice(n_prompt, n_tok)
    heads = lambda a: a[:, smp].transpose(1, 0, 2)
    xs = _dsa_sample(x_all[smp], heads(q), heads(qi), wi[smp], new_ki[smp], new_k[smp], new_v[smp],
                     cache_k[0], cache_v[0], cache_idx_k[0], page_table, att_w_out[0])
    x_all = _dsa_prompt(x_all, (q, k, v, qi, ki, wi), att_w_out[0], n_batch, seq)
    x_all = lax.dynamic_update_slice(x_all, xs, (n_prompt, 0))
    x_all = peer(x_all, 1)

    kv_shape = (1, n_batch, seq, N_KV_HEADS, HEAD_DIM)
    kv_s_shape = (1, n_dec, 1, N_KV_HEADS, HEAD_DIM)
    return (x_all[:n_prompt].reshape(n_batch, seq, D_MODEL), x_all[smp].reshape(n_dec, 1, D_MODEL),
            new_k[:n_prompt].reshape(kv_shape), new_v[:n_prompt].reshape(kv_shape),
            new_ki[:n_prompt].reshape(1, n_batch, seq, IDX_DIM),
            new_k[smp].reshape(kv_s_shape), new_v[smp].reshape(kv_s_shape),
            new_ki[smp].reshape(1, n_dec, 1, IDX_DIM), v_new[:n_dec].reshape(1, n_dec, 1, GMLP_WIDTH))
```

```python
import functools
import math

import jax
import jax.numpy as jnp
import numpy as np
from jax import lax
from jax.experimental import pallas as pl
from jax.experimental.pallas import tpu as pltpu

F32 = jnp.float32
BF16 = jnp.bfloat16
I32 = jnp.int32

D_MODEL = 1024
EPS = 1e-6

LANES = 128
SUBLANES = 8
MIB = 1024 * 1024
VMEM_LIMIT = 56 * MIB

ROW_BLOCK = 256

PEER_HEADS = 8
N_KEYS = 128
N_EXPERTS = N_KEYS * N_KEYS
PEER_TOPK = 16
PEER_KEY_DIM = 256
PEER_HALF = PEER_KEY_DIM // 2
PEER_PICKS = PEER_HEADS * PEER_TOPK
ROW_WORDS = D_MODEL // 2
ROW_SUB = ROW_WORDS // LANES
PEER_TOKENS = 128
N_CAND = sum(PEER_TOPK // (a + 1) for a in range(PEER_TOPK))

CHUNK = 128
GMLP_WIDTH = 2 * D_MODEL
GMLP_GROUPS = 8
GMLP_GROUP_DIM = GMLP_WIDTH // GMLP_GROUPS

PAST_LEN = 16384
PAGE_SIZE = 128
N_HEADS = 16
HEAD_DIM = D_MODEL // N_HEADS
N_KV_HEADS = 4
GROUP = N_HEADS // N_KV_HEADS
IDX_HEADS = 8
IDX_DIM = 64
TOPK_MAX = 256
ROPE_THETA = 10000.0
IDX_SCALE = (IDX_DIM ** -0.5) * (IDX_HEADS ** -0.5)
ATT_SIZES = (N_HEADS * HEAD_DIM, N_KV_HEADS * HEAD_DIM, N_KV_HEADS * HEAD_DIM,
             IDX_HEADS * IDX_DIM, IDX_DIM, IDX_HEADS)
ATT_COLS = sum(ATT_SIZES)
ATT_COLS_PAD = -(-ATT_COLS // LANES) * LANES
Q_COL, K_COL, V_COL, QI_COL, KI_COL = (int(c) for c in np.cumsum((0,) + ATT_SIZES[:4]))


def _nt_dot(a, b, **kw):
    return lax.dot_general(a, b, (((1,), (1,)), ((), ())), **kw)


def _rms_rows(x, g):
    return x * lax.rsqrt(jnp.mean(x * x, axis=-1, keepdims=True) + EPS) * g


def _full_spec(shape, **kw):
    return pl.BlockSpec(shape, lambda *_: (0,) * len(shape), **kw)


def _topk_rows(s, k, iota):
    n_rows = s.shape[0]
    vals, idxs = [], []
    for _ in range(k):
        m = jnp.max(s, axis=0, keepdims=True)
        first = jnp.min(jnp.where(s == m, iota, n_rows), axis=0, keepdims=True)
        vals.append(m)
        idxs.append(first)
        s = jnp.where(iota == first, -jnp.inf, s)
    return jnp.concatenate(vals, axis=0), jnp.concatenate(idxs, axis=0)


def _peer_router_kernel(x_ref, g_ref, wq_ref, k1_ref, k2_ref, h_ref, eidx_ref, gate_ref):
    h = _rms_rows(x_ref[...], g_ref[...])
    h_ref[...] = h
    q = jnp.dot(h.astype(BF16), wq_ref[...], preferred_element_type=F32)
    n_tok = h.shape[0]
    iota_k = lax.broadcasted_iota(I32, (N_KEYS, n_tok), 0)
    iota_c = lax.broadcasted_iota(I32, (N_CAND, n_tok), 0)
    k1 = k1_ref[...]
    k2 = k2_ref[...]
    eidx_rows, gate_rows = [], []
    for head in range(PEER_HEADS):
        base = head * PEER_KEY_DIM
        q1 = q[:, base:base + PEER_HALF].astype(BF16)
        q2 = q[:, base + PEER_HALF:base + PEER_KEY_DIM].astype(BF16)
        s1 = _nt_dot(k1, q1, preferred_element_type=F32)
        s2 = _nt_dot(k2, q2, preferred_element_type=F32)
        v1, i1 = _topk_rows(s1, PEER_TOPK, iota_k)
        v2, i2 = _topk_rows(s2, PEER_TOPK, iota_k)
        width = [PEER_TOPK // (a + 1) for a in range(PEER_TOPK)]
        cand = jnp.concatenate([v1[a:a + 1] + v2[:width[a]] for a in range(PEER_TOPK)], axis=0)
        cidx = jnp.concatenate([(i1[a:a + 1] * N_KEYS + i2[:width[a]]) * ROW_SUB for a in range(PEER_TOPK)], axis=0)
        top_s, top_e = [], []
        for _ in range(PEER_TOPK):
            m = jnp.max(cand, axis=0, keepdims=True)
            first = jnp.min(jnp.where(cand == m, iota_c, N_CAND), axis=0, keepdims=True)
            hit = iota_c == first
            top_s.append(m)
            top_e.append(jnp.sum(jnp.where(hit, cidx, 0), axis=0, keepdims=True))
            cand = jnp.where(hit, -jnp.inf, cand)
        top_s = jnp.concatenate(top_s, axis=0)
        e = jnp.exp(top_s - top_s[0:1])
        gate_rows.append(e / jnp.sum(e, axis=0, keepdims=True))
        eidx_rows.append(jnp.concatenate(top_e, axis=0))
    gate_t = jnp.concatenate(gate_rows, axis=0)
    eidx_t = jnp.concatenate(eidx_rows, axis=0)
    gate_ref[...] = gate_t.T
    eidx_ref[...] = lax.bitcast_convert_type(lax.bitcast_convert_type(eidx_t, F32).T, I32)


def _peer_router(x, g, w_q, sub_k1, sub_k2):
    n = x.shape[0]
    tm = PEER_TOKENS
    return pl.pallas_call(
        _peer_router_kernel,
        grid=(n // tm,),
        in_specs=[
            pl.BlockSpec((tm, D_MODEL), lambda i: (i, 0)),
            _full_spec((1, D_MODEL)),
            _full_spec((D_MODEL, PEER_HEADS * PEER_KEY_DIM)),
            _full_spec((N_KEYS, PEER_HALF)),
            _full_spec((N_KEYS, PEER_HALF)),
        ],
        out_specs=[
            pl.BlockSpec((tm, D_MODEL), lambda i: (i, 0)),
            pl.BlockSpec((tm, PEER_PICKS), lambda i: (i, 0)),
            pl.BlockSpec((tm, PEER_PICKS), lambda i: (i, 0)),
        ],
        out_shape=[
            jax.ShapeDtypeStruct((n, D_MODEL), F32),
            jax.ShapeDtypeStruct((n, PEER_PICKS), I32),
            jax.ShapeDtypeStruct((n, PEER_PICKS), F32),
        ],
        compiler_params=pltpu.CompilerParams(dimension_semantics=("arbitrary",), vmem_limit_bytes=VMEM_LIMIT),
        name="peer_router",
    )(x, g.reshape(1, D_MODEL), w_q.astype(BF16), sub_k1.astype(BF16), sub_k2.astype(BF16))


PACK_ROWS = 512


def _pack_kernel(w_ref, out_ref):
    w = w_ref[...]
    as_bits = lambda a: lax.bitcast_convert_type(a.astype(jnp.bfloat16).astype(F32), I32)
    hi = as_bits(w[:, :ROW_WORDS]) & jnp.int32(-65536)
    lo = (as_bits(w[:, ROW_WORDS:]) >> 16) & jnp.int32(0xFFFF)
    out_ref[...] = hi | lo


def _pack_table(w):
    words = pl.pallas_call(
        _pack_kernel,
        grid=(N_EXPERTS // PACK_ROWS,),
        in_specs=[pl.BlockSpec((PACK_ROWS, D_MODEL), lambda i: (i, 0))],
        out_specs=pl.BlockSpec((PACK_ROWS, ROW_WORDS), lambda i: (i, 0)),
        out_shape=jax.ShapeDtypeStruct((N_EXPERTS, ROW_WORDS), I32),
        compiler_params=pltpu.CompilerParams(dimension_semantics=("arbitrary",)),
        name="peer_pack_table",
    )(w)
    return words.reshape(N_EXPERTS * ROW_SUB, LANES)


PLANE_STRIDE = PEER_PICKS + SUBLANES
GATHER_ROWS = ROW_SUB * PLANE_STRIDE
TOKEN_GROUP = 2
N_GATHER_BUFS = 2 * TOKEN_GROUP


def _gather_token(eidx_ref, tab_ref, buf_ref, t):
    for p in range(PEER_PICKS):
        row0 = pl.multiple_of(eidx_ref[t, p], ROW_SUB)
        buf_ref[pl.ds(p, ROW_SUB, stride=PLANE_STRIDE), :] = tab_ref[pl.ds(row0, ROW_SUB), :]


def _gathered_rows(buf_ref):
    hi, lo = [], []
    for s in range(ROW_SUB):
        w = buf_ref[s * PLANE_STRIDE:s * PLANE_STRIDE + PEER_PICKS, :]
        hi.append(lax.bitcast_convert_type(w & jnp.int32(-65536), F32).astype(BF16))
        lo.append(lax.bitcast_convert_type(w << 16, F32).astype(BF16))
    return jnp.concatenate(hi + lo, axis=1)


def _token_pipeline(eidx_ref, tab_ref, bufs, n_tok, compute):
    g = TOKEN_GROUP
    groups = (bufs[:g], bufs[g:])
    for k, buf in enumerate(bufs):
        _gather_token(eidx_ref, tab_ref, buf, k)

    def step(i, carry):
        for j, group in enumerate(groups):
            t = 2 * g * i + g * j
            compute(t, group)
            for k, buf in enumerate(group):
                _gather_token(eidx_ref, tab_ref, buf, jnp.minimum(t + 2 * g + k, n_tok - 1))
        return carry

    lax.fori_loop(0, n_tok // (2 * g), step, 0)


def _group_rows(group):
    return jnp.concatenate([_gathered_rows(buf) for buf in group], axis=0)


def _token_rows(ref, t):
    return [ref[pl.ds(t + k, 1), :] for k in range(TOKEN_GROUP)]


def _own_block(n_rows):
    r = lax.broadcasted_iota(I32, (n_rows, TOKEN_GROUP * PEER_PICKS), 0)
    c = lax.broadcasted_iota(I32, (n_rows, TOKEN_GROUP * PEER_PICKS), 1)
    return r == c // PEER_PICKS


def _peer_act_kernel(eidx_ref, tab_ref, h_ref, gate_ref, coef_ref, *bufs):
    g = TOKEN_GROUP
    pad = jnp.zeros((2 * SUBLANES - g, D_MODEL), F32)

    def compute(t, group):
        x = jnp.concatenate(_token_rows(h_ref, t) + [pad], axis=0).astype(BF16)
        dots = _nt_dot(x, _group_rows(group), preferred_element_type=F32)
        act = jnp.concatenate([dots[k:k + 1, k * PEER_PICKS:(k + 1) * PEER_PICKS] for k in range(g)], axis=0)
        coef = jnp.concatenate(_token_rows(gate_ref, t), axis=0) * jax.nn.gelu(act)
        for k in range(g):
            coef_ref[pl.ds(t + k, 1), :] = coef[k:k + 1]

    _token_pipeline(eidx_ref, tab_ref, bufs, h_ref.shape[0], compute)


def _peer_out_kernel(eidx_ref, tab_ref, coef_ref, x_ref, out_ref, *bufs):
    g = TOKEN_GROUP
    own = _own_block(g)
    pad = jnp.zeros((2 * SUBLANES - g, g * PEER_PICKS), F32)

    def compute(t, group):
        c = jnp.concatenate(_token_rows(coef_ref, t), axis=0)
        lhs = jnp.concatenate([jnp.where(own, jnp.concatenate([c] * g, axis=1), 0.0), pad], axis=0)
        y = jnp.dot(lhs.astype(BF16), _group_rows(group), preferred_element_type=F32)
        for k in range(g):
            out_ref[pl.ds(t + k, 1), :] = x_ref[pl.ds(t + k, 1), :] + y[k:k + 1]

    _token_pipeline(eidx_ref, tab_ref, bufs, x_ref.shape[0], compute)


def _peer_gather_call(body, name, eidx, tab, a, b, out_cols):
    n = eidx.shape[0]
    tb = ROW_BLOCK
    rows = lambda cols: pl.BlockSpec((tb, cols), lambda i: (i, 0))
    return pl.pallas_call(
        body,
        grid=(n // tb,),
        in_specs=[
            pl.BlockSpec((tb, PEER_PICKS), lambda i: (i, 0), memory_space=pltpu.SMEM),
            _full_spec((N_EXPERTS * ROW_SUB, LANES), pipeline_mode=pl.Buffered(1)),
            rows(a.shape[1]),
            rows(b.shape[1]),
        ],
        out_specs=rows(out_cols),
        out_shape=jax.ShapeDtypeStruct((n, out_cols), F32),
        scratch_shapes=[pltpu.VMEM((GATHER_ROWS, LANES), I32) for _ in range(N_GATHER_BUFS)],
        compiler_params=pltpu.CompilerParams(dimension_semantics=("arbitrary",), vmem_limit_bytes=VMEM_LIMIT),
        name=name,
    )(eidx, tab, a, b)


def _peer_layer(x, g, w_q, sub_k1, sub_k2, expert_u, expert_v):
    h, eidx, gate = _peer_router(x, g, w_q, sub_k1, sub_k2)
    coef = _peer_gather_call(_peer_act_kernel, "peer_act", eidx, _pack_table(expert_u), h, gate, PEER_PICKS)
    return _peer_gather_call(_peer_out_kernel, "peer_out", eidx, _pack_table(expert_v), coef, x, D_MODEL)


def _gmlp_kernel(x_ref, g_ref, win_ref, vg_ref, ws_ref, bst_ref, s0_ref, b0_ref, wout_ref,
                 out_ref, vnew_ref, gated_ref, *, n_chunk_blocks):
    x = x_ref[...]
    h = _rms_rows(x, g_ref[...])
    uv = jnp.dot(h.astype(BF16), win_ref[...], preferred_element_type=F32)
    v = _rms_rows(uv[:, GMLP_WIDTH:], vg_ref[...])
    is_chunked = pl.program_id(0) < n_chunk_blocks

    @pl.when(is_chunked)
    def _():
        r_i = lax.broadcasted_iota(I32, (CHUNK, CHUNK), 0)
        c_i = lax.broadcasted_iota(I32, (CHUNK, CHUNK), 1)
        causal = c_i <= r_i
        for g in range(GMLP_GROUPS):
            cols = slice(g * GMLP_GROUP_DIM, (g + 1) * GMLP_GROUP_DIM)
            ws = jnp.where(causal, ws_ref[g], 0.0).astype(BF16)
            bias = bst_ref[:, g:g + 1]
            for c in range(x.shape[0] // CHUNK):
                rows = slice(c * CHUNK, (c + 1) * CHUNK)
                mixed = jnp.dot(ws, v[rows, cols].astype(BF16), preferred_element_type=F32) + bias
                gated_ref[rows, cols] = (uv[rows, cols] * mixed).astype(BF16)

    @pl.when(jnp.logical_not(is_chunked))
    def _():
        gated_ref[...] = (uv[:, :GMLP_WIDTH] * (v * s0_ref[...] + b0_ref[...])).astype(BF16)
        vnew_ref[...] = v

    out_ref[...] = x + jnp.dot(gated_ref[...], wout_ref[...], preferred_element_type=F32)


def _gmlp_layer(x_all, n_chunk_rows, g, w_in, v_g, w_s, b_s, w_out):
    n = x_all.shape[0]
    tb = ROW_BLOCK
    spread = lambda a: jnp.repeat(a, GMLP_GROUP_DIM).reshape(1, GMLP_WIDTH)
    body = functools.partial(_gmlp_kernel, n_chunk_blocks=n_chunk_rows // tb)
    return pl.pallas_call(
        body,
        grid=(n // tb,),
        in_specs=[
            pl.BlockSpec((tb, D_MODEL), lambda i: (i, 0)),
            _full_spec((1, D_MODEL)),
            _full_spec((D_MODEL, 2 * GMLP_WIDTH), pipeline_mode=pl.Buffered(1)),
            _full_spec((1, GMLP_WIDTH)),
            _full_spec((GMLP_GROUPS, CHUNK, CHUNK)),
            _full_spec((CHUNK, GMLP_GROUPS)),
            _full_spec((1, GMLP_WIDTH)),
            _full_spec((1, GMLP_WIDTH)),
            _full_spec((GMLP_WIDTH, D_MODEL), pipeline_mode=pl.Buffered(1)),
        ],
        out_specs=[
            pl.BlockSpec((tb, D_MODEL), lambda i: (i, 0)),
            _full_spec((tb, GMLP_WIDTH)),
        ],
        out_shape=[
            jax.ShapeDtypeStruct((n, D_MODEL), F32),
            jax.ShapeDtypeStruct((tb, GMLP_WIDTH), F32),
        ],
        scratch_shapes=[pltpu.VMEM((tb, GMLP_WIDTH), BF16)],
        compiler_params=pltpu.CompilerParams(dimension_semantics=("arbitrary",), vmem_limit_bytes=VMEM_LIMIT),
        name="gmlp_mixer",
    )(x_all, g.reshape(1, D_MODEL), w_in.astype(BF16), v_g.reshape(1, GMLP_WIDTH), w_s, b_s.T,
      spread(w_s[:, 0, 0]), spread(b_s[:, 0]), w_out.astype(BF16))


def _att_proj_kernel(x_ref, g_ref, w_ref, qg_ref, kg_ref, ikg_ref, cos_ref, sin_ref,
                     q_ref, k_ref, v_ref, qi_ref, ki_ref, wi_ref, nk_ref, nv_ref, nki_ref):
    h = _rms_rows(x_ref[...], g_ref[...])
    p = jnp.dot(h.astype(BF16), w_ref[...], preferred_element_type=F32)
    n_tok = p.shape[0]
    cos = cos_ref[...]
    sin = sin_ref[...]
    lane = lax.broadcasted_iota(I32, (n_tok, LANES), 1)
    first_half = (lane & (HEAD_DIM // 2)) == 0
    r_i = lax.broadcasted_iota(I32, (LANES, LANES), 0)
    c_i = lax.broadcasted_iota(I32, (LANES, LANES), 1)
    head_mean = jnp.where((r_i // HEAD_DIM) == (c_i // HEAD_DIM), 1.0 / HEAD_DIM, 0.0).astype(BF16)

    def slab(col):
        return p[:, col:col + LANES]

    def head_rms(a, gain):
        ms = jnp.dot((a * a).astype(BF16), head_mean, preferred_element_type=F32)
        return a * lax.rsqrt(ms + EPS) * gain

    def rope(a):
        partner = jnp.where(first_half, pltpu.roll(a, LANES - HEAD_DIM // 2, 1), pltpu.roll(a, HEAD_DIM // 2, 1))
        return a * cos + partner * sin

    def put_heads(ref, i, a):
        ref[2 * i] = a[:, :HEAD_DIM].astype(BF16)
        ref[2 * i + 1] = a[:, HEAD_DIM:].astype(BF16)

    for i in range(N_HEADS // 2):
        put_heads(q_ref, i, rope(head_rms(slab(Q_COL + i * LANES), qg_ref[...])) * (HEAD_DIM ** -0.5))
    for i in range(N_KV_HEADS // 2):
        k = rope(head_rms(slab(K_COL + i * LANES), kg_ref[...]))
        nk_ref[:, i * LANES:(i + 1) * LANES] = k
        put_heads(k_ref, i, k)
        v = slab(V_COL + i * LANES)
        nv_ref[:, i * LANES:(i + 1) * LANES] = v
        put_heads(v_ref, i, v)
    for i in range(IDX_HEADS // 2):
        put_heads(qi_ref, i, rope(slab(QI_COL + i * LANES)))
    tail = slab(KI_COL)
    ki = rope(head_rms(tail, ikg_ref[...]))[:, :IDX_DIM]
    nki_ref[...] = ki
    ki_ref[...] = ki.astype(BF16)
    wi_ref[...] = tail[:, IDX_DIM:IDX_DIM + IDX_HEADS]


def _att_project(x_all, pos, g, w_in, qn_g, kn_g, ikn_g):
    n = x_all.shape[0]
    tb = ROW_BLOCK
    half = HEAD_DIM // 2
    inv = jnp.exp(-math.log(ROPE_THETA) * jnp.arange(half, dtype=F32) / half)
    ang = pos.astype(F32)[:, None] * inv[None, :]
    cos = jnp.tile(jnp.cos(ang), (1, LANES // half))
    sin = jnp.tile(jnp.concatenate([-jnp.sin(ang), jnp.sin(ang)], axis=1), (1, LANES // HEAD_DIM))
    pair = lambda gain: jnp.tile(gain, LANES // HEAD_DIM).reshape(1, LANES)
    w_pad = jnp.pad(w_in, ((0, 0), (0, ATT_COLS_PAD - ATT_COLS))).astype(BF16)
    heads = lambda n_heads: (pl.BlockSpec((n_heads, tb, HEAD_DIM), lambda i: (0, i, 0)),
                             jax.ShapeDtypeStruct((n_heads, n, HEAD_DIM), BF16))
    rows = lambda cols, dt: (pl.BlockSpec((tb, cols), lambda i: (i, 0)), jax.ShapeDtypeStruct((n, cols), dt))
    outs = [heads(N_HEADS), heads(N_KV_HEADS), heads(N_KV_HEADS), heads(IDX_HEADS), rows(IDX_DIM, BF16),
            rows(IDX_HEADS, F32), rows(N_KV_HEADS * HEAD_DIM, F32), rows(N_KV_HEADS * HEAD_DIM, F32),
            rows(IDX_DIM, F32)]
    return pl.pallas_call(
        _att_proj_kernel,
        grid=(n // tb,),
        in_specs=[
            pl.BlockSpec((tb, D_MODEL), lambda i: (i, 0)),
            _full_spec((1, D_MODEL)),
            _full_spec((D_MODEL, ATT_COLS_PAD), pipeline_mode=pl.Buffered(1)),
            _full_spec((1, LANES)),
            _full_spec((1, LANES)),
            _full_spec((1, LANES)),
            pl.BlockSpec((tb, LANES), lambda i: (i, 0)),
            pl.BlockSpec((tb, LANES), lambda i: (i, 0)),
        ],
        out_specs=[o[0] for o in outs],
        out_shape=[o[1] for o in outs],
        compiler_params=pltpu.CompilerParams(dimension_semantics=("arbitrary",), vmem_limit_bytes=VMEM_LIMIT),
        name="att_project",
    )(x_all, g.reshape(1, D_MODEL), w_pad, pair(qn_g), pair(kn_g), pair(ikn_g), cos, sin)


DSA_TQ = 256
DSA_CLASS = 512
INT32_MIN = -(2 ** 31)


def _sortable_key(score):
    bits = lax.bitcast_convert_type(score, I32)
    return bits ^ ((bits >> 31) & jnp.int32(0x7FFFFFFF))


def _dsa_prompt_kernel(qi_ref, wi_ref, ki_ref, q_ref, k_ref, v_ref, x_ref, wo_ref, out_ref,
                       key_ref, bias_ref, *, s_eff, q_block0, n_keep):
    tq = DSA_TQ
    q0 = (q_block0 + pl.program_id(1)) * tq
    ki = ki_ref[...]
    wi = wi_ref[...]
    score = jnp.zeros((tq, s_eff), F32)
    for h in range(IDX_HEADS):
        s = _nt_dot(qi_ref[h], ki, preferred_element_type=F32)
        score = score + wi[:, h:h + 1] * jnp.maximum(s, 0.0)
    score = score * IDX_SCALE
    q_pos = q0 + lax.broadcasted_iota(I32, (tq, 1), 0)
    k_pos = lax.broadcasted_iota(I32, (tq, s_eff), 1)
    score = jnp.where(k_pos <= q_pos, score, -jnp.inf)
    key_ref[...] = _sortable_key(score)

    def search(it, thr):
        cand = thr + jnp.left_shift(jnp.int32(1), 31 - it)
        cnt = jnp.sum(jnp.where(key_ref[...] >= cand, 1, 0), axis=1, keepdims=True)
        return jnp.where(cnt >= n_keep, cand, thr)

    thr = lax.fori_loop(0, 32, search, jnp.full((tq, 1), INT32_MIN, I32))
    n_gt = jnp.sum(jnp.where(key_ref[...] > thr, 1, 0), axis=1, keepdims=True)
    need = (n_keep - n_gt).astype(F32)
    r_i = lax.broadcasted_iota(I32, (LANES, LANES), 0)
    c_i = lax.broadcasted_iota(I32, (LANES, LANES), 1)
    before = jnp.where(r_i < c_i, 1.0, 0.0).astype(BF16)
    run = jnp.zeros((tq, 1), F32)
    for c in range(s_eff // LANES):
        sl = slice(c * LANES, (c + 1) * LANES)
        key_c = key_ref[:, sl]
        tie = jnp.where(key_c == thr, 1.0, 0.0)
        rank = run + jnp.dot(tie.astype(BF16), before, preferred_element_type=F32)
        keep_tie = jnp.where(rank < need, tie, 0.0)
        keep = jnp.where(key_c > thr, 1.0, keep_tie)
        adm = (c * LANES + lax.broadcasted_iota(I32, (tq, LANES), 1)) <= q_pos
        bias_ref[:, sl] = jnp.where(adm, jnp.where(keep > 0.0, 0.0, -jnp.inf), -jnp.inf)
        run = run + jnp.sum(tie, axis=1, keepdims=True)

    outs = []
    for hk in range(N_KV_HEADS):
        k_h = k_ref[hk]
        v_h = v_ref[hk]
        for g in range(GROUP):
            logits = _nt_dot(q_ref[hk * GROUP + g], k_h, preferred_element_type=F32) + bias_ref[...]
            m = jnp.max(logits, axis=1, keepdims=True)
            p = jnp.exp(logits - m)
            den = jnp.sum(p, axis=1, keepdims=True)
            o = jnp.dot(p.astype(BF16), v_h, preferred_element_type=F32)
            outs.append(o / den)
    o_all = jnp.concatenate(outs, axis=1)
    out_ref[...] = x_ref[...] + jnp.dot(o_all.astype(BF16), wo_ref[...], preferred_element_type=F32)


def _dsa_prompt_call(x_all, qi, wi, ki, q, k, v, w_out, *, n_batch, seq, s_eff, q_block0, n_qblocks, n_keep):
    tq = DSA_TQ
    per_seq = seq // tq
    q_rows = lambda b, j: b * per_seq + q_block0 + j
    keys = lambda n_heads: pl.BlockSpec((pl.Element(n_heads), pl.Element(s_eff), pl.Element(HEAD_DIM)),
                                        lambda b, j: (0, b * seq, 0), pipeline_mode=pl.Buffered(1))
    body = functools.partial(_dsa_prompt_kernel, s_eff=s_eff, q_block0=q_block0, n_keep=n_keep)
    return pl.pallas_call(
        body,
        grid=(n_batch, n_qblocks),
        in_specs=[
            pl.BlockSpec((IDX_HEADS, tq, IDX_DIM), lambda b, j: (0, q_rows(b, j), 0)),
            pl.BlockSpec((tq, IDX_HEADS), lambda b, j: (q_rows(b, j), 0)),
            pl.BlockSpec((pl.Element(s_eff), pl.Element(IDX_DIM)), lambda b, j: (b * seq, 0),
                         pipeline_mode=pl.Buffered(1)),
            pl.BlockSpec((N_HEADS, tq, HEAD_DIM), lambda b, j: (0, q_rows(b, j), 0)),
            keys(N_KV_HEADS),
            keys(N_KV_HEADS),
            pl.BlockSpec((tq, D_MODEL), lambda b, j: (q_rows(b, j), 0)),
            _full_spec((D_MODEL, D_MODEL), pipeline_mode=pl.Buffered(1)),
        ],
        out_specs=pl.BlockSpec((tq, D_MODEL), lambda b, j: (q_rows(b, j), 0)),
        out_shape=jax.ShapeDtypeStruct(x_all.shape, F32),
        input_output_aliases={6: 0},
        scratch_shapes=[pltpu.VMEM((tq, s_eff), I32), pltpu.VMEM((tq, s_eff), F32)],
        compiler_params=pltpu.CompilerParams(
            dimension_semantics=("arbitrary", "arbitrary"), vmem_limit_bytes=VMEM_LIMIT),
        name=f"dsa_prompt_{s_eff}",
    )(qi, wi, ki, q, k, v, x_all, w_out)


def _dsa_prompt(x_all, proj, w_out, n_batch, seq, class_rows=DSA_CLASS):
    q, k, v, qi, ki, wi = proj
    n_keep = min(TOPK_MAX, seq // 4)
    wo_b = w_out.astype(BF16)
    per_call = class_rows // DSA_TQ
    for c in range(seq // class_rows):
        x_all = _dsa_prompt_call(x_all, qi, wi, ki, q, k, v, wo_b, n_batch=n_batch, seq=seq,
                                 s_eff=(c + 1) * class_rows, q_block0=c * per_call, n_qblocks=per_call,
                                 n_keep=n_keep)
    return x_all


PAGE_GROUP = 16


def _dsa_sample_kernel(pt_ref, *refs, n_keep, n_pages):
    del pt_ref
    pg = PAGE_GROUP
    ki_pages, k_pages, v_pages = refs[:pg], refs[pg:2 * pg], refs[2 * pg:3 * pg]
    (qi_ref, wi_ref, kin_ref, q_ref, kn_ref, vn_ref, x_ref, wo_ref, out_ref,
     score_ref, bias_ref, bnew_ref, m_ref, l_ref, acc_ref) = refs[3 * pg:]
    phase = pl.program_id(1)
    j = pl.program_id(2)
    last = pl.num_programs(2) - 1
    total = lambda a: jnp.sum(jnp.sum(a, axis=1, keepdims=True), axis=0, keepdims=True)
    head_group = lax.broadcasted_iota(I32, (N_HEADS, 1), 0) // GROUP

    def weigh(s):
        return jnp.sum(wi_ref[...] * jnp.maximum(s, 0.0), axis=0, keepdims=True) * IDX_SCALE

    def select():
        kin = kin_ref[...].astype(BF16).astype(F32)
        s_new = weigh(jnp.sum(qi_ref[...].astype(F32) * kin, axis=1, keepdims=True))
        key = _sortable_key(score_ref[...])
        key_new = _sortable_key(s_new)

        def search(it, thr):
            cand = thr + jnp.left_shift(jnp.int32(1), 31 - it)
            cnt = total(jnp.where(key >= cand, 1, 0)) + jnp.where(key_new >= cand, 1, 0)
            return jnp.where(cnt >= n_keep, cand, thr)

        thr = lax.fori_loop(0, 32, search, jnp.full((1, 1), INT32_MIN, I32))
        n_gt = total(jnp.where(key > thr, 1, 0)) + jnp.where(key_new > thr, 1, 0)
        need = (n_keep - n_gt).astype(F32)
        tie = jnp.where(key == thr, 1.0, 0.0)
        r_i = lax.broadcasted_iota(I32, (LANES, LANES), 0)
        c_i = lax.broadcasted_iota(I32, (LANES, LANES), 1)
        before = jnp.where(r_i < c_i, 1.0, 0.0).astype(BF16)
        rr = lax.broadcasted_iota(I32, (n_pages, n_pages), 0)
        cc = lax.broadcasted_iota(I32, (n_pages, n_pages), 1)
        rows_above = jnp.where(cc < rr, 1.0, 0.0).astype(BF16)
        tie_b = tie.astype(BF16)
        rank = (jnp.dot(tie_b, before, preferred_element_type=F32)
                + jnp.sum(jnp.dot(rows_above, tie_b, preferred_element_type=F32), axis=1, keepdims=True))
        keep = jnp.where(key > thr, 1.0, jnp.where(rank < need, tie, 0.0))
        bias_ref[...] = jnp.where(keep > 0.0, 0.0, -jnp.inf)
        keep_new = jnp.where(key_new > thr, 1.0,
                             jnp.where(key_new == thr, jnp.where(total(tie) < need, 1.0, 0.0), 0.0))
        bnew_ref[...] = jnp.where(keep_new > 0.0, 0.0, -jnp.inf)

    @pl.when(phase == 0)
    def _():
        for i in range(pg):
            s = jnp.dot(qi_ref[...], ki_pages[i][...].astype(BF16), preferred_element_type=F32)
            score_ref[pl.ds(j * pg + i, 1), :] = weigh(s)
        pl.when(j == last)(select)

    def flash_update(logits, pv_of):
        m_old = m_ref[...]
        m_new = jnp.maximum(m_old, jnp.max(logits, axis=1, keepdims=True))
        m_safe = jnp.where(m_new == -jnp.inf, 0.0, m_new)
        alpha = jnp.exp(m_old - m_safe)
        p = jnp.exp(logits - m_safe)
        m_ref[...] = m_new
        l_ref[...] = alpha * l_ref[...] + jnp.sum(p, axis=1, keepdims=True)
        acc_ref[...] = alpha * acc_ref[...] + pv_of(p)

    def by_group(per_kv_head):
        out = per_kv_head(0)
        for hk in range(1, N_KV_HEADS):
            out = jnp.where(head_group == hk, per_kv_head(hk), out)
        return out

    @pl.when(phase == 1)
    def _():
        @pl.when(j == 0)
        def _():
            m_ref[...] = jnp.full(m_ref.shape, -jnp.inf, F32)
            l_ref[...] = jnp.zeros(l_ref.shape, F32)
            acc_ref[...] = jnp.zeros(acc_ref.shape, F32)

        q = q_ref[...]
        logits = jnp.concatenate(
            [by_group(lambda hk: jnp.dot(q, k_pages[i][hk].astype(BF16), preferred_element_type=F32))
             + bias_ref[pl.ds(j * pg + i, 1), :] for i in range(pg)], axis=1)

        def pv_of(p):
            p = p.astype(BF16)
            page = lambda i: p[:, i * PAGE_SIZE:(i + 1) * PAGE_SIZE]
            return by_group(lambda hk: sum(
                _nt_dot(page(i), v_pages[i][hk].astype(BF16), preferred_element_type=F32) for i in range(pg)))

        flash_update(logits, pv_of)

        @pl.when(j == last)
        def _():
            rows = lambda ref: by_group(lambda hk: ref[hk:hk + 1, :]).astype(BF16).astype(F32)
            logit_new = jnp.sum(q.astype(F32) * rows(kn_ref), axis=1, keepdims=True) + bnew_ref[...]
            flash_update(logit_new, lambda p: p.astype(BF16).astype(F32) * rows(vn_ref))
            o = (acc_ref[...] / l_ref[...]).astype(BF16)
            y = x_ref[...]
            for h in range(N_HEADS):
                o_h = jnp.broadcast_to(o[h:h + 1, :], (2 * SUBLANES, HEAD_DIM))
                y = y + jnp.dot(o_h, wo_ref[h], preferred_element_type=F32)[0:1]
            out_ref[...] = y


def _dsa_sample(x_s, q_s, qi_s, wi_s, ki_new, k_new, v_new, cache_k, cache_v, cache_idx_k, page_table, w_out):
    n_dec = x_s.shape[0]
    n_pages = page_table.shape[1]
    n_pool = cache_k.shape[0]
    pg = PAGE_GROUP
    n_groups = n_pages // pg
    n_keep = min(TOPK_MAX, (n_pages * PAGE_SIZE + 1) // 4)
    def page_spec(page_shape, i, live_phase):
        def index(b, ph, j, pt):
            idle = i if live_phase == 1 else (n_groups - 1) * pg + i
            return (pt[b, jnp.where(ph == live_phase, j * pg + i, idle)],) + (0,) * len(page_shape)
        return pl.BlockSpec((None,) + page_shape, index)

    idx_page = (IDX_DIM, PAGE_SIZE)
    kv_page = (N_KV_HEADS, HEAD_DIM, PAGE_SIZE)
    cik = cache_idx_k.transpose(0, 2, 1)
    ck = cache_k.transpose(0, 2, 3, 1)
    cv = cache_v.transpose(0, 2, 3, 1)

    per_seq = lambda *shape: pl.BlockSpec((None,) + shape, lambda b, ph, j, pt: (b,) + (0,) * len(shape))
    grid_spec = pltpu.PrefetchScalarGridSpec(
        num_scalar_prefetch=1,
        grid=(n_dec, 2, n_groups),
        in_specs=([page_spec(idx_page, i, 0) for i in range(pg)]
                  + [page_spec(kv_page, i, 1) for i in range(pg)]
                  + [page_spec(kv_page, i, 1) for i in range(pg)]
                  + [per_seq(IDX_HEADS, IDX_DIM), per_seq(IDX_HEADS, 1), per_seq(1, IDX_DIM),
                     per_seq(N_HEADS, HEAD_DIM), per_seq(N_KV_HEADS, HEAD_DIM), per_seq(N_KV_HEADS, HEAD_DIM),
                     per_seq(1, D_MODEL),
                     pl.BlockSpec((N_HEADS, HEAD_DIM, D_MODEL), lambda b, ph, j, pt: (0, 0, 0))]),
        out_specs=per_seq(1, D_MODEL),
        scratch_shapes=[pltpu.VMEM((n_pages, PAGE_SIZE), F32), pltpu.VMEM((n_pages, PAGE_SIZE), F32),
                        pltpu.VMEM((1, 1), F32), pltpu.VMEM((N_HEADS, 1), F32), pltpu.VMEM((N_HEADS, 1), F32),
                        pltpu.VMEM((N_HEADS, HEAD_DIM), F32)],
    )
    out = pl.pallas_call(
        functools.partial(_dsa_sample_kernel, n_keep=n_keep, n_pages=n_pages),
        grid_spec=grid_spec,
        out_shape=jax.ShapeDtypeStruct((n_dec, 1, D_MODEL), F32),
        compiler_params=pltpu.CompilerParams(
            dimension_semantics=("arbitrary", "arbitrary", "arbitrary"), vmem_limit_bytes=VMEM_LIMIT),
        name="dsa_sample",
    )(page_table, *([cik] * pg), *([ck] * pg), *([cv] * pg),
      qi_s, wi_s.reshape(n_dec, IDX_HEADS, 1), ki_new.reshape(n_dec, 1, IDX_DIM), q_s,
      k_new.reshape(n_dec, N_KV_HEADS, HEAD_DIM), v_new.reshape(n_dec, N_KV_HEADS, HEAD_DIM),
      x_s.reshape(n_dec, 1, D_MODEL), w_out.astype(BF16).reshape(N_HEADS, HEAD_DIM, D_MODEL))
    return out.reshape(n_dec, D_MODEL)


def kernel(x_prompt, x_sample, cache_k, cache_v, cache_idx_k, page_table, norm_mix_g, norm_ffn_g,
           gmlp_w_in, gmlp_v_g, gmlp_w_s, gmlp_b_s, gmlp_w_out,
           att_w_in, att_q_norm_g, att_k_norm_g, att_idx_k_norm_g, att_w_out,
           peer_w_q, peer_sub_k1, peer_sub_k2, peer_u, peer_v):
    n_batch, seq, _ = x_prompt.shape
    n_dec, dec_seq, _ = x_sample.shape
    assert dec_seq == 1 and seq % DSA_CLASS == 0 and (n_batch * seq) % ROW_BLOCK == 0
    n_prompt = n_batch * seq
    n_tok = n_prompt + n_dec
    n_rows = n_prompt + ROW_BLOCK
    assert n_dec <= ROW_BLOCK

    def peer(x, layer):
        return _peer_layer(x, norm_ffn_g[layer], peer_w_q[layer], peer_sub_k1[layer], peer_sub_k2[layer],
                           peer_u[layer], peer_v[layer])

    x_all = jnp.concatenate([x_prompt.reshape(n_prompt, D_MODEL), x_sample.reshape(n_dec, D_MODEL),
                             jnp.zeros((n_rows - n_tok, D_MODEL), F32)], axis=0)
    x_all, v_new = _gmlp_layer(x_all, n_prompt, norm_mix_g[0], gmlp_w_in[0], gmlp_v_g[0], gmlp_w_s[0],
                               gmlp_b_s[0], gmlp_w_out[0])
    x_all = peer(x_all, 0)
    pos = jnp.concatenate([jnp.tile(jnp.arange(seq), n_batch), jnp.full((n_rows - n_prompt,), PAST_LEN)])
    q, k, v, qi, ki, wi, new_k, new_v, new_ki = _att_project(
        x_all, pos, norm_mix_g[1], att_w_in[0], att_q_norm_g[0], att_k_norm_g[0], att_idx_k_norm_g[0])
    smp = slice(n_prompt, n_tok)
    heads = lambda a: a[:, smp].transpose(1, 0, 2)
    xs = _dsa_sample(x_all[smp], heads(q), heads(qi), wi[smp], new_ki[smp], new_k[smp], new_v[smp],
                     cache_k[0], cache_v[0], cache_idx_k[0], page_table, att_w_out[0])
    x_all = _dsa_prompt(x_all, (q, k, v, qi, ki, wi), att_w_out[0], n_batch, seq)
    x_all = lax.dynamic_update_slice(x_all, xs, (n_prompt, 0))
    x_all = peer(x_all, 1)

    kv_shape = (1, n_batch, seq, N_KV_HEADS, HEAD_DIM)
    kv_s_shape = (1, n_dec, 1, N_KV_HEADS, HEAD_DIM)
    return (x_all[:n_prompt].reshape(n_batch, seq, D_MODEL), x_all[smp].reshape(n_dec, 1, D_MODEL),
            new_k[:n_prompt].reshape(kv_shape), new_v[:n_prompt].reshape(kv_shape),
            new_ki[:n_prompt].reshape(1, n_batch, seq, IDX_DIM),
            new_k[smp].reshape(kv_s_shape), new_v[smp].reshape(kv_s_shape),
            new_ki[smp].reshape(1, n_dec, 1, IDX_DIM), v_new[:n_dec].reshape(1, n_dec, 1, GMLP_WIDTH))
```

```python
import functools
import math

import jax
import jax.numpy as jnp
import numpy as np
from jax import lax
from jax.experimental import pallas as pl
from jax.experimental.pallas import tpu as pltpu

F32 = jnp.float32
BF16 = jnp.bfloat16
I32 = jnp.int32

D_MODEL = 1024
EPS = 1e-6

LANES = 128
SUBLANES = 8
MIB = 1024 * 1024
VMEM_LIMIT = 56 * MIB

ROW_BLOCK = 256

PEER_HEADS = 8
N_KEYS = 128
N_EXPERTS = N_KEYS * N_KEYS
PEER_TOPK = 16
PEER_KEY_DIM = 256
PEER_HALF = PEER_KEY_DIM // 2
PEER_PICKS = PEER_HEADS * PEER_TOPK
ROW_WORDS = D_MODEL // 2
ROW_SUB = ROW_WORDS // LANES
PEER_TOKENS = 128
N_CAND = sum(PEER_TOPK // (a + 1) for a in range(PEER_TOPK))

CHUNK = 128
GMLP_WIDTH = 2 * D_MODEL
GMLP_GROUPS = 8
GMLP_GROUP_DIM = GMLP_WIDTH // GMLP_GROUPS

PAST_LEN = 16384
PAGE_SIZE = 128
N_HEADS = 16
HEAD_DIM = D_MODEL // N_HEADS
N_KV_HEADS = 4
GROUP = N_HEADS // N_KV_HEADS
IDX_HEADS = 8
IDX_DIM = 64
TOPK_MAX = 256
ROPE_THETA = 10000.0
IDX_SCALE = (IDX_DIM ** -0.5) * (IDX_HEADS ** -0.5)
ATT_SIZES = (N_HEADS * HEAD_DIM, N_KV_HEADS * HEAD_DIM, N_KV_HEADS * HEAD_DIM,
             IDX_HEADS * IDX_DIM, IDX_DIM, IDX_HEADS)
ATT_COLS = sum(ATT_SIZES)
ATT_COLS_PAD = -(-ATT_COLS // LANES) * LANES
Q_COL, K_COL, V_COL, QI_COL, KI_COL = (int(c) for c in np.cumsum((0,) + ATT_SIZES[:4]))


def _nt_dot(a, b, **kw):
    return lax.dot_general(a, b, (((1,), (1,)), ((), ())), **kw)


def _rms_rows(x, g):
    return x * lax.rsqrt(jnp.mean(x * x, axis=-1, keepdims=True) + EPS) * g


def _full_spec(shape, **kw):
    return pl.BlockSpec(shape, lambda *_: (0,) * len(shape), **kw)


def _topk_rows(s, k, iota):
    n_rows = s.shape[0]
    vals, idxs = [], []
    for _ in range(k):
        m = jnp.max(s, axis=0, keepdims=True)
        first = jnp.min(jnp.where(s == m, iota, n_rows), axis=0, keepdims=True)
        vals.append(m)
        idxs.append(first)
        s = jnp.where(iota == first, -jnp.inf, s)
    return jnp.concatenate(vals, axis=0), jnp.concatenate(idxs, axis=0)


def _peer_router_kernel(x_ref, g_ref, wq_ref, k1_ref, k2_ref, h_ref, eidx_ref, gate_ref):
    h = _rms_rows(x_ref[...], g_ref[...])
    h_ref[...] = h
    q = jnp.dot(h.astype(BF16), wq_ref[...], preferred_element_type=F32)
    n_tok = h.shape[0]
    iota_k = lax.broadcasted_iota(I32, (N_KEYS, n_tok), 0)
    iota_c = lax.broadcasted_iota(I32, (N_CAND, n_tok), 0)
    k1 = k1_ref[...]
    k2 = k2_ref[...]
    eidx_rows, gate_rows = [], []
    for head in range(PEER_HEADS):
        base = head * PEER_KEY_DIM
        q1 = q[:, base:base + PEER_HALF].astype(BF16)
        q2 = q[:, base + PEER_HALF:base + PEER_KEY_DIM].astype(BF16)
        s1 = _nt_dot(k1, q1, preferred_element_type=F32)
        s2 = _nt_dot(k2, q2, preferred_element_type=F32)
        v1, i1 = _topk_rows(s1, PEER_TOPK, iota_k)
        v2, i2 = _topk_rows(s2, PEER_TOPK, iota_k)
        width = [PEER_TOPK // (a + 1) for a in range(PEER_TOPK)]
        cand = jnp.concatenate([v1[a:a + 1] + v2[:width[a]] for a in range(PEER_TOPK)], axis=0)
        cidx = jnp.concatenate([(i1[a:a + 1] * N_KEYS + i2[:width[a]]) * ROW_SUB for a in range(PEER_TOPK)], axis=0)
        top_s, top_e = [], []
        for _ in range(PEER_TOPK):
            m = jnp.max(cand, axis=0, keepdims=True)
            first = jnp.min(jnp.where(cand == m, iota_c, N_CAND), axis=0, keepdims=True)
            hit = iota_c == first
            top_s.append(m)
            top_e.append(jnp.sum(jnp.where(hit, cidx, 0), axis=0, keepdims=True))
            cand = jnp.where(hit, -jnp.inf, cand)
        top_s = jnp.concatenate(top_s, axis=0)
        e = jnp.exp(top_s - top_s[0:1])
        gate_rows.append(e / jnp.sum(e, axis=0, keepdims=True))
        eidx_rows.append(jnp.concatenate(top_e, axis=0))
    gate_t = jnp.concatenate(gate_rows, axis=0)
    eidx_t = jnp.concatenate(eidx_rows, axis=0)
    gate_ref[...] = gate_t.T
    eidx_ref[...] = lax.bitcast_convert_type(lax.bitcast_convert_type(eidx_t, F32).T, I32)


def _peer_router(x, g, w_q, sub_k1, sub_k2):
    n = x.shape[0]
    tm = PEER_TOKENS
    return pl.pallas_call(
        _peer_router_kernel,
        grid=(n // tm,),
        in_specs=[
            pl.BlockSpec((tm, D_MODEL), lambda i: (i, 0)),
            _full_spec((1, D_MODEL)),
            _full_spec((D_MODEL, PEER_HEADS * PEER_KEY_DIM)),
            _full_spec((N_KEYS, PEER_HALF)),
            _full_spec((N_KEYS, PEER_HALF)),
        ],
        out_specs=[
            pl.BlockSpec((tm, D_MODEL), lambda i: (i, 0)),
            pl.BlockSpec((tm, PEER_PICKS), lambda i: (i, 0)),
            pl.BlockSpec((tm, PEER_PICKS), lambda i: (i, 0)),
        ],
        out_shape=[
            jax.ShapeDtypeStruct((n, D_MODEL), F32),
            jax.ShapeDtypeStruct((n, PEER_PICKS), I32),
            jax.ShapeDtypeStruct((n, PEER_PICKS), F32),
        ],
        compiler_params=pltpu.CompilerParams(dimension_semantics=("arbitrary",), vmem_limit_bytes=VMEM_LIMIT),
        name="peer_router",
    )(x, g.reshape(1, D_MODEL), w_q.astype(BF16), sub_k1.astype(BF16), sub_k2.astype(BF16))


PACK_ROWS = 512


def _pack_kernel(w_ref, out_ref):
    w = w_ref[...]
    as_bits = lambda a: lax.bitcast_convert_type(a.astype(jnp.bfloat16).astype(F32), I32)
    hi = as_bits(w[:, :ROW_WORDS]) & jnp.int32(-65536)
    lo = (as_bits(w[:, ROW_WORDS:]) >> 16) & jnp.int32(0xFFFF)
    out_ref[...] = hi | lo


def _pack_table(w):
    words = pl.pallas_call(
        _pack_kernel,
        grid=(N_EXPERTS // PACK_ROWS,),
        in_specs=[pl.BlockSpec((PACK_ROWS, D_MODEL), lambda i: (i, 0))],
        out_specs=pl.BlockSpec((PACK_ROWS, ROW_WORDS), lambda i: (i, 0)),
        out_shape=jax.ShapeDtypeStruct((N_EXPERTS, ROW_WORDS), I32),
        compiler_params=pltpu.CompilerParams(dimension_semantics=("arbitrary",)),
        name="peer_pack_table",
    )(w)
    return words.reshape(N_EXPERTS * ROW_SUB, LANES)


PLANE_STRIDE = PEER_PICKS + SUBLANES
GATHER_ROWS = ROW_SUB * PLANE_STRIDE
TOKEN_GROUP = 2
N_GATHER_BUFS = 2 * TOKEN_GROUP


def _gather_token(eidx_ref, tab_ref, buf_ref, t):
    for p in range(PEER_PICKS):
        row0 = pl.multiple_of(eidx_ref[t, p], ROW_SUB)
        buf_ref[pl.ds(p, ROW_SUB, stride=PLANE_STRIDE), :] = tab_ref[pl.ds(row0, ROW_SUB), :]


def _gathered_rows(buf_ref):
    hi, lo = [], []
    for s in range(ROW_SUB):
        w = buf_ref[s * PLANE_STRIDE:s * PLANE_STRIDE + PEER_PICKS, :]
        hi.append(lax.bitcast_convert_type(w & jnp.int32(-65536), F32).astype(BF16))
        lo.append(lax.bitcast_convert_type(w << 16, F32).astype(BF16))
    return jnp.concatenate(hi + lo, axis=1)


def _token_pipeline(eidx_ref, tab_ref, bufs, n_tok, compute):
    g = TOKEN_GROUP
    groups = (bufs[:g], bufs[g:])
    for k, buf in enumerate(bufs):
        _gather_token(eidx_ref, tab_ref, buf, k)

    def step(i, carry):
        for j, group in enumerate(groups):
            t = 2 * g * i + g * j
            compute(t, group)
            for k, buf in enumerate(group):
                _gather_token(eidx_ref, tab_ref, buf, jnp.minimum(t + 2 * g + k, n_tok - 1))
        return carry

    lax.fori_loop(0, n_tok // (2 * g), step, 0)


def _group_rows(group):
    return jnp.concatenate([_gathered_rows(buf) for buf in group], axis=0)


def _token_rows(ref, t):
    return [ref[pl.ds(t + k, 1), :] for k in range(TOKEN_GROUP)]


def _own_block(n_rows):
    r = lax.broadcasted_iota(I32, (n_rows, TOKEN_GROUP * PEER_PICKS), 0)
    c = lax.broadcasted_iota(I32, (n_rows, TOKEN_GROUP * PEER_PICKS), 1)
    return r == c // PEER_PICKS


def _peer_act_kernel(eidx_ref, tab_ref, h_ref, gate_ref, coef_ref, *bufs):
    g = TOKEN_GROUP
    pad = jnp.zeros((2 * SUBLANES - g, D_MODEL), F32)

    def compute(t, group):
        x = jnp.concatenate(_token_rows(h_ref, t) + [pad], axis=0).astype(BF16)
        dots = _nt_dot(x, _group_rows(group), preferred_element_type=F32)
        act = jnp.concatenate([dots[k:k + 1, k * PEER_PICKS:(k + 1) * PEER_PICKS] for k in range(g)], axis=0)
        coef = jnp.concatenate(_token_rows(gate_ref, t), axis=0) * jax.nn.gelu(act)
        for k in range(g):
            coef_ref[pl.ds(t + k, 1), :] = coef[k:k + 1]

    _token_pipeline(eidx_ref, tab_ref, bufs, h_ref.shape[0], compute)


def _peer_out_kernel(eidx_ref, tab_ref, coef_ref, x_ref, out_ref, *bufs):
    g = TOKEN_GROUP
    own = _own_block(g)
    pad = jnp.zeros((2 * SUBLANES - g, g * PEER_PICKS), F32)

    def compute(t, group):
        c = jnp.concatenate(_token_rows(coef_ref, t), axis=0)
        lhs = jnp.concatenate([jnp.where(own, jnp.concatenate([c] * g, axis=1), 0.0), pad], axis=0)
        y = jnp.dot(lhs.astype(BF16), _group_rows(group), preferred_element_type=F32)
        for k in range(g):
            out_ref[pl.ds(t + k, 1), :] = x_ref[pl.ds(t + k, 1), :] + y[k:k + 1]

    _token_pipeline(eidx_ref, tab_ref, bufs, x_ref.shape[0], compute)


def _peer_gather_call(body, name, eidx, tab, a, b, out_cols):
    n = eidx.shape[0]
    tb = ROW_BLOCK
    rows = lambda cols: pl.BlockSpec((tb, cols), lambda i: (i, 0))
    return pl.pallas_call(
        body,
        grid=(n // tb,),
        in_specs=[
            pl.BlockSpec((tb, PEER_PICKS), lambda i: (i, 0), memory_space=pltpu.SMEM),
            _full_spec((N_EXPERTS * ROW_SUB, LANES), pipeline_mode=pl.Buffered(1)),
            rows(a.shape[1]),
            rows(b.shape[1]),
        ],
        out_specs=rows(out_cols),
        out_shape=jax.ShapeDtypeStruct((n, out_cols), F32),
        scratch_shapes=[pltpu.VMEM((GATHER_ROWS, LANES), I32) for _ in range(N_GATHER_BUFS)],
        compiler_params=pltpu.CompilerParams(dimension_semantics=("arbitrary",), vmem_limit_bytes=VMEM_LIMIT),
        name=name,
    )(eidx, tab, a, b)


def _peer_layer(x, g, w_q, sub_k1, sub_k2, expert_u, expert_v):
    h, eidx, gate = _peer_router(x, g, w_q, sub_k1, sub_k2)
    coef = _peer_gather_call(_peer_act_kernel, "peer_act", eidx, _pack_table(expert_u), h, gate, PEER_PICKS)
    return _peer_gather_call(_peer_out_kernel, "peer_out", eidx, _pack_table(expert_v), coef, x, D_MODEL)


def _gmlp_kernel(x_ref, g_ref, win_ref, vg_ref, ws_ref, bst_ref, s0_ref, b0_ref, wout_ref,
                 out_ref, vnew_ref, gated_ref, *, n_chunk_blocks):
    x = x_ref[...]
    h = _rms_rows(x, g_ref[...])
    uv = jnp.dot(h.astype(BF16), win_ref[...], preferred_element_type=F32)
    v = _rms_rows(uv[:, GMLP_WIDTH:], vg_ref[...])
    is_chunked = pl.program_id(0) < n_chunk_blocks

    @pl.when(is_chunked)
    def _():
        r_i = lax.broadcasted_iota(I32, (CHUNK, CHUNK), 0)
        c_i = lax.broadcasted_iota(I32, (CHUNK, CHUNK), 1)
        causal = c_i <= r_i
        for g in range(GMLP_GROUPS):
            cols = slice(g * GMLP_GROUP_DIM, (g + 1) * GMLP_GROUP_DIM)
            ws = jnp.where(causal, ws_ref[g], 0.0).astype(BF16)
            bias = bst_ref[:, g:g + 1]
            for c in range(x.shape[0] // CHUNK):
                rows = slice(c * CHUNK, (c + 1) * CHUNK)
                mixed = jnp.dot(ws, v[rows, cols].astype(BF16), preferred_element_type=F32) + bias
                gated_ref[rows, cols] = (uv[rows, cols] * mixed).astype(BF16)

    @pl.when(jnp.logical_not(is_chunked))
    def _():
        gated_ref[...] = (uv[:, :GMLP_WIDTH] * (v * s0_ref[...] + b0_ref[...])).astype(BF16)
        vnew_ref[...] = v

    out_ref[...] = x + jnp.dot(gated_ref[...], wout_ref[...], preferred_element_type=F32)


def _gmlp_layer(x_all, n_chunk_rows, g, w_in, v_g, w_s, b_s, w_out):
    n = x_all.shape[0]
    tb = ROW_BLOCK
    spread = lambda a: jnp.repeat(a, GMLP_GROUP_DIM).reshape(1, GMLP_WIDTH)
    body = functools.partial(_gmlp_kernel, n_chunk_blocks=n_chunk_rows // tb)
    return pl.pallas_call(
        body,
        grid=(n // tb,),
        in_specs=[
            pl.BlockSpec((tb, D_MODEL), lambda i: (i, 0)),
            _full_spec((1, D_MODEL)),
            _full_spec((D_MODEL, 2 * GMLP_WIDTH), pipeline_mode=pl.Buffered(1)),
            _full_spec((1, GMLP_WIDTH)),
            _full_spec((GMLP_GROUPS, CHUNK, CHUNK)),
            _full_spec((CHUNK, GMLP_GROUPS)),
            _full_spec((1, GMLP_WIDTH)),
            _full_spec((1, GMLP_WIDTH)),
            _full_spec((GMLP_WIDTH, D_MODEL), pipeline_mode=pl.Buffered(1)),
        ],
        out_specs=[
            pl.BlockSpec((tb, D_MODEL), lambda i: (i, 0)),
            _full_spec((tb, GMLP_WIDTH)),
        ],
        out_shape=[
            jax.ShapeDtypeStruct((n, D_MODEL), F32),
            jax.ShapeDtypeStruct((tb, GMLP_WIDTH), F32),
        ],
        scratch_shapes=[pltpu.VMEM((tb, GMLP_WIDTH), BF16)],
        compiler_params=pltpu.CompilerParams(dimension_semantics=("arbitrary",), vmem_limit_bytes=VMEM_LIMIT),
        name="gmlp_mixer",
    )(x_all, g.reshape(1, D_MODEL), w_in.astype(BF16), v_g.reshape(1, GMLP_WIDTH), w_s, b_s.T,
      spread(w_s[:, 0, 0]), spread(b_s[:, 0]), w_out.astype(BF16))


def _att_proj_kernel(x_ref, g_ref, w_ref, qg_ref, kg_ref, ikg_ref, cos_ref, sin_ref,
                     q_ref, k_ref, v_ref, qi_ref, ki_ref, wi_ref, nk_ref, nv_ref, nki_ref):
    h = _rms_rows(x_ref[...], g_ref[...])
    p = jnp.dot(h.astype(BF16), w_ref[...], preferred_element_type=F32)
    n_tok = p.shape[0]
    cos = cos_ref[...]
    sin = sin_ref[...]
    lane = lax.broadcasted_iota(I32, (n_tok, LANES), 1)
    first_half = (lane & (HEAD_DIM // 2)) == 0
    r_i = lax.broadcasted_iota(I32, (LANES, LANES), 0)
    c_i = lax.broadcasted_iota(I32, (LANES, LANES), 1)
    head_mean = jnp.where((r_i // HEAD_DIM) == (c_i // HEAD_DIM), 1.0 / HEAD_DIM, 0.0).astype(BF16)

    def slab(col):
        return p[:, col:col + LANES]

    def head_rms(a, gain):
        ms = jnp.dot((a * a).astype(BF16), head_mean, preferred_element_type=F32)
        return a * lax.rsqrt(ms + EPS) * gain

    def rope(a):
        partner = jnp.where(first_half, pltpu.roll(a, LANES - HEAD_DIM // 2, 1), pltpu.roll(a, HEAD_DIM // 2, 1))
        return a * cos + partner * sin

    def put_heads(ref, i, a):
        ref[2 * i] = a[:, :HEAD_DIM].astype(BF16)
        ref[2 * i + 1] = a[:, HEAD_DIM:].astype(BF16)

    for i in range(N_HEADS // 2):
        put_heads(q_ref, i, rope(head_rms(slab(Q_COL + i * LANES), qg_ref[...])) * (HEAD_DIM ** -0.5))
    for i in range(N_KV_HEADS // 2):
        k = rope(head_rms(slab(K_COL + i * LANES), kg_ref[...]))
        nk_ref[:, i * LANES:(i + 1) * LANES] = k
        put_heads(k_ref, i, k)
        v = slab(V_COL + i * LANES)
        nv_ref[:, i * LANES:(i + 1) * LANES] = v
        put_heads(v_ref, i, v)
    for i in range(IDX_HEADS // 2):
        put_heads(qi_ref, i, rope(slab(QI_COL + i * LANES)))
    tail = slab(KI_COL)
    ki = rope(head_rms(tail, ikg_ref[...]))[:, :IDX_DIM]
    nki_ref[...] = ki
    ki_ref[...] = ki.astype(BF16)
    wi_ref[...] = tail[:, IDX_DIM:IDX_DIM + IDX_HEADS]


def _att_project(x_all, pos, g, w_in, qn_g, kn_g, ikn_g):
    n = x_all.shape[0]
    tb = ROW_BLOCK
    half = HEAD_DIM // 2
    inv = jnp.exp(-math.log(ROPE_THETA) * jnp.arange(half, dtype=F32) / half)
    ang = pos.astype(F32)[:, None] * inv[None, :]
    cos = jnp.tile(jnp.cos(ang), (1, LANES // half))
    sin = jnp.tile(jnp.concatenate([-jnp.sin(ang), jnp.sin(ang)], axis=1), (1, LANES // HEAD_DIM))
    pair = lambda gain: jnp.tile(gain, LANES // HEAD_DIM).reshape(1, LANES)
    w_pad = jnp.pad(w_in, ((0, 0), (0, ATT_COLS_PAD - ATT_COLS))).astype(BF16)
    heads = lambda n_heads: (pl.BlockSpec((n_heads, tb, HEAD_DIM), lambda i: (0, i, 0)),
                             jax.ShapeDtypeStruct((n_heads, n, HEAD_DIM), BF16))
    rows = lambda cols, dt: (pl.BlockSpec((tb, cols), lambda i: (i, 0)), jax.ShapeDtypeStruct((n, cols), dt))
    outs = [heads(N_HEADS), heads(N_KV_HEADS), heads(N_KV_HEADS), heads(IDX_HEADS), rows(IDX_DIM, BF16),
            rows(IDX_HEADS, F32), rows(N_KV_HEADS * HEAD_DIM, F32), rows(N_KV_HEADS * HEAD_DIM, F32),
            rows(IDX_DIM, F32)]
    return pl.pallas_call(
        _att_proj_kernel,
        grid=(n // tb,),
        in_specs=[
            pl.BlockSpec((tb, D_MODEL), lambda i: (i, 0)),
            _full_spec((1, D_MODEL)),
            _full_spec((D_MODEL, ATT_COLS_PAD), pipeline_mode=pl.Buffered(1)),
            _full_spec((1, LANES)),
            _full_spec((1, LANES)),
            _full_spec((1, LANES)),
            pl.BlockSpec((tb, LANES), lambda i: (i, 0)),
            pl.BlockSpec((tb, LANES), lambda i: (i, 0)),
        ],
        out_specs=[o[0] for o in outs],
        out_shape=[o[1] for o in outs],
        compiler_params=pltpu.CompilerParams(dimension_semantics=("arbitrary",), vmem_limit_bytes=VMEM_LIMIT),
        name="att_project",
    )(x_all, g.reshape(1, D_MODEL), w_pad, pair(qn_g), pair(kn_g), pair(ikn_g), cos, sin)


DSA_TQ = 256
DSA_CLASS = 256
INT32_MIN = -(2 ** 31)


def _sortable_key(score):
    bits = lax.bitcast_convert_type(score, I32)
    return bits ^ ((bits >> 31) & jnp.int32(0x7FFFFFFF))


def _dsa_prompt_kernel(qi_ref, wi_ref, ki_ref, q_ref, k_ref, v_ref, x_ref, wo_ref, out_ref,
                       key_ref, bias_ref, *, s_eff, q_block0, n_keep):
    tq = DSA_TQ
    q0 = (q_block0 + pl.program_id(1)) * tq
    ki = ki_ref[...]
    wi = wi_ref[...]
    score = jnp.zeros((tq, s_eff), F32)
    for h in range(IDX_HEADS):
        s = _nt_dot(qi_ref[h], ki, preferred_element_type=F32)
        score = score + wi[:, h:h + 1] * jnp.maximum(s, 0.0)
    score = score * IDX_SCALE
    q_pos = q0 + lax.broadcasted_iota(I32, (tq, 1), 0)
    k_pos = lax.broadcasted_iota(I32, (tq, s_eff), 1)
    score = jnp.where(k_pos <= q_pos, score, -jnp.inf)
    key_ref[...] = _sortable_key(score)

    def search(it, thr):
        cand = thr + jnp.left_shift(jnp.int32(1), 31 - it)
        cnt = jnp.sum(jnp.where(key_ref[...] >= cand, 1, 0), axis=1, keepdims=True)
        return jnp.where(cnt >= n_keep, cand, thr)

    thr = lax.fori_loop(0, 32, search, jnp.full((tq, 1), INT32_MIN, I32))
    n_gt = jnp.sum(jnp.where(key_ref[...] > thr, 1, 0), axis=1, keepdims=True)
    need = (n_keep - n_gt).astype(F32)
    r_i = lax.broadcasted_iota(I32, (LANES, LANES), 0)
    c_i = lax.broadcasted_iota(I32, (LANES, LANES), 1)
    before = jnp.where(r_i < c_i, 1.0, 0.0).astype(BF16)
    run = jnp.zeros((tq, 1), F32)
    for c in range(s_eff // LANES):
        sl = slice(c * LANES, (c + 1) * LANES)
        key_c = key_ref[:, sl]
        tie = jnp.where(key_c == thr, 1.0, 0.0)
        rank = run + jnp.dot(tie.astype(BF16), before, preferred_element_type=F32)
        keep_tie = jnp.where(rank < need, tie, 0.0)
        keep = jnp.where(key_c > thr, 1.0, keep_tie)
        adm = (c * LANES + lax.broadcasted_iota(I32, (tq, LANES), 1)) <= q_pos
        bias_ref[:, sl] = jnp.where(adm, jnp.where(keep > 0.0, 0.0, -jnp.inf), -jnp.inf)
        run = run + jnp.sum(tie, axis=1, keepdims=True)

    outs = []
    for hk in range(N_KV_HEADS):
        k_h = k_ref[hk]
        v_h = v_ref[hk]
        for g in range(GROUP):
            logits = _nt_dot(q_ref[hk * GROUP + g], k_h, preferred_element_type=F32) + bias_ref[...]
            m = jnp.max(logits, axis=1, keepdims=True)
            p = jnp.exp(logits - m)
            den = jnp.sum(p, axis=1, keepdims=True)
            o = jnp.dot(p.astype(BF16), v_h, preferred_element_type=F32)
            outs.append(o / den)
    o_all = jnp.concatenate(outs, axis=1)
    out_ref[...] = x_ref[...] + jnp.dot(o_all.astype(BF16), wo_ref[...], preferred_element_type=F32)


def _dsa_prompt_call(x_all, qi, wi, ki, q, k, v, w_out, *, n_batch, seq, s_eff, q_block0, n_qblocks, n_keep):
    tq = DSA_TQ
    per_seq = seq // tq
    q_rows = lambda b, j: b * per_seq + q_block0 + j
    keys = lambda n_heads: pl.BlockSpec((pl.Element(n_heads), pl.Element(s_eff), pl.Element(HEAD_DIM)),
                                        lambda b, j: (0, b * seq, 0), pipeline_mode=pl.Buffered(1))
    body = functools.partial(_dsa_prompt_kernel, s_eff=s_eff, q_block0=q_block0, n_keep=n_keep)
    return pl.pallas_call(
        body,
        grid=(n_batch, n_qblocks),
        in_specs=[
            pl.BlockSpec((IDX_HEADS, tq, IDX_DIM), lambda b, j: (0, q_rows(b, j), 0)),
            pl.BlockSpec((tq, IDX_HEADS), lambda b, j: (q_rows(b, j), 0)),
            pl.BlockSpec((pl.Element(s_eff), pl.Element(IDX_DIM)), lambda b, j: (b * seq, 0),
                         pipeline_mode=pl.Buffered(1)),
            pl.BlockSpec((N_HEADS, tq, HEAD_DIM), lambda b, j: (0, q_rows(b, j), 0)),
            keys(N_KV_HEADS),
            keys(N_KV_HEADS),
            pl.BlockSpec((tq, D_MODEL), lambda b, j: (q_rows(b, j), 0)),
            _full_spec((D_MODEL, D_MODEL), pipeline_mode=pl.Buffered(1)),
        ],
        out_specs=pl.BlockSpec((tq, D_MODEL), lambda b, j: (q_rows(b, j), 0)),
        out_shape=jax.ShapeDtypeStruct(x_all.shape, F32),
        input_output_aliases={6: 0},
        scratch_shapes=[pltpu.VMEM((tq, s_eff), I32), pltpu.VMEM((tq, s_eff), F32)],
        compiler_params=pltpu.CompilerParams(
            dimension_semantics=("arbitrary", "arbitrary"), vmem_limit_bytes=VMEM_LIMIT),
        name=f"dsa_prompt_{s_eff}",
    )(qi, wi, ki, q, k, v, x_all, w_out)


def _dsa_prompt(x_all, proj, w_out, n_batch, seq, class_rows=DSA_CLASS):
    q, k, v, qi, ki, wi = proj
    n_keep = min(TOPK_MAX, seq // 4)
    wo_b = w_out.astype(BF16)
    per_call = class_rows // DSA_TQ
    for c in range(seq // class_rows):
        x_all = _dsa_prompt_call(x_all, qi, wi, ki, q, k, v, wo_b, n_batch=n_batch, seq=seq,
                                 s_eff=(c + 1) * class_rows, q_block0=c * per_call, n_qblocks=per_call,
                                 n_keep=n_keep)
    return x_all


PAGE_GROUP = 16


def _dsa_sample_kernel(pt_ref, *refs, n_keep, n_pages):
    del pt_ref
    pg = PAGE_GROUP
    ki_pages, k_pages, v_pages = refs[:pg], refs[pg:2 * pg], refs[2 * pg:3 * pg]
    (qi_ref, wi_ref, kin_ref, q_ref, kn_ref, vn_ref, x_ref, wo_ref, out_ref,
     score_ref, bias_ref, bnew_ref, m_ref, l_ref, acc_ref) = refs[3 * pg:]
    phase = pl.program_id(1)
    j = pl.program_id(2)
    last = pl.num_programs(2) - 1
    total = lambda a: jnp.sum(jnp.sum(a, axis=1, keepdims=True), axis=0, keepdims=True)
    head_group = lax.broadcasted_iota(I32, (N_HEADS, 1), 0) // GROUP

    def weigh(s):
        return jnp.sum(wi_ref[...] * jnp.maximum(s, 0.0), axis=0, keepdims=True) * IDX_SCALE

    def select():
        kin = kin_ref[...].astype(BF16).astype(F32)
        s_new = weigh(jnp.sum(qi_ref[...].astype(F32) * kin, axis=1, keepdims=True))
        key = _sortable_key(score_ref[...])
        key_new = _sortable_key(s_new)

        def search(it, thr):
            cand = thr + jnp.left_shift(jnp.int32(1), 31 - it)
            cnt = total(jnp.where(key >= cand, 1, 0)) + jnp.where(key_new >= cand, 1, 0)
            return jnp.where(cnt >= n_keep, cand, thr)

        thr = lax.fori_loop(0, 32, search, jnp.full((1, 1), INT32_MIN, I32))
        n_gt = total(jnp.where(key > thr, 1, 0)) + jnp.where(key_new > thr, 1, 0)
        need = (n_keep - n_gt).astype(F32)
        tie = jnp.where(key == thr, 1.0, 0.0)
        r_i = lax.broadcasted_iota(I32, (LANES, LANES), 0)
        c_i = lax.broadcasted_iota(I32, (LANES, LANES), 1)
        before = jnp.where(r_i < c_i, 1.0, 0.0).astype(BF16)
        rr = lax.broadcasted_iota(I32, (n_pages, n_pages), 0)
        cc = lax.broadcasted_iota(I32, (n_pages, n_pages), 1)
        rows_above = jnp.where(cc < rr, 1.0, 0.0).astype(BF16)
        tie_b = tie.astype(BF16)
        rank = (jnp.dot(tie_b, before, preferred_element_type=F32)
                + jnp.sum(jnp.dot(rows_above, tie_b, preferred_element_type=F32), axis=1, keepdims=True))
        keep = jnp.where(key > thr, 1.0, jnp.where(rank < need, tie, 0.0))
        bias_ref[...] = jnp.where(keep > 0.0, 0.0, -jnp.inf)
        keep_new = jnp.where(key_new > thr, 1.0,
                             jnp.where(key_new == thr, jnp.where(total(tie) < need, 1.0, 0.0), 0.0))
        bnew_ref[...] = jnp.where(keep_new > 0.0, 0.0, -jnp.inf)

    @pl.when(phase == 0)
    def _():
        for i in range(pg):
            s = jnp.dot(qi_ref[...], ki_pages[i][...].astype(BF16), preferred_element_type=F32)
            score_ref[pl.ds(j * pg + i, 1), :] = weigh(s)
        pl.when(j == last)(select)

    def flash_update(logits, pv_of):
        m_old = m_ref[...]
        m_new = jnp.maximum(m_old, jnp.max(logits, axis=1, keepdims=True))
        m_safe = jnp.where(m_new == -jnp.inf, 0.0, m_new)
        alpha = jnp.exp(m_old - m_safe)
        p = jnp.exp(logits - m_safe)
        m_ref[...] = m_new
        l_ref[...] = alpha * l_ref[...] + jnp.sum(p, axis=1, keepdims=True)
        acc_ref[...] = alpha * acc_ref[...] + pv_of(p)

    def by_group(per_kv_head):
        out = per_kv_head(0)
        for hk in range(1, N_KV_HEADS):
            out = jnp.where(head_group == hk, per_kv_head(hk), out)
        return out

    @pl.when(phase == 1)
    def _():
        @pl.when(j == 0)
        def _():
            m_ref[...] = jnp.full(m_ref.shape, -jnp.inf, F32)
            l_ref[...] = jnp.zeros(l_ref.shape, F32)
            acc_ref[...] = jnp.zeros(acc_ref.shape, F32)

        q = q_ref[...]
        logits = jnp.concatenate(
            [by_group(lambda hk: jnp.dot(q, k_pages[i][hk].astype(BF16), preferred_element_type=F32))
             + bias_ref[pl.ds(j * pg + i, 1), :] for i in range(pg)], axis=1)

        def pv_of(p):
            p = p.astype(BF16)
            page = lambda i: p[:, i * PAGE_SIZE:(i + 1) * PAGE_SIZE]
            return by_group(lambda hk: sum(
                _nt_dot(page(i), v_pages[i][hk].astype(BF16), preferred_element_type=F32) for i in range(pg)))

        flash_update(logits, pv_of)

        @pl.when(j == last)
        def _():
            rows = lambda ref: by_group(lambda hk: ref[hk:hk + 1, :]).astype(BF16).astype(F32)
            logit_new = jnp.sum(q.astype(F32) * rows(kn_ref), axis=1, keepdims=True) + bnew_ref[...]
            flash_update(logit_new, lambda p: p.astype(BF16).astype(F32) * rows(vn_ref))
            o = (acc_ref[...] / l_ref[...]).astype(BF16)
            y = x_ref[...]
            for h in range(N_HEADS):
                o_h = jnp.broadcast_to(o[h:h + 1, :], (2 * SUBLANES, HEAD_DIM))
                y = y + jnp.dot(o_h, wo_ref[h], preferred_element_type=F32)[0:1]
            out_ref[...] = y


def _dsa_sample(x_s, q_s, qi_s, wi_s, ki_new, k_new, v_new, cache_k, cache_v, cache_idx_k, page_table, w_out):
    n_dec = x_s.shape[0]
    n_pages = page_table.shape[1]
    n_pool = cache_k.shape[0]
    pg = PAGE_GROUP
    n_groups = n_pages // pg
    n_keep = min(TOPK_MAX, (n_pages * PAGE_SIZE + 1) // 4)
    def page_spec(page_shape, i, live_phase):
        def index(b, ph, j, pt):
            idle = i if live_phase == 1 else (n_groups - 1) * pg + i
            return (pt[b, jnp.where(ph == live_phase, j * pg + i, idle)],) + (0,) * len(page_shape)
        return pl.BlockSpec((None,) + page_shape, index)

    idx_page = (IDX_DIM, PAGE_SIZE)
    kv_page = (N_KV_HEADS, HEAD_DIM, PAGE_SIZE)
    cik = cache_idx_k.transpose(0, 2, 1)
    ck = cache_k.transpose(0, 2, 3, 1)
    cv = cache_v.transpose(0, 2, 3, 1)

    per_seq = lambda *shape: pl.BlockSpec((None,) + shape, lambda b, ph, j, pt: (b,) + (0,) * len(shape))
    grid_spec = pltpu.PrefetchScalarGridSpec(
        num_scalar_prefetch=1,
        grid=(n_dec, 2, n_groups),
        in_specs=([page_spec(idx_page, i, 0) for i in range(pg)]
                  + [page_spec(kv_page, i, 1) for i in range(pg)]
                  + [page_spec(kv_page, i, 1) for i in range(pg)]
                  + [per_seq(IDX_HEADS, IDX_DIM), per_seq(IDX_HEADS, 1), per_seq(1, IDX_DIM),
                     per_seq(N_HEADS, HEAD_DIM), per_seq(N_KV_HEADS, HEAD_DIM), per_seq(N_KV_HEADS, HEAD_DIM),
                     per_seq(1, D_MODEL),
                     pl.BlockSpec((N_HEADS, HEAD_DIM, D_MODEL), lambda b, ph, j, pt: (0, 0, 0))]),
        out_specs=per_seq(1, D_MODEL),
        scratch_shapes=[pltpu.VMEM((n_pages, PAGE_SIZE), F32), pltpu.VMEM((n_pages, PAGE_SIZE), F32),
                        pltpu.VMEM((1, 1), F32), pltpu.VMEM((N_HEADS, 1), F32), pltpu.VMEM((N_HEADS, 1), F32),
                        pltpu.VMEM((N_HEADS, HEAD_DIM), F32)],
    )
    out = pl.pallas_call(
        functools.partial(_dsa_sample_kernel, n_keep=n_keep, n_pages=n_pages),
        grid_spec=grid_spec,
        out_shape=jax.ShapeDtypeStruct((n_dec, 1, D_MODEL), F32),
        compiler_params=pltpu.CompilerParams(
            dimension_semantics=("arbitrary", "arbitrary", "arbitrary"), vmem_limit_bytes=VMEM_LIMIT),
        name="dsa_sample",
    )(page_table, *([cik] * pg), *([ck] * pg), *([cv] * pg),
      qi_s, wi_s.reshape(n_dec, IDX_HEADS, 1), ki_new.reshape(n_dec, 1, IDX_DIM), q_s,
      k_new.reshape(n_dec, N_KV_HEADS, HEAD_DIM), v_new.reshape(n_dec, N_KV_HEADS, HEAD_DIM),
      x_s.reshape(n_dec, 1, D_MODEL), w_out.astype(BF16).reshape(N_HEADS, HEAD_DIM, D_MODEL))
    return out.reshape(n_dec, D_MODEL)


def kernel(x_prompt, x_sample, cache_k, cache_v, cache_idx_k, page_table, norm_mix_g, norm_ffn_g,
           gmlp_w_in, gmlp_v_g, gmlp_w_s, gmlp_b_s, gmlp_w_out,
           att_w_in, att_q_norm_g, att_k_norm_g, att_idx_k_norm_g, att_w_out,
           peer_w_q, peer_sub_k1, peer_sub_k2, peer_u, peer_v):
    n_batch, seq, _ = x_prompt.shape
    n_dec, dec_seq, _ = x_sample.shape
    assert dec_seq == 1 and seq % DSA_CLASS == 0 and (n_batch * seq) % ROW_BLOCK == 0
    n_prompt = n_batch * seq
    n_tok = n_prompt + n_dec
    n_rows = n_prompt + ROW_BLOCK
    assert n_dec <= ROW_BLOCK

    def peer(x, layer):
        return _peer_layer(x, norm_ffn_g[layer], peer_w_q[layer], peer_sub_k1[layer], peer_sub_k2[layer],
                           peer_u[layer], peer_v[layer])

    x_all = jnp.concatenate([x_prompt.reshape(n_prompt, D_MODEL), x_sample.reshape(n_dec, D_MODEL),
                             jnp.zeros((n_rows - n_tok, D_MODEL), F32)], axis=0)
    x_all, v_new = _gmlp_layer(x_all, n_prompt, norm_mix_g[0], gmlp_w_in[0], gmlp_v_g[0], gmlp_w_s[0],
                               gmlp_b_s[0], gmlp_w_out[0])
    x_all = peer(x_all, 0)
    pos = jnp.concatenate([jnp.tile(jnp.arange(seq), n_batch), jnp.full((n_rows - n_prompt,), PAST_LEN)])
    q, k, v, qi, ki, wi, new_k, new_v, new_ki = _att_project(
        x_all, pos, norm_mix_g[1], att_w_in[0], att_q_norm_g[0], att_k_norm_g[0], att_idx_k_norm_g[0])
    smp = slice(n_prompt, n_tok)
    heads = lambda a: a[:, smp].transpose(1, 0, 2)
    xs = _dsa_sample(x_all[smp], heads(q), heads(qi), wi[smp], new_ki[smp], new_k[smp], new_v[smp],
                     cache_k[0], cache_v[0], cache_idx_k[0], page_table, att_w_out[0])
    x_all = _dsa_prompt(x_all, (q, k, v, qi, ki, wi), att_w_out[0], n_batch, seq)
    x_all = lax.dynamic_update_slice(x_all, xs, (n_prompt, 0))
    x_all = peer(x_all, 1)

    kv_shape = (1, n_batch, seq, N_KV_HEADS, HEAD_DIM)
    kv_s_shape = (1, n_dec, 1, N_KV_HEADS, HEAD_DIM)
    return (x_all[:n_prompt].reshape(n_batch, seq, D_MODEL), x_all[smp].reshape(n_dec, 1, D_MODEL),
            new_k[:n_prompt].reshape(kv_shape), new_v[:n_prompt].reshape(kv_shape),
            new_ki[:n_prompt].reshape(1, n_batch, seq, IDX_DIM),
            new_k[smp].reshape(kv_s_shape), new_v[smp].reshape(kv_s_shape),
            new_ki[smp].reshape(1, n_dec, 1, IDX_DIM), v_new[:n_dec].reshape(1, n_dec, 1, GMLP_WIDTH))
```

```python
import functools
import math

import jax
import jax.numpy as jnp
import numpy as np
from jax import lax
from jax.experimental import pallas as pl
from jax.experimental.pallas import tpu as pltpu

F32 = jnp.float32
BF16 = jnp.bfloat16
I32 = jnp.int32

D_MODEL = 1024
EPS = 1e-6

LANES = 128
SUBLANES = 8
MIB = 1024 * 1024
VMEM_LIMIT = 56 * MIB

ROW_BLOCK = 256

PEER_HEADS = 8
N_KEYS = 128
N_EXPERTS = N_KEYS * N_KEYS
PEER_TOPK = 16
PEER_KEY_DIM = 256
PEER_HALF = PEER_KEY_DIM // 2
PEER_PICKS = PEER_HEADS * PEER_TOPK
ROW_WORDS = D_MODEL // 2
ROW_SUB = ROW_WORDS // LANES
PEER_TOKENS = 128
N_CAND = sum(PEER_TOPK // (a + 1) for a in range(PEER_TOPK))

CHUNK = 128
GMLP_WIDTH = 2 * D_MODEL
GMLP_GROUPS = 8
GMLP_GROUP_DIM = GMLP_WIDTH // GMLP_GROUPS

PAST_LEN = 16384
PAGE_SIZE = 128
N_HEADS = 16
HEAD_DIM = D_MODEL // N_HEADS
N_KV_HEADS = 4
GROUP = N_HEADS // N_KV_HEADS
IDX_HEADS = 8
IDX_DIM = 64
TOPK_MAX = 256
ROPE_THETA = 10000.0
IDX_SCALE = (IDX_DIM ** -0.5) * (IDX_HEADS ** -0.5)
ATT_SIZES = (N_HEADS * HEAD_DIM, N_KV_HEADS * HEAD_DIM, N_KV_HEADS * HEAD_DIM,
             IDX_HEADS * IDX_DIM, IDX_DIM, IDX_HEADS)
ATT_COLS = sum(ATT_SIZES)
ATT_COLS_PAD = -(-ATT_COLS // LANES) * LANES
Q_COL, K_COL, V_COL, QI_COL, KI_COL = (int(c) for c in np.cumsum((0,) + ATT_SIZES[:4]))


def _nt_dot(a, b, **kw):
    return lax.dot_general(a, b, (((1,), (1,)), ((), ())), **kw)


def _rms_rows(x, g):
    return x * lax.rsqrt(jnp.mean(x * x, axis=-1, keepdims=True) + EPS) * g


def _full_spec(shape, **kw):
    return pl.BlockSpec(shape, lambda *_: (0,) * len(shape), **kw)


def _topk_rows(s, k, iota):
    n_rows = s.shape[0]
    vals, idxs = [], []
    for _ in range(k):
        m = jnp.max(s, axis=0, keepdims=True)
        first = jnp.min(jnp.where(s == m, iota, n_rows), axis=0, keepdims=True)
        vals.append(m)
        idxs.append(first)
        s = jnp.where(iota == first, -jnp.inf, s)
    return jnp.concatenate(vals, axis=0), jnp.concatenate(idxs, axis=0)


def _peer_router_kernel(x_ref, g_ref, wq_ref, k1_ref, k2_ref, h_ref, eidx_ref, gate_ref):
    h = _rms_rows(x_ref[...], g_ref[...])
    h_ref[...] = h
    q = jnp.dot(h.astype(BF16), wq_ref[...], preferred_element_type=F32)
    n_tok = h.shape[0]
    iota_k = lax.broadcasted_iota(I32, (N_KEYS, n_tok), 0)
    iota_c = lax.broadcasted_iota(I32, (N_CAND, n_tok), 0)
    k1 = k1_ref[...]
    k2 = k2_ref[...]
    eidx_rows, gate_rows = [], []
    for head in range(PEER_HEADS):
        base = head * PEER_KEY_DIM
        q1 = q[:, base:base + PEER_HALF].astype(BF16)
        q2 = q[:, base + PEER_HALF:base + PEER_KEY_DIM].astype(BF16)
        s1 = _nt_dot(k1, q1, preferred_element_type=F32)
        s2 = _nt_dot(k2, q2, preferred_element_type=F32)
        v1, i1 = _topk_rows(s1, PEER_TOPK, iota_k)
        v2, i2 = _topk_rows(s2, PEER_TOPK, iota_k)
        width = [PEER_TOPK // (a + 1) for a in range(PEER_TOPK)]
        cand = jnp.concatenate([v1[a:a + 1] + v2[:width[a]] for a in range(PEER_TOPK)], axis=0)
        cidx = jnp.concatenate([(i1[a:a + 1] * N_KEYS + i2[:width[a]]) * ROW_SUB for a in range(PEER_TOPK)], axis=0)
        top_s, top_e = [], []
        for _ in range(PEER_TOPK):
            m = jnp.max(cand, axis=0, keepdims=True)
            first = jnp.min(jnp.where(cand == m, iota_c, N_CAND), axis=0, keepdims=True)
            hit = iota_c == first
            top_s.append(m)
            top_e.append(jnp.sum(jnp.where(hit, cidx, 0), axis=0, keepdims=True))
            cand = jnp.where(hit, -jnp.inf, cand)
        top_s = jnp.concatenate(top_s, axis=0)
        e = jnp.exp(top_s - top_s[0:1])
        gate_rows.append(e / jnp.sum(e, axis=0, keepdims=True))
        eidx_rows.append(jnp.concatenate(top_e, axis=0))
    gate_t = jnp.concatenate(gate_rows, axis=0)
    eidx_t = jnp.concatenate(eidx_rows, axis=0)
    gate_ref[...] = gate_t.T
    eidx_ref[...] = lax.bitcast_convert_type(lax.bitcast_convert_type(eidx_t, F32).T, I32)


def _peer_router(x, g, w_q, sub_k1, sub_k2):
    n = x.shape[0]
    tm = PEER_TOKENS
    return pl.pallas_call(
        _peer_router_kernel,
        grid=(n // tm,),
        in_specs=[
            pl.BlockSpec((tm, D_MODEL), lambda i: (i, 0)),
            _full_spec((1, D_MODEL)),
            _full_spec((D_MODEL, PEER_HEADS * PEER_KEY_DIM)),
            _full_spec((N_KEYS, PEER_HALF)),
            _full_spec((N_KEYS, PEER_HALF)),
        ],
        out_specs=[
            pl.BlockSpec((tm, D_MODEL), lambda i: (i, 0)),
            pl.BlockSpec((tm, PEER_PICKS), lambda i: (i, 0)),
            pl.BlockSpec((tm, PEER_PICKS), lambda i: (i, 0)),
        ],
        out_shape=[
            jax.ShapeDtypeStruct((n, D_MODEL), F32),
            jax.ShapeDtypeStruct((n, PEER_PICKS), I32),
            jax.ShapeDtypeStruct((n, PEER_PICKS), F32),
        ],
        compiler_params=pltpu.CompilerParams(dimension_semantics=("arbitrary",), vmem_limit_bytes=VMEM_LIMIT),
        name="peer_router",
    )(x, g.reshape(1, D_MODEL), w_q.astype(BF16), sub_k1.astype(BF16), sub_k2.astype(BF16))


PACK_ROWS = 512


def _pack_kernel(w_ref, out_ref):
    w = w_ref[...]
    as_bits = lambda a: lax.bitcast_convert_type(a.astype(jnp.bfloat16).astype(F32), I32)
    hi = as_bits(w[:, :ROW_WORDS]) & jnp.int32(-65536)
    lo = (as_bits(w[:, ROW_WORDS:]) >> 16) & jnp.int32(0xFFFF)
    words = hi | lo
    for c in range(ROW_SUB):
        out_ref[:, c, :] = words[:, c * LANES:(c + 1) * LANES]


def _pack_table(w):
    words = pl.pallas_call(
        _pack_kernel,
        grid=(N_EXPERTS // PACK_ROWS,),
        in_specs=[pl.BlockSpec((PACK_ROWS, D_MODEL), lambda i: (i, 0))],
        out_specs=pl.BlockSpec((PACK_ROWS, ROW_SUB, LANES), lambda i: (i, 0, 0)),
        out_shape=jax.ShapeDtypeStruct((N_EXPERTS, ROW_SUB, LANES), I32),
        compiler_params=pltpu.CompilerParams(dimension_semantics=("arbitrary",)),
        name="peer_pack_table",
    )(w)
    return words.reshape(N_EXPERTS * ROW_SUB, LANES)


PLANE_STRIDE = PEER_PICKS + SUBLANES
GATHER_ROWS = ROW_SUB * PLANE_STRIDE
TOKEN_GROUP = 2
N_GATHER_BUFS = 2 * TOKEN_GROUP


def _gather_token(eidx_ref, tab_ref, buf_ref, t):
    for p in range(PEER_PICKS):
        row0 = pl.multiple_of(eidx_ref[t, p], ROW_SUB)
        buf_ref[pl.ds(p, ROW_SUB, stride=PLANE_STRIDE), :] = tab_ref[pl.ds(row0, ROW_SUB), :]


def _gathered_rows(buf_ref):
    hi, lo = [], []
    for s in range(ROW_SUB):
        w = buf_ref[s * PLANE_STRIDE:s * PLANE_STRIDE + PEER_PICKS, :]
        hi.append(lax.bitcast_convert_type(w & jnp.int32(-65536), F32).astype(BF16))
        lo.append(lax.bitcast_convert_type(w << 16, F32).astype(BF16))
    return jnp.concatenate(hi + lo, axis=1)


def _token_pipeline(eidx_ref, tab_ref, bufs, n_tok, compute):
    g = TOKEN_GROUP
    groups = (bufs[:g], bufs[g:])
    for k, buf in enumerate(bufs):
        _gather_token(eidx_ref, tab_ref, buf, k)

    def step(i, carry):
        for j, group in enumerate(groups):
            t = 2 * g * i + g * j
            compute(t, group)
            for k, buf in enumerate(group):
                _gather_token(eidx_ref, tab_ref, buf, jnp.minimum(t + 2 * g + k, n_tok - 1))
        return carry

    lax.fori_loop(0, n_tok // (2 * g), step, 0)


def _group_rows(group):
    return jnp.concatenate([_gathered_rows(buf) for buf in group], axis=0)


def _token_rows(ref, t):
    return [ref[pl.ds(t + k, 1), :] for k in range(TOKEN_GROUP)]


def _own_block(n_rows):
    r = lax.broadcasted_iota(I32, (n_rows, TOKEN_GROUP * PEER_PICKS), 0)
    c = lax.broadcasted_iota(I32, (n_rows, TOKEN_GROUP * PEER_PICKS), 1)
    return r == c // PEER_PICKS


def _peer_act_kernel(eidx_ref, tab_ref, h_ref, gate_ref, coef_ref, *bufs):
    g = TOKEN_GROUP
    pad = jnp.zeros((2 * SUBLANES - g, D_MODEL), F32)

    def compute(t, group):
        x = jnp.concatenate(_token_rows(h_ref, t) + [pad], axis=0).astype(BF16)
        dots = _nt_dot(x, _group_rows(group), preferred_element_type=F32)
        act = jnp.concatenate([dots[k:k + 1, k * PEER_PICKS:(k + 1) * PEER_PICKS] for k in range(g)], axis=0)
        coef = jnp.concatenate(_token_rows(gate_ref, t), axis=0) * jax.nn.gelu(act)
        for k in range(g):
            coef_ref[pl.ds(t + k, 1), :] = coef[k:k + 1]

    _token_pipeline(eidx_ref, tab_ref, bufs, h_ref.shape[0], compute)


def _peer_out_kernel(eidx_ref, tab_ref, coef_ref, x_ref, out_ref, *bufs):
    g = TOKEN_GROUP
    own = _own_block(g)
    pad = jnp.zeros((2 * SUBLANES - g, g * PEER_PICKS), F32)

    def compute(t, group):
        c = jnp.concatenate(_token_rows(coef_ref, t), axis=0)
        lhs = jnp.concatenate([jnp.where(own, jnp.concatenate([c] * g, axis=1), 0.0), pad], axis=0)
        y = jnp.dot(lhs.astype(BF16), _group_rows(group), preferred_element_type=F32)
        for k in range(g):
            out_ref[pl.ds(t + k, 1), :] = x_ref[pl.ds(t + k, 1), :] + y[k:k + 1]

    _token_pipeline(eidx_ref, tab_ref, bufs, x_ref.shape[0], compute)


def _peer_gather_call(body, name, eidx, tab, a, b, out_cols):
    n = eidx.shape[0]
    tb = ROW_BLOCK
    rows = lambda cols: pl.BlockSpec((tb, cols), lambda i: (i, 0))
    return pl.pallas_call(
        body,
        grid=(n // tb,),
        in_specs=[
            pl.BlockSpec((tb, PEER_PICKS), lambda i: (i, 0), memory_space=pltpu.SMEM),
            _full_spec((N_EXPERTS * ROW_SUB, LANES), pipeline_mode=pl.Buffered(1)),
            rows(a.shape[1]),
            rows(b.shape[1]),
        ],
        out_specs=rows(out_cols),
        out_shape=jax.ShapeDtypeStruct((n, out_cols), F32),
        scratch_shapes=[pltpu.VMEM((GATHER_ROWS, LANES), I32) for _ in range(N_GATHER_BUFS)],
        compiler_params=pltpu.CompilerParams(dimension_semantics=("arbitrary",), vmem_limit_bytes=VMEM_LIMIT),
        name=name,
    )(eidx, tab, a, b)


def _peer_layer(x, g, w_q, sub_k1, sub_k2, expert_u, expert_v):
    h, eidx, gate = _peer_router(x, g, w_q, sub_k1, sub_k2)
    coef = _peer_gather_call(_peer_act_kernel, "peer_act", eidx, _pack_table(expert_u), h, gate, PEER_PICKS)
    return _peer_gather_call(_peer_out_kernel, "peer_out", eidx, _pack_table(expert_v), coef, x, D_MODEL)


def _gmlp_kernel(x_ref, g_ref, win_ref, vg_ref, ws_ref, bst_ref, s0_ref, b0_ref, wout_ref,
                 out_ref, vnew_ref, gated_ref, *, n_chunk_blocks):
    x = x_ref[...]
    h = _rms_rows(x, g_ref[...])
    uv = jnp.dot(h.astype(BF16), win_ref[...], preferred_element_type=F32)
    v = _rms_rows(uv[:, GMLP_WIDTH:], vg_ref[...])
    is_chunked = pl.program_id(0) < n_chunk_blocks

    @pl.when(is_chunked)
    def _():
        r_i = lax.broadcasted_iota(I32, (CHUNK, CHUNK), 0)
        c_i = lax.broadcasted_iota(I32, (CHUNK, CHUNK), 1)
        causal = c_i <= r_i
        for g in range(GMLP_GROUPS):
            cols = slice(g * GMLP_GROUP_DIM, (g + 1) * GMLP_GROUP_DIM)
            ws = jnp.where(causal, ws_ref[g], 0.0).astype(BF16)
            bias = bst_ref[:, g:g + 1]
            for c in range(x.shape[0] // CHUNK):
                rows = slice(c * CHUNK, (c + 1) * CHUNK)
                mixed = jnp.dot(ws, v[rows, cols].astype(BF16), preferred_element_type=F32) + bias
                gated_ref[rows, cols] = (uv[rows, cols] * mixed).astype(BF16)

    @pl.when(jnp.logical_not(is_chunked))
    def _():
        gated_ref[...] = (uv[:, :GMLP_WIDTH] * (v * s0_ref[...] + b0_ref[...])).astype(BF16)
        vnew_ref[...] = v

    out_ref[...] = x + jnp.dot(gated_ref[...], wout_ref[...], preferred_element_type=F32)


def _gmlp_layer(x_all, n_chunk_rows, g, w_in, v_g, w_s, b_s, w_out):
    n = x_all.shape[0]
    tb = ROW_BLOCK
    spread = lambda a: jnp.repeat(a, GMLP_GROUP_DIM).reshape(1, GMLP_WIDTH)
    body = functools.partial(_gmlp_kernel, n_chunk_blocks=n_chunk_rows // tb)
    return pl.pallas_call(
        body,
        grid=(n // tb,),
        in_specs=[
            pl.BlockSpec((tb, D_MODEL), lambda i: (i, 0)),
            _full_spec((1, D_MODEL)),
            _full_spec((D_MODEL, 2 * GMLP_WIDTH), pipeline_mode=pl.Buffered(1)),
            _full_spec((1, GMLP_WIDTH)),
            _full_spec((GMLP_GROUPS, CHUNK, CHUNK)),
            _full_spec((CHUNK, GMLP_GROUPS)),
            _full_spec((1, GMLP_WIDTH)),
            _full_spec((1, GMLP_WIDTH)),
            _full_spec((GMLP_WIDTH, D_MODEL), pipeline_mode=pl.Buffered(1)),
        ],
        out_specs=[
            pl.BlockSpec((tb, D_MODEL), lambda i: (i, 0)),
            _full_spec((tb, GMLP_WIDTH)),
        ],
        out_shape=[
            jax.ShapeDtypeStruct((n, D_MODEL), F32),
            jax.ShapeDtypeStruct((tb, GMLP_WIDTH), F32),
        ],
        scratch_shapes=[pltpu.VMEM((tb, GMLP_WIDTH), BF16)],
        compiler_params=pltpu.CompilerParams(dimension_semantics=("arbitrary",), vmem_limit_bytes=VMEM_LIMIT),
        name="gmlp_mixer",
    )(x_all, g.reshape(1, D_MODEL), w_in.astype(BF16), v_g.reshape(1, GMLP_WIDTH), w_s, b_s.T,
      spread(w_s[:, 0, 0]), spread(b_s[:, 0]), w_out.astype(BF16))


def _att_proj_kernel(x_ref, g_ref, w_ref, qg_ref, kg_ref, ikg_ref, cos_ref, sin_ref,
                     q_ref, k_ref, v_ref, qi_ref, ki_ref, wi_ref, nk_ref, nv_ref, nki_ref):
    h = _rms_rows(x_ref[...], g_ref[...])
    p = jnp.dot(h.astype(BF16), w_ref[...], preferred_element_type=F32)
    n_tok = p.shape[0]
    cos = cos_ref[...]
    sin = sin_ref[...]
    lane = lax.broadcasted_iota(I32, (n_tok, LANES), 1)
    first_half = (lane & (HEAD_DIM // 2)) == 0
    r_i = lax.broadcasted_iota(I32, (LANES, LANES), 0)
    c_i = lax.broadcasted_iota(I32, (LANES, LANES), 1)
    head_mean = jnp.where((r_i // HEAD_DIM) == (c_i // HEAD_DIM), 1.0 / HEAD_DIM, 0.0).astype(BF16)

    def slab(col):
        return p[:, col:col + LANES]

    def head_rms(a, gain):
        ms = jnp.dot((a * a).astype(BF16), head_mean, preferred_element_type=F32)
        return a * lax.rsqrt(ms + EPS) * gain

    def rope(a):
        partner = jnp.where(first_half, pltpu.roll(a, LANES - HEAD_DIM // 2, 1), pltpu.roll(a, HEAD_DIM // 2, 1))
        return a * cos + partner * sin

    def put_heads(ref, i, a):
        ref[2 * i] = a[:, :HEAD_DIM].astype(BF16)
        ref[2 * i + 1] = a[:, HEAD_DIM:].astype(BF16)

    for i in range(N_HEADS // 2):
        put_heads(q_ref, i, rope(head_rms(slab(Q_COL + i * LANES), qg_ref[...])) * (HEAD_DIM ** -0.5))
    for i in range(N_KV_HEADS // 2):
        k = rope(head_rms(slab(K_COL + i * LANES), kg_ref[...]))
        nk_ref[:, i * LANES:(i + 1) * LANES] = k
        put_heads(k_ref, i, k)
        v = slab(V_COL + i * LANES)
        nv_ref[:, i * LANES:(i + 1) * LANES] = v
        put_heads(v_ref, i, v)
    for i in range(IDX_HEADS // 2):
        put_heads(qi_ref, i, rope(slab(QI_COL + i * LANES)))
    tail = slab(KI_COL)
    ki = rope(head_rms(tail, ikg_ref[...]))[:, :IDX_DIM]
    nki_ref[...] = ki
    ki_ref[...] = ki.astype(BF16)
    wi_ref[...] = tail[:, IDX_DIM:IDX_DIM + IDX_HEADS]


def _att_project(x_all, pos, g, w_in, qn_g, kn_g, ikn_g):
    n = x_all.shape[0]
    tb = ROW_BLOCK
    half = HEAD_DIM // 2
    inv = jnp.exp(-math.log(ROPE_THETA) * jnp.arange(half, dtype=F32) / half)
    ang = pos.astype(F32)[:, None] * inv[None, :]
    cos = jnp.tile(jnp.cos(ang), (1, LANES // half))
    sin = jnp.tile(jnp.concatenate([-jnp.sin(ang), jnp.sin(ang)], axis=1), (1, LANES // HEAD_DIM))
    pair = lambda gain: jnp.tile(gain, LANES // HEAD_DIM).reshape(1, LANES)
    w_pad = jnp.pad(w_in, ((0, 0), (0, ATT_COLS_PAD - ATT_COLS))).astype(BF16)
    heads = lambda n_heads: (pl.BlockSpec((n_heads, tb, HEAD_DIM), lambda i: (0, i, 0)),
                             jax.ShapeDtypeStruct((n_heads, n, HEAD_DIM), BF16))
    rows = lambda cols, dt: (pl.BlockSpec((tb, cols), lambda i: (i, 0)), jax.ShapeDtypeStruct((n, cols), dt))
    outs = [heads(N_HEADS), heads(N_KV_HEADS), heads(N_KV_HEADS), heads(IDX_HEADS), rows(IDX_DIM, BF16),
            rows(IDX_HEADS, F32), rows(N_KV_HEADS * HEAD_DIM, F32), rows(N_KV_HEADS * HEAD_DIM, F32),
            rows(IDX_DIM, F32)]
    return pl.pallas_call(
        _att_proj_kernel,
        grid=(n // tb,),
        in_specs=[
            pl.BlockSpec((tb, D_MODEL), lambda i: (i, 0)),
            _full_spec((1, D_MODEL)),
            _full_spec((D_MODEL, ATT_COLS_PAD), pipeline_mode=pl.Buffered(1)),
            _full_spec((1, LANES)),
            _full_spec((1, LANES)),
            _full_spec((1, LANES)),
            pl.BlockSpec((tb, LANES), lambda i: (i, 0)),
            pl.BlockSpec((tb, LANES), lambda i: (i, 0)),
        ],
        out_specs=[o[0] for o in outs],
        out_shape=[o[1] for o in outs],
        compiler_params=pltpu.CompilerParams(dimension_semantics=("arbitrary",), vmem_limit_bytes=VMEM_LIMIT),
        name="att_project",
    )(x_all, g.reshape(1, D_MODEL), w_pad, pair(qn_g), pair(kn_g), pair(ikn_g), cos, sin)


DSA_TQ = 256
DSA_CLASS = 256
INT32_MIN = -(2 ** 31)


def _sortable_key(score):
    bits = lax.bitcast_convert_type(score, I32)
    return bits ^ ((bits >> 31) & jnp.int32(0x7FFFFFFF))


def _dsa_prompt_kernel(qi_ref, wi_ref, ki_ref, q_ref, k_ref, v_ref, x_ref, wo_ref, out_ref,
                       key_ref, bias_ref, *, s_eff, q_block0, n_keep):
    tq = DSA_TQ
    q0 = (q_block0 + pl.program_id(1)) * tq
    ki = ki_ref[...]
    wi = wi_ref[...]
    score = jnp.zeros((tq, s_eff), F32)
    for h in range(IDX_HEADS):
        s = _nt_dot(qi_ref[h], ki, preferred_element_type=F32)
        score = score + wi[:, h:h + 1] * jnp.maximum(s, 0.0)
    score = score * IDX_SCALE
    q_pos = q0 + lax.broadcasted_iota(I32, (tq, 1), 0)
    k_pos = lax.broadcasted_iota(I32, (tq, s_eff), 1)
    score = jnp.where(k_pos <= q_pos, score, -jnp.inf)
    key_ref[...] = _sortable_key(score)

    def search(it, thr):
        cand = thr + jnp.left_shift(jnp.int32(1), 31 - it)
        cnt = jnp.sum(jnp.where(key_ref[...] >= cand, 1, 0), axis=1, keepdims=True)
        return jnp.where(cnt >= n_keep, cand, thr)

    thr = lax.fori_loop(0, 32, search, jnp.full((tq, 1), INT32_MIN, I32))
    n_gt = jnp.sum(jnp.where(key_ref[...] > thr, 1, 0), axis=1, keepdims=True)
    need = (n_keep - n_gt).astype(F32)
    r_i = lax.broadcasted_iota(I32, (LANES, LANES), 0)
    c_i = lax.broadcasted_iota(I32, (LANES, LANES), 1)
    before = jnp.where(r_i < c_i, 1.0, 0.0).astype(BF16)
    run = jnp.zeros((tq, 1), F32)
    for c in range(s_eff // LANES):
        sl = slice(c * LANES, (c + 1) * LANES)
        key_c = key_ref[:, sl]
        tie = jnp.where(key_c == thr, 1.0, 0.0)
        rank = run + jnp.dot(tie.astype(BF16), before, preferred_element_type=F32)
        keep_tie = jnp.where(rank < need, tie, 0.0)
        keep = jnp.where(key_c > thr, 1.0, keep_tie)
        adm = (c * LANES + lax.broadcasted_iota(I32, (tq, LANES), 1)) <= q_pos
        bias_ref[:, sl] = jnp.where(adm, jnp.where(keep > 0.0, 0.0, -jnp.inf), -jnp.inf)
        run = run + jnp.sum(tie, axis=1, keepdims=True)

    outs = []
    for hk in range(N_KV_HEADS):
        k_h = k_ref[hk]
        v_h = v_ref[hk]
        for g in range(GROUP):
            logits = _nt_dot(q_ref[hk * GROUP + g], k_h, preferred_element_type=F32) + bias_ref[...]
            m = jnp.max(logits, axis=1, keepdims=True)
            p = jnp.exp(logits - m)
            den = jnp.sum(p, axis=1, keepdims=True)
            o = jnp.dot(p.astype(BF16), v_h, preferred_element_type=F32)
            outs.append(o / den)
    o_all = jnp.concatenate(outs, axis=1)
    out_ref[...] = x_ref[...] + jnp.dot(o_all.astype(BF16), wo_ref[...], preferred_element_type=F32)


def _dsa_prompt_call(x_all, qi, wi, ki, q, k, v, w_out, *, n_batch, seq, s_eff, q_block0, n_qblocks, n_keep):
    tq = DSA_TQ
    per_seq = seq // tq
    q_rows = lambda b, j: b * per_seq + q_block0 + j
    keys = lambda n_heads: pl.BlockSpec((pl.Element(n_heads), pl.Element(s_eff), pl.Element(HEAD_DIM)),
                                        lambda b, j: (0, b * seq, 0), pipeline_mode=pl.Buffered(1))
    body = functools.partial(_dsa_prompt_kernel, s_eff=s_eff, q_block0=q_block0, n_keep=n_keep)
    return pl.pallas_call(
        body,
        grid=(n_batch, n_qblocks),
        in_specs=[
            pl.BlockSpec((IDX_HEADS, tq, IDX_DIM), lambda b, j: (0, q_rows(b, j), 0)),
            pl.BlockSpec((tq, IDX_HEADS), lambda b, j: (q_rows(b, j), 0)),
            pl.BlockSpec((pl.Element(s_eff), pl.Element(IDX_DIM)), lambda b, j: (b * seq, 0),
                         pipeline_mode=pl.Buffered(1)),
            pl.BlockSpec((N_HEADS, tq, HEAD_DIM), lambda b, j: (0, q_rows(b, j), 0)),
            keys(N_KV_HEADS),
            keys(N_KV_HEADS),
            pl.BlockSpec((tq, D_MODEL), lambda b, j: (q_rows(b, j), 0)),
            _full_spec((D_MODEL, D_MODEL), pipeline_mode=pl.Buffered(1)),
        ],
        out_specs=pl.BlockSpec((tq, D_MODEL), lambda b, j: (q_rows(b, j), 0)),
        out_shape=jax.ShapeDtypeStruct(x_all.shape, F32),
        input_output_aliases={6: 0},
        scratch_shapes=[pltpu.VMEM((tq, s_eff), I32), pltpu.VMEM((tq, s_eff), F32)],
        compiler_params=pltpu.CompilerParams(
            dimension_semantics=("arbitrary", "arbitrary"), vmem_limit_bytes=VMEM_LIMIT),
        name=f"dsa_prompt_{s_eff}",
    )(qi, wi, ki, q, k, v, x_all, w_out)


def _dsa_prompt(x_all, proj, w_out, n_batch, seq, class_rows=DSA_CLASS):
    q, k, v, qi, ki, wi = proj
    n_keep = min(TOPK_MAX, seq // 4)
    wo_b = w_out.astype(BF16)
    per_call = class_rows // DSA_TQ
    for c in range(seq // class_rows):
        x_all = _dsa_prompt_call(x_all, qi, wi, ki, q, k, v, wo_b, n_batch=n_batch, seq=seq,
                                 s_eff=(c + 1) * class_rows, q_block0=c * per_call, n_qblocks=per_call,
                                 n_keep=n_keep)
    return x_all


PAGE_GROUP = 16


def _dsa_sample_kernel(pt_ref, *refs, n_keep, n_pages):
    del pt_ref
    pg = PAGE_GROUP
    ki_pages, k_pages, v_pages = refs[:pg], refs[pg:2 * pg], refs[2 * pg:3 * pg]
    (qi_ref, wi_ref, kin_ref, q_ref, kn_ref, vn_ref, x_ref, wo_ref, out_ref,
     score_ref, bias_ref, bnew_ref, m_ref, l_ref, acc_ref) = refs[3 * pg:]
    phase = pl.program_id(1)
    j = pl.program_id(2)
    last = pl.num_programs(2) - 1
    total = lambda a: jnp.sum(jnp.sum(a, axis=1, keepdims=True), axis=0, keepdims=True)
    head_group = lax.broadcasted_iota(I32, (N_HEADS, 1), 0) // GROUP

    def weigh(s):
        return jnp.sum(wi_ref[...] * jnp.maximum(s, 0.0), axis=0, keepdims=True) * IDX_SCALE

    def select():
        kin = kin_ref[...].astype(BF16).astype(F32)
        s_new = weigh(jnp.sum(qi_ref[...].astype(F32) * kin, axis=1, keepdims=True))
        key = _sortable_key(score_ref[...])
        key_new = _sortable_key(s_new)

        def search(it, thr):
            cand = thr + jnp.left_shift(jnp.int32(1), 31 - it)
            cnt = total(jnp.where(key >= cand, 1, 0)) + jnp.where(key_new >= cand, 1, 0)
            return jnp.where(cnt >= n_keep, cand, thr)

        thr = lax.fori_loop(0, 32, search, jnp.full((1, 1), INT32_MIN, I32))
        n_gt = total(jnp.where(key > thr, 1, 0)) + jnp.where(key_new > thr, 1, 0)
        need = (n_keep - n_gt).astype(F32)
        tie = jnp.where(key == thr, 1.0, 0.0)
        r_i = lax.broadcasted_iota(I32, (LANES, LANES), 0)
        c_i = lax.broadcasted_iota(I32, (LANES, LANES), 1)
        before = jnp.where(r_i < c_i, 1.0, 0.0).astype(BF16)
        rr = lax.broadcasted_iota(I32, (n_pages, n_pages), 0)
        cc = lax.broadcasted_iota(I32, (n_pages, n_pages), 1)
        rows_above = jnp.where(cc < rr, 1.0, 0.0).astype(BF16)
        tie_b = tie.astype(BF16)
        rank = (jnp.dot(tie_b, before, preferred_element_type=F32)
                + jnp.sum(jnp.dot(rows_above, tie_b, preferred_element_type=F32), axis=1, keepdims=True))
        keep = jnp.where(key > thr, 1.0, jnp.where(rank < need, tie, 0.0))
        bias_ref[...] = jnp.where(keep > 0.0, 0.0, -jnp.inf)
        keep_new = jnp.where(key_new > thr, 1.0,
                             jnp.where(key_new == thr, jnp.where(total(tie) < need, 1.0, 0.0), 0.0))
        bnew_ref[...] = jnp.where(keep_new > 0.0, 0.0, -jnp.inf)

    @pl.when(phase == 0)
    def _():
        for i in range(pg):
            s = jnp.dot(qi_ref[...], ki_pages[i][...].astype(BF16), preferred_element_type=F32)
            score_ref[pl.ds(j * pg + i, 1), :] = weigh(s)
        pl.when(j == last)(select)

    def flash_update(logits, pv_of):
        m_old = m_ref[...]
        m_new = jnp.maximum(m_old, jnp.max(logits, axis=1, keepdims=True))
        m_safe = jnp.where(m_new == -jnp.inf, 0.0, m_new)
        alpha = jnp.exp(m_old - m_safe)
        p = jnp.exp(logits - m_safe)
        m_ref[...] = m_new
        l_ref[...] = alpha * l_ref[...] + jnp.sum(p, axis=1, keepdims=True)
        acc_ref[...] = alpha * acc_ref[...] + pv_of(p)

    def by_group(per_kv_head):
        out = per_kv_head(0)
        for hk in range(1, N_KV_HEADS):
            out = jnp.where(head_group == hk, per_kv_head(hk), out)
        return out

    @pl.when(phase == 1)
    def _():
        @pl.when(j == 0)
        def _():
            m_ref[...] = jnp.full(m_ref.shape, -jnp.inf, F32)
            l_ref[...] = jnp.zeros(l_ref.shape, F32)
            acc_ref[...] = jnp.zeros(acc_ref.shape, F32)

        q = q_ref[...]
        logits = jnp.concatenate(
            [by_group(lambda hk: jnp.dot(q, k_pages[i][hk].astype(BF16), preferred_element_type=F32))
             + bias_ref[pl.ds(j * pg + i, 1), :] for i in range(pg)], axis=1)

        def pv_of(p):
            p = p.astype(BF16)
            page = lambda i: p[:, i * PAGE_SIZE:(i + 1) * PAGE_SIZE]
            return by_group(lambda hk: sum(
                _nt_dot(page(i), v_pages[i][hk].astype(BF16), preferred_element_type=F32) for i in range(pg)))

        flash_update(logits, pv_of)

        @pl.when(j == last)
        def _():
            rows = lambda ref: by_group(lambda hk: ref[hk:hk + 1, :]).astype(BF16).astype(F32)
            logit_new = jnp.sum(q.astype(F32) * rows(kn_ref), axis=1, keepdims=True) + bnew_ref[...]
            flash_update(logit_new, lambda p: p.astype(BF16).astype(F32) * rows(vn_ref))
            o = (acc_ref[...] / l_ref[...]).astype(BF16)
            y = x_ref[...]
            for h in range(N_HEADS):
                o_h = jnp.broadcast_to(o[h:h + 1, :], (2 * SUBLANES, HEAD_DIM))
                y = y + jnp.dot(o_h, wo_ref[h], preferred_element_type=F32)[0:1]
            out_ref[...] = y


def _dsa_sample(x_s, q_s, qi_s, wi_s, ki_new, k_new, v_new, cache_k, cache_v, cache_idx_k, page_table, w_out):
    n_dec = x_s.shape[0]
    n_pages = page_table.shape[1]
    n_pool = cache_k.shape[0]
    pg = PAGE_GROUP
    n_groups = n_pages // pg
    n_keep = min(TOPK_MAX, (n_pages * PAGE_SIZE + 1) // 4)
    def page_spec(page_shape, i, live_phase):
        def index(b, ph, j, pt):
            idle = i if live_phase == 1 else (n_groups - 1) * pg + i
            return (pt[b, jnp.where(ph == live_phase, j * pg + i, idle)],) + (0,) * len(page_shape)
        return pl.BlockSpec((None,) + page_shape, index)

    idx_page = (IDX_DIM, PAGE_SIZE)
    kv_page = (N_KV_HEADS, HEAD_DIM, PAGE_SIZE)
    cik = cache_idx_k.transpose(0, 2, 1)
    ck = cache_k.transpose(0, 2, 3, 1)
    cv = cache_v.transpose(0, 2, 3, 1)

    per_seq = lambda *shape: pl.BlockSpec((None,) + shape, lambda b, ph, j, pt: (b,) + (0,) * len(shape))
    grid_spec = pltpu.PrefetchScalarGridSpec(
        num_scalar_prefetch=1,
        grid=(n_dec, 2, n_groups),
        in_specs=([page_spec(idx_page, i, 0) for i in range(pg)]
                  + [page_spec(kv_page, i, 1) for i in range(pg)]
                  + [page_spec(kv_page, i, 1) for i in range(pg)]
                  + [per_seq(IDX_HEADS, IDX_DIM), per_seq(IDX_HEADS, 1), per_seq(1, IDX_DIM),
                     per_seq(N_HEADS, HEAD_DIM), per_seq(N_KV_HEADS, HEAD_DIM), per_seq(N_KV_HEADS, HEAD_DIM),
                     per_seq(1, D_MODEL),
                     pl.BlockSpec((N_HEADS, HEAD_DIM, D_MODEL), lambda b, ph, j, pt: (0, 0, 0))]),
        out_specs=per_seq(1, D_MODEL),
        scratch_shapes=[pltpu.VMEM((n_pages, PAGE_SIZE), F32), pltpu.VMEM((n_pages, PAGE_SIZE), F32),
                        pltpu.VMEM((1, 1), F32), pltpu.VMEM((N_HEADS, 1), F32), pltpu.VMEM((N_HEADS, 1), F32),
                        pltpu.VMEM((N_HEADS, HEAD_DIM), F32)],
    )
    out = pl.pallas_call(
        functools.partial(_dsa_sample_kernel, n_keep=n_keep, n_pages=n_pages),
        grid_spec=grid_spec,
        out_shape=jax.ShapeDtypeStruct((n_dec, 1, D_MODEL), F32),
        compiler_params=pltpu.CompilerParams(
            dimension_semantics=("arbitrary", "arbitrary", "arbitrary"), vmem_limit_bytes=VMEM_LIMIT),
        name="dsa_sample",
    )(page_table, *([cik] * pg), *([ck] * pg), *([cv] * pg),
      qi_s, wi_s.reshape(n_dec, IDX_HEADS, 1), ki_new.reshape(n_dec, 1, IDX_DIM), q_s,
      k_new.reshape(n_dec, N_KV_HEADS, HEAD_DIM), v_new.reshape(n_dec, N_KV_HEADS, HEAD_DIM),
      x_s.reshape(n_dec, 1, D_MODEL), w_out.astype(BF16).reshape(N_HEADS, HEAD_DIM, D_MODEL))
    return out.reshape(n_dec, D_MODEL)


def kernel(x_prompt, x_sample, cache_k, cache_v, cache_idx_k, page_table, norm_mix_g, norm_ffn_g,
           gmlp_w_in, gmlp_v_g, gmlp_w_s, gmlp_b_s, gmlp_w_out,
           att_w_in, att_q_norm_g, att_k_norm_g, att_idx_k_norm_g, att_w_out,
           peer_w_q, peer_sub_k1, peer_sub_k2, peer_u, peer_v):
    n_batch, seq, _ = x_prompt.shape
    n_dec, dec_seq, _ = x_sample.shape
    assert dec_seq == 1 and seq % DSA_CLASS == 0 and (n_batch * seq) % ROW_BLOCK == 0
    n_prompt = n_batch * seq
    n_tok = n_prompt + n_dec
    n_rows = n_prompt + ROW_BLOCK
    assert n_dec <= ROW_BLOCK

    def peer(x, layer):
        return _peer_layer(x, norm_ffn_g[layer], peer_w_q[layer], peer_sub_k1[layer], peer_sub_k2[layer],
                           peer_u[layer], peer_v[layer])

    x_all = jnp.concatenate([x_prompt.reshape(n_prompt, D_MODEL), x_sample.reshape(n_dec, D_MODEL),
                             jnp.zeros((n_rows - n_tok, D_MODEL), F32)], axis=0)
    x_all, v_new = _gmlp_layer(x_all, n_prompt, norm_mix_g[0], gmlp_w_in[0], gmlp_v_g[0], gmlp_w_s[0],
                               gmlp_b_s[0], gmlp_w_out[0])
    x_all = peer(x_all, 0)
    pos = jnp.concatenate([jnp.tile(jnp.arange(seq), n_batch), jnp.full((n_rows - n_prompt,), PAST_LEN)])
    q, k, v, qi, ki, wi, new_k, new_v, new_ki = _att_project(
        x_all, pos, norm_mix_g[1], att_w_in[0], att_q_norm_g[0], att_k_norm_g[0], att_idx_k_norm_g[0])
    smp = slice(n_prompt, n_tok)
    heads = lambda a: a[:, smp].transpose(1, 0, 2)
    xs = _dsa_sample(x_all[smp], heads(q), heads(qi), wi[smp], new_ki[smp], new_k[smp], new_v[smp],
                     cache_k[0], cache_v[0], cache_idx_k[0], page_table, att_w_out[0])
    x_all = _dsa_prompt(x_all, (q, k, v, qi, ki, wi), att_w_out[0], n_batch, seq)
    x_all = lax.dynamic_update_slice(x_all, xs, (n_prompt, 0))
    x_all = peer(x_all, 1)

    kv_shape = (1, n_batch, seq, N_KV_HEADS, HEAD_DIM)
    kv_s_shape = (1, n_dec, 1, N_KV_HEADS, HEAD_DIM)
    return (x_all[:n_prompt].reshape(n_batch, seq, D_MODEL), x_all[smp].reshape(n_dec, 1, D_MODEL),
            new_k[:n_prompt].reshape(kv_shape), new_v[:n_prompt].reshape(kv_shape),
            new_ki[:n_prompt].reshape(1, n_batch, seq, IDX_DIM),
            new_k[smp].reshape(kv_s_shape), new_v[smp].reshape(kv_s_shape),
            new_ki[smp].reshape(1, n_dec, 1, IDX_DIM), v_new[:n_dec].reshape(1, n_dec, 1, GMLP_WIDTH))
```

```python
import functools
import math

import jax
import jax.numpy as jnp
import numpy as np
from jax import lax
from jax.experimental import pallas as pl
from jax.experimental.pallas import tpu as pltpu

F32 = jnp.float32
BF16 = jnp.bfloat16
I32 = jnp.int32

D_MODEL = 1024
EPS = 1e-6

LANES = 128
SUBLANES = 8
MIB = 1024 * 1024
VMEM_LIMIT = 56 * MIB

ROW_BLOCK = 256

PEER_HEADS = 8
N_KEYS = 128
N_EXPERTS = N_KEYS * N_KEYS
PEER_TOPK = 16
PEER_KEY_DIM = 256
PEER_HALF = PEER_KEY_DIM // 2
PEER_PICKS = PEER_HEADS * PEER_TOPK
ROW_WORDS = D_MODEL // 2
ROW_SUB = ROW_WORDS // LANES
PEER_TOKENS = 128
N_CAND = sum(PEER_TOPK // (a + 1) for a in range(PEER_TOPK))

CHUNK = 128
GMLP_WIDTH = 2 * D_MODEL
GMLP_GROUPS = 8
GMLP_GROUP_DIM = GMLP_WIDTH // GMLP_GROUPS

PAST_LEN = 16384
PAGE_SIZE = 128
N_HEADS = 16
HEAD_DIM = D_MODEL // N_HEADS
N_KV_HEADS = 4
GROUP = N_HEADS // N_KV_HEADS
IDX_HEADS = 8
IDX_DIM = 64
TOPK_MAX = 256
ROPE_THETA = 10000.0
IDX_SCALE = (IDX_DIM ** -0.5) * (IDX_HEADS ** -0.5)
ATT_SIZES = (N_HEADS * HEAD_DIM, N_KV_HEADS * HEAD_DIM, N_KV_HEADS * HEAD_DIM,
             IDX_HEADS * IDX_DIM, IDX_DIM, IDX_HEADS)
ATT_COLS = sum(ATT_SIZES)
ATT_COLS_PAD = -(-ATT_COLS // LANES) * LANES
Q_COL, K_COL, V_COL, QI_COL, KI_COL = (int(c) for c in np.cumsum((0,) + ATT_SIZES[:4]))


def _nt_dot(a, b, **kw):
    return lax.dot_general(a, b, (((1,), (1,)), ((), ())), **kw)


def _rms_rows(x, g):
    return x * lax.rsqrt(jnp.mean(x * x, axis=-1, keepdims=True) + EPS) * g


def _full_spec(shape, **kw):
    return pl.BlockSpec(shape, lambda *_: (0,) * len(shape), **kw)


def _topk_rows(s, k, iota):
    n_rows = s.shape[0]
    vals, idxs = [], []
    for _ in range(k):
        parts = [(s[r:r + SUBLANES], iota[r:r + SUBLANES]) for r in range(0, n_rows, SUBLANES)]
        while len(parts) > 1:
            merged = []
            for a in range(0, len(parts), 2):
                (va, ia), (vb, ib) = parts[a], parts[a + 1]
                take = vb > va
                merged.append((jnp.where(take, vb, va), jnp.where(take, ib, ia)))
            parts = merged
        v8, i8 = parts[0]
        m = jnp.max(v8, axis=0, keepdims=True)
        first = jnp.min(jnp.where(v8 == m, i8, n_rows), axis=0, keepdims=True)
        vals.append(m)
        idxs.append(first)
        s = jnp.where(iota == first, -jnp.inf, s)
    return jnp.concatenate(vals, axis=0), jnp.concatenate(idxs, axis=0)


def _peer_router_kernel(x_ref, g_ref, wq_ref, k1_ref, k2_ref, h_ref, eidx_ref, gate_ref):
    h = _rms_rows(x_ref[...], g_ref[...])
    h_ref[...] = h
    q = jnp.dot(h.astype(BF16), wq_ref[...], preferred_element_type=F32)
    n_tok = h.shape[0]
    iota_k = lax.broadcasted_iota(I32, (N_KEYS, n_tok), 0)
    iota_c = lax.broadcasted_iota(I32, (N_CAND, n_tok), 0)
    k1 = k1_ref[...]
    k2 = k2_ref[...]
    eidx_rows, gate_rows = [], []
    for head in range(PEER_HEADS):
        base = head * PEER_KEY_DIM
        q1 = q[:, base:base + PEER_HALF].astype(BF16)
        q2 = q[:, base + PEER_HALF:base + PEER_KEY_DIM].astype(BF16)
        s1 = _nt_dot(k1, q1, preferred_element_type=F32)
        s2 = _nt_dot(k2, q2, preferred_element_type=F32)
        v1, i1 = _topk_rows(s1, PEER_TOPK, iota_k)
        v2, i2 = _topk_rows(s2, PEER_TOPK, iota_k)
        width = [PEER_TOPK // (a + 1) for a in range(PEER_TOPK)]
        cand = jnp.concatenate([v1[a:a + 1] + v2[:width[a]] for a in range(PEER_TOPK)], axis=0)
        cidx = jnp.concatenate([(i1[a:a + 1] * N_KEYS + i2[:width[a]]) * ROW_SUB for a in range(PEER_TOPK)], axis=0)
        top_s, top_e = [], []
        for _ in range(PEER_TOPK):
            m = jnp.max(cand, axis=0, keepdims=True)
            first = jnp.min(jnp.where(cand == m, iota_c, N_CAND), axis=0, keepdims=True)
            hit = iota_c == first
            top_s.append(m)
            top_e.append(jnp.sum(jnp.where(hit, cidx, 0), axis=0, keepdims=True))
            cand = jnp.where(hit, -jnp.inf, cand)
        top_s = jnp.concatenate(top_s, axis=0)
        e = jnp.exp(top_s - top_s[0:1])
        gate_rows.append(e / jnp.sum(e, axis=0, keepdims=True))
        eidx_rows.append(jnp.concatenate(top_e, axis=0))
    gate_t = jnp.concatenate(gate_rows, axis=0)
    eidx_t = jnp.concatenate(eidx_rows, axis=0)
    gate_ref[...] = gate_t.T
    eidx_ref[...] = lax.bitcast_convert_type(lax.bitcast_convert_type(eidx_t, F32).T, I32)


def _peer_router(x, g, w_q, sub_k1, sub_k2):
    n = x.shape[0]
    tm = PEER_TOKENS
    return pl.pallas_call(
        _peer_router_kernel,
        grid=(n // tm,),
        in_specs=[
            pl.BlockSpec((tm, D_MODEL), lambda i: (i, 0)),
            _full_spec((1, D_MODEL)),
            _full_spec((D_MODEL, PEER_HEADS * PEER_KEY_DIM)),
            _full_spec((N_KEYS, PEER_HALF)),
            _full_spec((N_KEYS, PEER_HALF)),
        ],
        out_specs=[
            pl.BlockSpec((tm, D_MODEL), lambda i: (i, 0)),
            pl.BlockSpec((tm, PEER_PICKS), lambda i: (i, 0)),
            pl.BlockSpec((tm, PEER_PICKS), lambda i: (i, 0)),
        ],
        out_shape=[
            jax.ShapeDtypeStruct((n, D_MODEL), F32),
            jax.ShapeDtypeStruct((n, PEER_PICKS), I32),
            jax.ShapeDtypeStruct((n, PEER_PICKS), F32),
        ],
        compiler_params=pltpu.CompilerParams(dimension_semantics=("arbitrary",), vmem_limit_bytes=VMEM_LIMIT),
        name="peer_router",
    )(x, g.reshape(1, D_MODEL), w_q.astype(BF16), sub_k1.astype(BF16), sub_k2.astype(BF16))


PACK_ROWS = 512


def _pack_kernel(w_ref, out_ref):
    w = w_ref[...]
    as_bits = lambda a: lax.bitcast_convert_type(a.astype(jnp.bfloat16).astype(F32), I32)
    hi = as_bits(w[:, :ROW_WORDS]) & jnp.int32(-65536)
    lo = (as_bits(w[:, ROW_WORDS:]) >> 16) & jnp.int32(0xFFFF)
    words = hi | lo
    for c in range(ROW_SUB):
        out_ref[:, c, :] = words[:, c * LANES:(c + 1) * LANES]


def _pack_table(w):
    words = pl.pallas_call(
        _pack_kernel,
        grid=(N_EXPERTS // PACK_ROWS,),
        in_specs=[pl.BlockSpec((PACK_ROWS, D_MODEL), lambda i: (i, 0))],
        out_specs=pl.BlockSpec((PACK_ROWS, ROW_SUB, LANES), lambda i: (i, 0, 0)),
        out_shape=jax.ShapeDtypeStruct((N_EXPERTS, ROW_SUB, LANES), I32),
        compiler_params=pltpu.CompilerParams(dimension_semantics=("arbitrary",)),
        name="peer_pack_table",
    )(w)
    return words.reshape(N_EXPERTS * ROW_SUB, LANES)


PLANE_STRIDE = PEER_PICKS + SUBLANES
GATHER_ROWS = ROW_SUB * PLANE_STRIDE
TOKEN_GROUP = 2
N_GATHER_BUFS = 2 * TOKEN_GROUP


def _gather_token(eidx_ref, tab_ref, buf_ref, t):
    for p in range(PEER_PICKS):
        row0 = pl.multiple_of(eidx_ref[t, p], ROW_SUB)
        buf_ref[pl.ds(p, ROW_SUB, stride=PLANE_STRIDE), :] = tab_ref[pl.ds(row0, ROW_SUB), :]


def _gathered_rows(buf_ref):
    hi, lo = [], []
    for s in range(ROW_SUB):
        w = buf_ref[s * PLANE_STRIDE:s * PLANE_STRIDE + PEER_PICKS, :]
        hi.append(lax.bitcast_convert_type(w & jnp.int32(-65536), F32).astype(BF16))
        lo.append(lax.bitcast_convert_type(w << 16, F32).astype(BF16))
    return jnp.concatenate(hi + lo, axis=1)


def _token_pipeline(eidx_ref, tab_ref, bufs, n_tok, compute):
    g = TOKEN_GROUP
    groups = (bufs[:g], bufs[g:])
    for k, buf in enumerate(bufs):
        _gather_token(eidx_ref, tab_ref, buf, k)

    def step(i, carry):
        for j, group in enumerate(groups):
            t = 2 * g * i + g * j
            compute(t, group)
            for k, buf in enumerate(group):
                _gather_token(eidx_ref, tab_ref, buf, jnp.minimum(t + 2 * g + k, n_tok - 1))
        return carry

    lax.fori_loop(0, n_tok // (2 * g), step, 0)


def _group_rows(group):
    return jnp.concatenate([_gathered_rows(buf) for buf in group], axis=0)


def _token_rows(ref, t):
    return [ref[pl.ds(t + k, 1), :] for k in range(TOKEN_GROUP)]


def _own_block(n_rows):
    r = lax.broadcasted_iota(I32, (n_rows, TOKEN_GROUP * PEER_PICKS), 0)
    c = lax.broadcasted_iota(I32, (n_rows, TOKEN_GROUP * PEER_PICKS), 1)
    return r == c // PEER_PICKS


def _peer_act_kernel(eidx_ref, tab_ref, h_ref, gate_ref, coef_ref, *bufs):
    g = TOKEN_GROUP
    pad = jnp.zeros((2 * SUBLANES - g, D_MODEL), F32)

    def compute(t, group):
        x = jnp.concatenate(_token_rows(h_ref, t) + [pad], axis=0).astype(BF16)
        dots = _nt_dot(x, _group_rows(group), preferred_element_type=F32)
        act = jnp.concatenate([dots[k:k + 1, k * PEER_PICKS:(k + 1) * PEER_PICKS] for k in range(g)], axis=0)
        coef = jnp.concatenate(_token_rows(gate_ref, t), axis=0) * jax.nn.gelu(act)
        for k in range(g):
            coef_ref[pl.ds(t + k, 1), :] = coef[k:k + 1]

    _token_pipeline(eidx_ref, tab_ref, bufs, h_ref.shape[0], compute)


def _peer_out_kernel(eidx_ref, tab_ref, coef_ref, x_ref, out_ref, *bufs):
    g = TOKEN_GROUP
    own = _own_block(g)
    pad = jnp.zeros((2 * SUBLANES - g, g * PEER_PICKS), F32)

    def compute(t, group):
        c = jnp.concatenate(_token_rows(coef_ref, t), axis=0)
        lhs = jnp.concatenate([jnp.where(own, jnp.concatenate([c] * g, axis=1), 0.0), pad], axis=0)
        y = jnp.dot(lhs.astype(BF16), _group_rows(group), preferred_element_type=F32)
        for k in range(g):
            out_ref[pl.ds(t + k, 1), :] = x_ref[pl.ds(t + k, 1), :] + y[k:k + 1]

    _token_pipeline(eidx_ref, tab_ref, bufs, x_ref.shape[0], compute)


def _peer_gather_call(body, name, eidx, tab, a, b, out_cols):
    n = eidx.shape[0]
    tb = ROW_BLOCK
    rows = lambda cols: pl.BlockSpec((tb, cols), lambda i: (i, 0))
    return pl.pallas_call(
        body,
        grid=(n // tb,),
        in_specs=[
            pl.BlockSpec((tb, PEER_PICKS), lambda i: (i, 0), memory_space=pltpu.SMEM),
            _full_spec((N_EXPERTS * ROW_SUB, LANES), pipeline_mode=pl.Buffered(1)),
            rows(a.shape[1]),
            rows(b.shape[1]),
        ],
        out_specs=rows(out_cols),
        out_shape=jax.ShapeDtypeStruct((n, out_cols), F32),
        scratch_shapes=[pltpu.VMEM((GATHER_ROWS, LANES), I32) for _ in range(N_GATHER_BUFS)],
        compiler_params=pltpu.CompilerParams(dimension_semantics=("arbitrary",), vmem_limit_bytes=VMEM_LIMIT),
        name=name,
    )(eidx, tab, a, b)


def _peer_layer(x, g, w_q, sub_k1, sub_k2, expert_u, expert_v):
    h, eidx, gate = _peer_router(x, g, w_q, sub_k1, sub_k2)
    coef = _peer_gather_call(_peer_act_kernel, "peer_act", eidx, _pack_table(expert_u), h, gate, PEER_PICKS)
    return _peer_gather_call(_peer_out_kernel, "peer_out", eidx, _pack_table(expert_v), coef, x, D_MODEL)


def _gmlp_kernel(x_ref, g_ref, win_ref, vg_ref, ws_ref, bst_ref, s0_ref, b0_ref, wout_ref,
                 out_ref, vnew_ref, gated_ref, *, n_chunk_blocks):
    x = x_ref[...]
    h = _rms_rows(x, g_ref[...])
    uv = jnp.dot(h.astype(BF16), win_ref[...], preferred_element_type=F32)
    v = _rms_rows(uv[:, GMLP_WIDTH:], vg_ref[...])
    is_chunked = pl.program_id(0) < n_chunk_blocks

    @pl.when(is_chunked)
    def _():
        r_i = lax.broadcasted_iota(I32, (CHUNK, CHUNK), 0)
        c_i = lax.broadcasted_iota(I32, (CHUNK, CHUNK), 1)
        causal = c_i <= r_i
        for g in range(GMLP_GROUPS):
            cols = slice(g * GMLP_GROUP_DIM, (g + 1) * GMLP_GROUP_DIM)
            ws = jnp.where(causal, ws_ref[g], 0.0).astype(BF16)
            bias = bst_ref[:, g:g + 1]
            for c in range(x.shape[0] // CHUNK):
                rows = slice(c * CHUNK, (c + 1) * CHUNK)
                mixed = jnp.dot(ws, v[rows, cols].astype(BF16), preferred_element_type=F32) + bias
                gated_ref[rows, cols] = (uv[rows, cols] * mixed).astype(BF16)

    @pl.when(jnp.logical_not(is_chunked))
    def _():
        gated_ref[...] = (uv[:, :GMLP_WIDTH] * (v * s0_ref[...] + b0_ref[...])).astype(BF16)
        vnew_ref[...] = v

    out_ref[...] = x + jnp.dot(gated_ref[...], wout_ref[...], preferred_element_type=F32)


def _gmlp_layer(x_all, n_chunk_rows, g, w_in, v_g, w_s, b_s, w_out):
    n = x_all.shape[0]
    tb = ROW_BLOCK
    spread = lambda a: jnp.repeat(a, GMLP_GROUP_DIM).reshape(1, GMLP_WIDTH)
    body = functools.partial(_gmlp_kernel, n_chunk_blocks=n_chunk_rows // tb)
    return pl.pallas_call(
        body,
        grid=(n // tb,),
        in_specs=[
            pl.BlockSpec((tb, D_MODEL), lambda i: (i, 0)),
            _full_spec((1, D_MODEL)),
            _full_spec((D_MODEL, 2 * GMLP_WIDTH), pipeline_mode=pl.Buffered(1)),
            _full_spec((1, GMLP_WIDTH)),
            _full_spec((GMLP_GROUPS, CHUNK, CHUNK)),
            _full_spec((CHUNK, GMLP_GROUPS)),
            _full_spec((1, GMLP_WIDTH)),
            _full_spec((1, GMLP_WIDTH)),
            _full_spec((GMLP_WIDTH, D_MODEL), pipeline_mode=pl.Buffered(1)),
        ],
        out_specs=[
            pl.BlockSpec((tb, D_MODEL), lambda i: (i, 0)),
            _full_spec((tb, GMLP_WIDTH)),
        ],
        out_shape=[
            jax.ShapeDtypeStruct((n, D_MODEL), F32),
            jax.ShapeDtypeStruct((tb, GMLP_WIDTH), F32),
        ],
        scratch_shapes=[pltpu.VMEM((tb, GMLP_WIDTH), BF16)],
        compiler_params=pltpu.CompilerParams(dimension_semantics=("arbitrary",), vmem_limit_bytes=VMEM_LIMIT),
        name="gmlp_mixer",
    )(x_all, g.reshape(1, D_MODEL), w_in.astype(BF16), v_g.reshape(1, GMLP_WIDTH), w_s, b_s.T,
      spread(w_s[:, 0, 0]), spread(b_s[:, 0]), w_out.astype(BF16))


def _att_proj_kernel(x_ref, g_ref, w_ref, qg_ref, kg_ref, ikg_ref, cos_ref, sin_ref,
                     q_ref, k_ref, v_ref, qi_ref, ki_ref, wi_ref, nk_ref, nv_ref, nki_ref):
    h = _rms_rows(x_ref[...], g_ref[...])
    p = jnp.dot(h.astype(BF16), w_ref[...], preferred_element_type=F32)
    n_tok = p.shape[0]
    cos = cos_ref[...]
    sin = sin_ref[...]
    lane = lax.broadcasted_iota(I32, (n_tok, LANES), 1)
    first_half = (lane & (HEAD_DIM // 2)) == 0
    r_i = lax.broadcasted_iota(I32, (LANES, LANES), 0)
    c_i = lax.broadcasted_iota(I32, (LANES, LANES), 1)
    head_mean = jnp.where((r_i // HEAD_DIM) == (c_i // HEAD_DIM), 1.0 / HEAD_DIM, 0.0).astype(BF16)

    def slab(col):
        return p[:, col:col + LANES]

    def head_rms(a, gain):
        ms = jnp.dot((a * a).astype(BF16), head_mean, preferred_element_type=F32)
        return a * lax.rsqrt(ms + EPS) * gain

    def rope(a):
        partner = jnp.where(first_half, pltpu.roll(a, LANES - HEAD_DIM // 2, 1), pltpu.roll(a, HEAD_DIM // 2, 1))
        return a * cos + partner * sin

    def put_heads(ref, i, a):
        ref[2 * i] = a[:, :HEAD_DIM].astype(BF16)
        ref[2 * i + 1] = a[:, HEAD_DIM:].astype(BF16)

    for i in range(N_HEADS // 2):
        put_heads(q_ref, i, rope(head_rms(slab(Q_COL + i * LANES), qg_ref[...])) * (HEAD_DIM ** -0.5))
    for i in range(N_KV_HEADS // 2):
        k = rope(head_rms(slab(K_COL + i * LANES), kg_ref[...]))
        nk_ref[:, i * LANES:(i + 1) * LANES] = k
        put_heads(k_ref, i, k)
        v = slab(V_COL + i * LANES)
        nv_ref[:, i * LANES:(i + 1) * LANES] = v
        put_heads(v_ref, i, v)
    for i in range(IDX_HEADS // 2):
        put_heads(qi_ref, i, rope(slab(QI_COL + i * LANES)))
    tail = slab(KI_COL)
    ki = rope(head_rms(tail, ikg_ref[...]))[:, :IDX_DIM]
    nki_ref[...] = ki
    ki_ref[...] = ki.astype(BF16)
    wi_ref[...] = tail[:, IDX_DIM:IDX_DIM + IDX_HEADS]


def _att_project(x_all, pos, g, w_in, qn_g, kn_g, ikn_g):
    n = x_all.shape[0]
    tb = ROW_BLOCK
    half = HEAD_DIM // 2
    inv = jnp.exp(-math.log(ROPE_THETA) * jnp.arange(half, dtype=F32) / half)
    ang = pos.astype(F32)[:, None] * inv[None, :]
    cos = jnp.tile(jnp.cos(ang), (1, LANES // half))
    sin = jnp.tile(jnp.concatenate([-jnp.sin(ang), jnp.sin(ang)], axis=1), (1, LANES // HEAD_DIM))
    pair = lambda gain: jnp.tile(gain, LANES // HEAD_DIM).reshape(1, LANES)
    w_pad = jnp.pad(w_in, ((0, 0), (0, ATT_COLS_PAD - ATT_COLS))).astype(BF16)
    heads = lambda n_heads: (pl.BlockSpec((n_heads, tb, HEAD_DIM), lambda i: (0, i, 0)),
                             jax.ShapeDtypeStruct((n_heads, n, HEAD_DIM), BF16))
    rows = lambda cols, dt: (pl.BlockSpec((tb, cols), lambda i: (i, 0)), jax.ShapeDtypeStruct((n, cols), dt))
    outs = [heads(N_HEADS), heads(N_KV_HEADS), heads(N_KV_HEADS), heads(IDX_HEADS), rows(IDX_DIM, BF16),
            rows(IDX_HEADS, F32), rows(N_KV_HEADS * HEAD_DIM, F32), rows(N_KV_HEADS * HEAD_DIM, F32),
            rows(IDX_DIM, F32)]
    return pl.pallas_call(
        _att_proj_kernel,
        grid=(n // tb,),
        in_specs=[
            pl.BlockSpec((tb, D_MODEL), lambda i: (i, 0)),
            _full_spec((1, D_MODEL)),
            _full_spec((D_MODEL, ATT_COLS_PAD), pipeline_mode=pl.Buffered(1)),
            _full_spec((1, LANES)),
            _full_spec((1, LANES)),
            _full_spec((1, LANES)),
            pl.BlockSpec((tb, LANES), lambda i: (i, 0)),
            pl.BlockSpec((tb, LANES), lambda i: (i, 0)),
        ],
        out_specs=[o[0] for o in outs],
        out_shape=[o[1] for o in outs],
        compiler_params=pltpu.CompilerParams(dimension_semantics=("arbitrary",), vmem_limit_bytes=VMEM_LIMIT),
        name="att_project",
    )(x_all, g.reshape(1, D_MODEL), w_pad, pair(qn_g), pair(kn_g), pair(ikn_g), cos, sin)


DSA_TQ = 256
DSA_CLASS = 256
INT32_MIN = -(2 ** 31)


def _sortable_key(score):
    bits = lax.bitcast_convert_type(score, I32)
    return bits ^ ((bits >> 31) & jnp.int32(0x7FFFFFFF))


def _dsa_prompt_kernel(qi_ref, wi_ref, ki_ref, q_ref, k_ref, v_ref, x_ref, wo_ref, out_ref,
                       key_ref, bias_ref, *, s_eff, q_block0, n_keep):
    tq = DSA_TQ
    q0 = (q_block0 + pl.program_id(1)) * tq
    ki = ki_ref[...]
    wi = wi_ref[...]
    score = jnp.zeros((tq, s_eff), F32)
    for h in range(IDX_HEADS):
        s = _nt_dot(qi_ref[h], ki, preferred_element_type=F32)
        score = score + wi[:, h:h + 1] * jnp.maximum(s, 0.0)
    score = score * IDX_SCALE
    q_pos = q0 + lax.broadcasted_iota(I32, (tq, 1), 0)
    k_pos = lax.broadcasted_iota(I32, (tq, s_eff), 1)
    score = jnp.where(k_pos <= q_pos, score, -jnp.inf)
    key_ref[...] = _sortable_key(score)

    def search(it, thr):
        cand = thr + jnp.left_shift(jnp.int32(1), 31 - it)
        cnt = jnp.sum(jnp.where(key_ref[...] >= cand, 1, 0), axis=1, keepdims=True)
        return jnp.where(cnt >= n_keep, cand, thr)

    thr = lax.fori_loop(0, 32, search, jnp.full((tq, 1), INT32_MIN, I32))
    n_gt = jnp.sum(jnp.where(key_ref[...] > thr, 1, 0), axis=1, keepdims=True)
    need = (n_keep - n_gt).astype(F32)
    r_i = lax.broadcasted_iota(I32, (LANES, LANES), 0)
    c_i = lax.broadcasted_iota(I32, (LANES, LANES), 1)
    before = jnp.where(r_i < c_i, 1.0, 0.0).astype(BF16)
    run = jnp.zeros((tq, 1), F32)
    for c in range(s_eff // LANES):
        sl = slice(c * LANES, (c + 1) * LANES)
        key_c = key_ref[:, sl]
        tie = jnp.where(key_c == thr, 1.0, 0.0)
        rank = run + jnp.dot(tie.astype(BF16), before, preferred_element_type=F32)
        keep_tie = jnp.where(rank < need, tie, 0.0)
        keep = jnp.where(key_c > thr, 1.0, keep_tie)
        adm = (c * LANES + lax.broadcasted_iota(I32, (tq, LANES), 1)) <= q_pos
        bias_ref[:, sl] = jnp.where(adm, jnp.where(keep > 0.0, 0.0, -jnp.inf), -jnp.inf)
        run = run + jnp.sum(tie, axis=1, keepdims=True)

    outs = []
    for hk in range(N_KV_HEADS):
        k_h = k_ref[hk]
        v_h = v_ref[hk]
        for g in range(GROUP):
            logits = _nt_dot(q_ref[hk * GROUP + g], k_h, preferred_element_type=F32) + bias_ref[...]
            m = jnp.max(logits, axis=1, keepdims=True)
            p = jnp.exp(logits - m)
            den = jnp.sum(p, axis=1, keepdims=True)
            o = jnp.dot(p.astype(BF16), v_h, preferred_element_type=F32)
            outs.append(o / den)
    o_all = jnp.concatenate(outs, axis=1)
    out_ref[...] = x_ref[...] + jnp.dot(o_all.astype(BF16), wo_ref[...], preferred_element_type=F32)


def _dsa_prompt_call(x_all, qi, wi, ki, q, k, v, w_out, *, n_batch, seq, s_eff, q_block0, n_qblocks, n_keep):
    tq = DSA_TQ
    per_seq = seq // tq
    q_rows = lambda b, j: b * per_seq + q_block0 + j
    keys = lambda n_heads: pl.BlockSpec((pl.Element(n_heads), pl.Element(s_eff), pl.Element(HEAD_DIM)),
                                        lambda b, j: (0, b * seq, 0), pipeline_mode=pl.Buffered(1))
    body = functools.partial(_dsa_prompt_kernel, s_eff=s_eff, q_block0=q_block0, n_keep=n_keep)
    return pl.pallas_call(
        body,
        grid=(n_batch, n_qblocks),
        in_specs=[
            pl.BlockSpec((IDX_HEADS, tq, IDX_DIM), lambda b, j: (0, q_rows(b, j), 0)),
            pl.BlockSpec((tq, IDX_HEADS), lambda b, j: (q_rows(b, j), 0)),
            pl.BlockSpec((pl.Element(s_eff), pl.Element(IDX_DIM)), lambda b, j: (b * seq, 0),
                         pipeline_mode=pl.Buffered(1)),
            pl.BlockSpec((N_HEADS, tq, HEAD_DIM), lambda b, j: (0, q_rows(b, j), 0)),
            keys(N_KV_HEADS),
            keys(N_KV_HEADS),
            pl.BlockSpec((tq, D_MODEL), lambda b, j: (q_rows(b, j), 0)),
            _full_spec((D_MODEL, D_MODEL), pipeline_mode=pl.Buffered(1)),
        ],
        out_specs=pl.BlockSpec((tq, D_MODEL), lambda b, j: (q_rows(b, j), 0)),
        out_shape=jax.ShapeDtypeStruct(x_all.shape, F32),
        input_output_aliases={6: 0},
        scratch_shapes=[pltpu.VMEM((tq, s_eff), I32), pltpu.VMEM((tq, s_eff), F32)],
        compiler_params=pltpu.CompilerParams(
            dimension_semantics=("arbitrary", "arbitrary"), vmem_limit_bytes=VMEM_LIMIT),
        name=f"dsa_prompt_{s_eff}",
    )(qi, wi, ki, q, k, v, x_all, w_out)


def _dsa_prompt(x_all, proj, w_out, n_batch, seq, class_rows=DSA_CLASS):
    q, k, v, qi, ki, wi = proj
    n_keep = min(TOPK_MAX, seq // 4)
    wo_b = w_out.astype(BF16)
    per_call = class_rows // DSA_TQ
    for c in range(seq // class_rows):
        x_all = _dsa_prompt_call(x_all, qi, wi, ki, q, k, v, wo_b, n_batch=n_batch, seq=seq,
                                 s_eff=(c + 1) * class_rows, q_block0=c * per_call, n_qblocks=per_call,
                                 n_keep=n_keep)
    return x_all


PAGE_GROUP = 16


def _dsa_sample_kernel(pt_ref, *refs, n_keep, n_pages):
    del pt_ref
    pg = PAGE_GROUP
    ki_pages, k_pages, v_pages = refs[:pg], refs[pg:2 * pg], refs[2 * pg:3 * pg]
    (qi_ref, wi_ref, kin_ref, q_ref, kn_ref, vn_ref, x_ref, wo_ref, out_ref,
     score_ref, bias_ref, bnew_ref, m_ref, l_ref, acc_ref) = refs[3 * pg:]
    phase = pl.program_id(1)
    j = pl.program_id(2)
    last = pl.num_programs(2) - 1
    total = lambda a: jnp.sum(jnp.sum(a, axis=1, keepdims=True), axis=0, keepdims=True)
    head_group = lax.broadcasted_iota(I32, (N_HEADS, 1), 0) // GROUP

    def weigh(s):
        return jnp.sum(wi_ref[...] * jnp.maximum(s, 0.0), axis=0, keepdims=True) * IDX_SCALE

    def select():
        kin = kin_ref[...].astype(BF16).astype(F32)
        s_new = weigh(jnp.sum(qi_ref[...].astype(F32) * kin, axis=1, keepdims=True))
        key = _sortable_key(score_ref[...])
        key_new = _sortable_key(s_new)

        def search(it, thr):
            cand = thr + jnp.left_shift(jnp.int32(1), 31 - it)
            cnt = total(jnp.where(key >= cand, 1, 0)) + jnp.where(key_new >= cand, 1, 0)
            return jnp.where(cnt >= n_keep, cand, thr)

        thr = lax.fori_loop(0, 32, search, jnp.full((1, 1), INT32_MIN, I32))
        n_gt = total(jnp.where(key > thr, 1, 0)) + jnp.where(key_new > thr, 1, 0)
        need = (n_keep - n_gt).astype(F32)
        tie = jnp.where(key == thr, 1.0, 0.0)
        r_i = lax.broadcasted_iota(I32, (LANES, LANES), 0)
        c_i = lax.broadcasted_iota(I32, (LANES, LANES), 1)
        before = jnp.where(r_i < c_i, 1.0, 0.0).astype(BF16)
        rr = lax.broadcasted_iota(I32, (n_pages, n_pages), 0)
        cc = lax.broadcasted_iota(I32, (n_pages, n_pages), 1)
        rows_above = jnp.where(cc < rr, 1.0, 0.0).astype(BF16)
        tie_b = tie.astype(BF16)
        rank = (jnp.dot(tie_b, before, preferred_element_type=F32)
                + jnp.sum(jnp.dot(rows_above, tie_b, preferred_element_type=F32), axis=1, keepdims=True))
        keep = jnp.where(key > thr, 1.0, jnp.where(rank < need, tie, 0.0))
        bias_ref[...] = jnp.where(keep > 0.0, 0.0, -jnp.inf)
        keep_new = jnp.where(key_new > thr, 1.0,
                             jnp.where(key_new == thr, jnp.where(total(tie) < need, 1.0, 0.0), 0.0))
        bnew_ref[...] = jnp.where(keep_new > 0.0, 0.0, -jnp.inf)

    @pl.when(phase == 0)
    def _():
        for i in range(pg):
            s = jnp.dot(qi_ref[...], ki_pages[i][...].astype(BF16), preferred_element_type=F32)
            score_ref[pl.ds(j * pg + i, 1), :] = weigh(s)
        pl.when(j == last)(select)

    def flash_update(logits, pv_of):
        m_old = m_ref[...]
        m_new = jnp.maximum(m_old, jnp.max(logits, axis=1, keepdims=True))
        m_safe = jnp.where(m_new == -jnp.inf, 0.0, m_new)
        alpha = jnp.exp(m_old - m_safe)
        p = jnp.exp(logits - m_safe)
        m_ref[...] = m_new
        l_ref[...] = alpha * l_ref[...] + jnp.sum(p, axis=1, keepdims=True)
        acc_ref[...] = alpha * acc_ref[...] + pv_of(p)

    def by_group(per_kv_head):
        out = per_kv_head(0)
        for hk in range(1, N_KV_HEADS):
            out = jnp.where(head_group == hk, per_kv_head(hk), out)
        return out

    @pl.when(phase == 1)
    def _():
        @pl.when(j == 0)
        def _():
            m_ref[...] = jnp.full(m_ref.shape, -jnp.inf, F32)
            l_ref[...] = jnp.zeros(l_ref.shape, F32)
            acc_ref[...] = jnp.zeros(acc_ref.shape, F32)

        q = q_ref[...]
        logits = jnp.concatenate(
            [by_group(lambda hk: jnp.dot(q, k_pages[i][hk].astype(BF16), preferred_element_type=F32))
             + bias_ref[pl.ds(j * pg + i, 1), :] for i in range(pg)], axis=1)

        def pv_of(p):
            p = p.astype(BF16)
            page = lambda i: p[:, i * PAGE_SIZE:(i + 1) * PAGE_SIZE]
            return by_group(lambda hk: sum(
                _nt_dot(page(i), v_pages[i][hk].astype(BF16), preferred_element_type=F32) for i in range(pg)))

        flash_update(logits, pv_of)

        @pl.when(j == last)
        def _():
            rows = lambda ref: by_group(lambda hk: ref[hk:hk + 1, :]).astype(BF16).astype(F32)
            logit_new = jnp.sum(q.astype(F32) * rows(kn_ref), axis=1, keepdims=True) + bnew_ref[...]
            flash_update(logit_new, lambda p: p.astype(BF16).astype(F32) * rows(vn_ref))
            o = (acc_ref[...] / l_ref[...]).astype(BF16)
            y = x_ref[...]
            for h in range(N_HEADS):
                o_h = jnp.broadcast_to(o[h:h + 1, :], (2 * SUBLANES, HEAD_DIM))
                y = y + jnp.dot(o_h, wo_ref[h], preferred_element_type=F32)[0:1]
            out_ref[...] = y


def _dsa_sample(x_s, q_s, qi_s, wi_s, ki_new, k_new, v_new, cache_k, cache_v, cache_idx_k, page_table, w_out):
    n_dec = x_s.shape[0]
    n_pages = page_table.shape[1]
    n_pool = cache_k.shape[0]
    pg = PAGE_GROUP
    n_groups = n_pages // pg
    n_keep = min(TOPK_MAX, (n_pages * PAGE_SIZE + 1) // 4)
    def page_spec(page_shape, i, live_phase):
        def index(b, ph, j, pt):
            idle = i if live_phase == 1 else (n_groups - 1) * pg + i
            return (pt[b, jnp.where(ph == live_phase, j * pg + i, idle)],) + (0,) * len(page_shape)
        return pl.BlockSpec((None,) + page_shape, index)

    idx_page = (IDX_DIM, PAGE_SIZE)
    kv_page = (N_KV_HEADS, HEAD_DIM, PAGE_SIZE)
    cik = cache_idx_k.transpose(0, 2, 1)
    ck = cache_k.transpose(0, 2, 3, 1)
    cv = cache_v.transpose(0, 2, 3, 1)

    per_seq = lambda *shape: pl.BlockSpec((None,) + shape, lambda b, ph, j, pt: (b,) + (0,) * len(shape))
    grid_spec = pltpu.PrefetchScalarGridSpec(
        num_scalar_prefetch=1,
        grid=(n_dec, 2, n_groups),
        in_specs=([page_spec(idx_page, i, 0) for i in range(pg)]
                  + [page_spec(kv_page, i, 1) for i in range(pg)]
                  + [page_spec(kv_page, i, 1) for i in range(pg)]
                  + [per_seq(IDX_HEADS, IDX_DIM), per_seq(IDX_HEADS, 1), per_seq(1, IDX_DIM),
                     per_seq(N_HEADS, HEAD_DIM), per_seq(N_KV_HEADS, HEAD_DIM), per_seq(N_KV_HEADS, HEAD_DIM),
                     per_seq(1, D_MODEL),
                     pl.BlockSpec((N_HEADS, HEAD_DIM, D_MODEL), lambda b, ph, j, pt: (0, 0, 0))]),
        out_specs=per_seq(1, D_MODEL),
        scratch_shapes=[pltpu.VMEM((n_pages, PAGE_SIZE), F32), pltpu.VMEM((n_pages, PAGE_SIZE), F32),
                        pltpu.VMEM((1, 1), F32), pltpu.VMEM((N_HEADS, 1), F32), pltpu.VMEM((N_HEADS, 1), F32),
                        pltpu.VMEM((N_HEADS, HEAD_DIM), F32)],
    )
    out = pl.pallas_call(
        functools.partial(_dsa_sample_kernel, n_keep=n_keep, n_pages=n_pages),
        grid_spec=grid_spec,
        out_shape=jax.ShapeDtypeStruct((n_dec, 1, D_MODEL), F32),
        compiler_params=pltpu.CompilerParams(
            dimension_semantics=("arbitrary", "arbitrary", "arbitrary"), vmem_limit_bytes=VMEM_LIMIT),
        name="dsa_sample",
    )(page_table, *([cik] * pg), *([ck] * pg), *([cv] * pg),
      qi_s, wi_s.reshape(n_dec, IDX_HEADS, 1), ki_new.reshape(n_dec, 1, IDX_DIM), q_s,
      k_new.reshape(n_dec, N_KV_HEADS, HEAD_DIM), v_new.reshape(n_dec, N_KV_HEADS, HEAD_DIM),
      x_s.reshape(n_dec, 1, D_MODEL), w_out.astype(BF16).reshape(N_HEADS, HEAD_DIM, D_MODEL))
    return out.reshape(n_dec, D_MODEL)


def kernel(x_prompt, x_sample, cache_k, cache_v, cache_idx_k, page_table, norm_mix_g, norm_ffn_g,
           gmlp_w_in, gmlp_v_g, gmlp_w_s, gmlp_b_s, gmlp_w_out,
           att_w_in, att_q_norm_g, att_k_norm_g, att_idx_k_norm_g, att_w_out,
           peer_w_q, peer_sub_k1, peer_sub_k2, peer_u, peer_v):
    n_batch, seq, _ = x_prompt.shape
    n_dec, dec_seq, _ = x_sample.shape
    assert dec_seq == 1 and seq % DSA_CLASS == 0 and (n_batch * seq) % ROW_BLOCK == 0
    n_prompt = n_batch * seq
    n_tok = n_prompt + n_dec
    n_rows = n_prompt + ROW_BLOCK
    assert n_dec <= ROW_BLOCK

    def peer(x, layer):
        return _peer_layer(x, norm_ffn_g[layer], peer_w_q[layer], peer_sub_k1[layer], peer_sub_k2[layer],
                           peer_u[layer], peer_v[layer])

    x_all = jnp.concatenate([x_prompt.reshape(n_prompt, D_MODEL), x_sample.reshape(n_dec, D_MODEL),
                             jnp.zeros((n_rows - n_tok, D_MODEL), F32)], axis=0)
    x_all, v_new = _gmlp_layer(x_all, n_prompt, norm_mix_g[0], gmlp_w_in[0], gmlp_v_g[0], gmlp_w_s[0],
                               gmlp_b_s[0], gmlp_w_out[0])
    x_all = peer(x_all, 0)
    pos = jnp.concatenate([jnp.tile(jnp.arange(seq), n_batch), jnp.full((n_rows - n_prompt,), PAST_LEN)])
    q, k, v, qi, ki, wi, new_k, new_v, new_ki = _att_project(
        x_all, pos, norm_mix_g[1], att_w_in[0], att_q_norm_g[0], att_k_norm_g[0], att_idx_k_norm_g[0])
    smp = slice(n_prompt, n_tok)
    heads = lambda a: a[:, smp].transpose(1, 0, 2)
    xs = _dsa_sample(x_all[smp], heads(q), heads(qi), wi[smp], new_ki[smp], new_k[smp], new_v[smp],
                     cache_k[0], cache_v[0], cache_idx_k[0], page_table, att_w_out[0])
    x_all = _dsa_prompt(x_all, (q, k, v, qi, ki, wi), att_w_out[0], n_batch, seq)
    x_all = lax.dynamic_update_slice(x_all, xs, (n_prompt, 0))
    x_all = peer(x_all, 1)

    kv_shape = (1, n_batch, seq, N_KV_HEADS, HEAD_DIM)
    kv_s_shape = (1, n_dec, 1, N_KV_HEADS, HEAD_DIM)
    return (x_all[:n_prompt].reshape(n_batch, seq, D_MODEL), x_all[smp].reshape(n_dec, 1, D_MODEL),
            new_k[:n_prompt].reshape(kv_shape), new_v[:n_prompt].reshape(kv_shape),
            new_ki[:n_prompt].reshape(1, n_batch, seq, IDX_DIM),
            new_k[smp].reshape(kv_s_shape), new_v[smp].reshape(kv_s_shape),
            new_ki[smp].reshape(1, n_dec, 1, IDX_DIM), v_new[:n_dec].reshape(1, n_dec, 1, GMLP_WIDTH))
```
